```python
import jax
import jax.numpy as jnp
from jax import lax
import numpy as np

D_MODEL = 1024
BATCH = 4
SEQ = 4096
DEPTH = 4
DEC_BATCH = 8
DEC_SEQ = 32
PAST_LEN = 1024

CHUNK = 64
N_META = 16
WINDOW = 128
WINDOW_CHUNKS = WINDOW // CHUNK
A_HEADS = 8
A_KV_HEADS = 2
A_HEAD_DIM = 64
A_GROUP = A_HEADS // A_KV_HEADS
A_Q_W = A_HEADS * A_HEAD_DIM
A_KV_W = A_KV_HEADS * A_HEAD_DIM
B_HEADS = 4
B_KEY_DIM = 128
B_VAL_DIM = 128
B_QK_W = B_HEADS * B_KEY_DIM
B_V_W = B_HEADS * B_VAL_DIM
B_BLOCK = 16
CONV_WIDTH = 3
D_FF = 2816
FFN_RESIDUAL = 0.5
N_EVEN = (DEPTH + 1) // 2
N_ODD = DEPTH // 2
EVEN_IN_W = A_Q_W + 2 * A_KV_W + 2 * B_QK_W + 2 * B_V_W
EVEN_OUT_W = A_Q_W + B_V_W
EPS = 1e-6
MASK_VALUE = -1e30
LB_FLOOR = 1e-30

kernel_name = 'hybrid_streaming_swa_hgrn2_shortconv_step'


def rms_norm(x, gain):
    xf = x.astype(jnp.float32)
    y = xf * lax.rsqrt(jnp.mean(xf * xf, axis=-1, keepdims=True) + EPS)
    return (y * gain.astype(jnp.float32)).astype(x.dtype)


def swiglu(h, w_gate, w_up, w_down):
    return (jax.nn.silu(h @ w_gate) * (h @ w_up)) @ w_down


def hgrn_lower_bounds(logits):
    p = jax.nn.softmax(logits.astype(jnp.float32), axis=0)
    return jnp.maximum(jnp.cumsum(p, axis=0) - p[0:1], 0.0)


def sink_attention(q, k, v, sink, key_valid):
    scale = A_HEAD_DIM ** -0.5
    s = jnp.einsum('bcqkgd,bcskd->bckgqs', q.astype(jnp.float32), k.astype(jnp.float32)) * scale
    s = jnp.where(key_valid[None, :, None, None, None, :], s, MASK_VALUE)
    sk = sink.astype(jnp.float32).reshape(A_KV_HEADS, A_GROUP)[None, None, :, :, None, None]
    m = jnp.maximum(jnp.max(s, axis=-1, keepdims=True), sk)
    p = jnp.exp(s - m)
    w = p / (jnp.sum(p, axis=-1, keepdims=True) + jnp.exp(sk - m))
    o = jnp.einsum('bckgqs,bcskd->bcqkgd', w, v.astype(jnp.float32))
    return o.astype(q.dtype)


def swa_prompt(q, k, v, sink):
    bsz, length = q.shape[:2]
    pad = (-N_META) % CHUNK
    n_chunks = (length + pad) // CHUNK
    back = WINDOW_CHUNKS * CHUNK
    qb = jnp.pad(q, ((0, 0), (pad, 0), (0, 0), (0, 0))).reshape(
        bsz, n_chunks, CHUNK, A_KV_HEADS, A_GROUP, A_HEAD_DIM)

    def band(a):
        ap = jnp.pad(a, ((0, 0), (pad + back, 0), (0, 0), (0, 0))).reshape(
            bsz, n_chunks + WINDOW_CHUNKS, CHUNK, A_KV_HEADS, A_HEAD_DIM)
        return jnp.concatenate([ap[:, j:j + n_chunks] for j in range(WINDOW_CHUNKS + 1)], axis=2)

    valid = (jnp.arange((n_chunks + WINDOW_CHUNKS) * CHUNK) >= pad + back).reshape(
        n_chunks + WINDOW_CHUNKS, CHUNK)
    valid = jnp.concatenate([valid[j:j + n_chunks] for j in range(WINDOW_CHUNKS + 1)], axis=1)
    o = sink_attention(qb, band(k), band(v), sink, valid)
    return o.reshape(bsz, n_chunks * CHUNK, A_Q_W)[:, pad:]


def hgrn2_blocks(q, k, v, log_f, s0, block):
    bsz, length, heads, dk = q.shape
    dv = v.shape[-1]
    n = length // block
    causal = jnp.tril(jnp.ones((block, block), bool))[None, :, :, None, None]

    def to_blocks(a):
        return jnp.moveaxis(a.reshape(bsz, n, block, heads, a.shape[-1]), 1, 0)

    def step(state, inp):
        qc, kc, vc, gc = inp
        b = jnp.cumsum(gc, axis=1)
        o_inter = jnp.einsum('bthk,bhkv->bthv', qc * jnp.exp(b), state)
        diff = b[:, :, None] - b[:, None, :]
        decay = jnp.where(causal, jnp.exp(jnp.where(causal, diff, 0.0)), 0.0)
        scores = jnp.einsum('bthk,bshk,btshk->bhts', qc, kc, decay)
        o_intra = jnp.einsum('bhts,bshv->bthv', scores, vc)
        b_last = b[:, -1]
        state = jnp.exp(b_last)[..., None] * state + jnp.einsum(
            'bshk,bshv->bhkv', kc * jnp.exp(b_last[:, None] - b), vc)
        return state, o_inter + o_intra

    state, o = lax.scan(step, s0, (to_blocks(q), to_blocks(k), to_blocks(v), to_blocks(log_f)))
    return jnp.moveaxis(o, 0, 1).reshape(bsz, length, heads, dv), state


def even_mixer(h, w_in, w_out, sink, lb, out_gain, cache_k, cache_v, state):
    bsz, length, _ = h.shape
    widths = (A_Q_W, A_KV_W, A_KV_W, B_QK_W, B_QK_W, B_V_W, B_V_W)
    cuts = [int(c) for c in np.cumsum(widths)[:-1]]
    qa, ka, va, qb, fb, ib, gb = jnp.split(h @ w_in, cuts, axis=-1)
    qa = qa.reshape(bsz, length, A_HEADS, A_HEAD_DIM)
    ka = ka.reshape(bsz, length, A_KV_HEADS, A_HEAD_DIM)
    va = va.reshape(bsz, length, A_KV_HEADS, A_HEAD_DIM)
    if cache_k is None:
        oa = swa_prompt(qa, ka, va, sink)
        new_k, new_v = ka[:, -WINDOW:], va[:, -WINDOW:]
        s0 = jnp.zeros((bsz, B_HEADS, B_KEY_DIM, B_VAL_DIM), jnp.float32)
        block = B_BLOCK
    else:
        kk = jnp.concatenate([cache_k.astype(ka.dtype), ka], axis=1)
        vv = jnp.concatenate([cache_v.astype(va.dtype), va], axis=1)
        qq = qa.reshape(bsz, 1, length, A_KV_HEADS, A_GROUP, A_HEAD_DIM)
        oa = sink_attention(qq, kk[:, None], vv[:, None], sink,
                            jnp.ones((1, kk.shape[1]), bool)).reshape(bsz, length, A_Q_W)
        win = cache_k.shape[1]
        new_k, new_v = kk[:, -win:], vv[:, -win:]
        s0 = state.astype(jnp.float32)
        block = length
    fx = fb.astype(jnp.float32).reshape(bsz, length, B_HEADS, B_KEY_DIM)
    lbh = lb.reshape(B_HEADS, B_KEY_DIM)
    log_f = jnp.logaddexp(jax.nn.log_sigmoid(fx),
                          jnp.log(jnp.maximum(lbh, LB_FLOOR)) + jax.nn.log_sigmoid(-fx))
    k_in = (1.0 - lbh) * jax.nn.sigmoid(-fx)
    q_in = jax.nn.silu(qb.astype(jnp.float32)).reshape(bsz, length, B_HEADS, B_KEY_DIM)
    v_in = ib.astype(jnp.float32).reshape(bsz, length, B_HEADS, B_VAL_DIM)
    ob, s_new = hgrn2_blocks(q_in, k_in, v_in, log_f, s0, block)
    ob = rms_norm(ob.astype(h.dtype), out_gain) * jax.nn.silu(gb).reshape(bsz, length, B_HEADS, B_VAL_DIM)
    out = jnp.concatenate([oa, ob.reshape(bsz, length, B_V_W)], axis=-1) @ w_out
    return out, new_k, new_v, s_new.astype(h.dtype)


def odd_mixer(h, w_in, conv_w, w_out, cache):
    bsz, length, d = h.shape
    bg, cg, xv = jnp.split(h @ w_in, 3, axis=-1)
    u = cg * xv
    if cache is None:
        left = jnp.zeros((bsz, CONV_WIDTH - 1, d), u.dtype)
    else:
        left = cache.astype(u.dtype)
    up = jnp.concatenate([left, u], axis=1)
    y = up[:, 0:length] * conv_w[0]
    for j in range(1, CONV_WIDTH):
        y = y + up[:, j:j + length] * conv_w[j]
    return (bg * y) @ w_out, up[:, -(CONV_WIDTH - 1):]


def run_trunk(x, cache_k, cache_v, rec_state, conv_cache, norm_gains, w_ffn_gate, w_ffn_up, w_ffn_down,
              w_in_even, w_out_even, attn_sinks, lower_bounds, hgrn_norm_gain, w_in_odd, conv_w, w_out_odd):
    streaming = cache_k is not None
    new_k, new_v, new_rec, new_conv = [], [], [], []
    for layer in range(DEPTH):
        g = norm_gains[layer]
        x = x + FFN_RESIDUAL * rms_norm(
            swiglu(rms_norm(x, g[0]), w_ffn_gate[layer, 0], w_ffn_up[layer, 0], w_ffn_down[layer, 0]), g[1])
        h = rms_norm(x, g[2])
        j = layer // 2
        if layer % 2 == 0:
            m, k_rows, v_rows, s_new = even_mixer(
                h, w_in_even[j], w_out_even[j], attn_sinks[j], lower_bounds[j], hgrn_norm_gain[j],
                cache_k[j] if streaming else None, cache_v[j] if streaming else None,
                rec_state[j] if streaming else None)
            new_k.append(k_rows)
            new_v.append(v_rows)
            new_rec.append(s_new)
        else:
            m, c_rows = odd_mixer(h, w_in_odd[j], conv_w[j], w_out_odd[j],
                                  conv_cache[j] if streaming else None)
            new_conv.append(c_rows)
        x = x + rms_norm(m, g[3])
        x = x + FFN_RESIDUAL * rms_norm(
            swiglu(rms_norm(x, g[4]), w_ffn_gate[layer, 1], w_ffn_up[layer, 1], w_ffn_down[layer, 1]), g[5])
    return x, jnp.stack(new_k), jnp.stack(new_v), jnp.stack(new_rec), jnp.stack(new_conv)


def setup_inputs(seed: int = 0) -> dict:
    key = jax.random.key(seed)
    ks = jax.random.split(key, 19)
    nrm = lambda k, shape, s: jax.random.normal(k, shape, jnp.float32) * s
    win = min(WINDOW, PAST_LEN)
    return {
        'x_prompt': nrm(ks[0], (BATCH, SEQ, D_MODEL), 1.0),
        'x_sample': nrm(ks[1], (DEC_BATCH, DEC_SEQ, D_MODEL), 1.0),
        'cache_swa_k': nrm(ks[2], (N_EVEN, DEC_BATCH, win, A_KV_HEADS, A_HEAD_DIM), 1.0),
        'cache_swa_v': nrm(ks[3], (N_EVEN, DEC_BATCH, win, A_KV_HEADS, A_HEAD_DIM), 1.0),
        'state_hgrn': nrm(ks[4], (N_EVEN, DEC_BATCH, B_HEADS, B_KEY_DIM, B_VAL_DIM), 0.5),
        'cache_conv': nrm(ks[5], (N_ODD, DEC_BATCH, CONV_WIDTH - 1, D_MODEL), 1.0),
        'meta_tokens': nrm(ks[6], (N_META, D_MODEL), 1.0),
        'norm_gains': 1.0 + nrm(ks[7], (DEPTH, 6, D_MODEL), 0.05),
        'w_ffn_gate': nrm(ks[8], (DEPTH, 2, D_MODEL, D_FF), D_MODEL ** -0.5),
        'w_ffn_up': nrm(ks[9], (DEPTH, 2, D_MODEL, D_FF), D_MODEL ** -0.5),
        'w_ffn_down': nrm(ks[10], (DEPTH, 2, D_FF, D_MODEL), D_FF ** -0.5),
        'w_in_even': nrm(ks[11], (N_EVEN, D_MODEL, EVEN_IN_W), D_MODEL ** -0.5),
        'w_out_even': nrm(ks[12], (N_EVEN, EVEN_OUT_W, D_MODEL), EVEN_OUT_W ** -0.5),
        'attn_sinks': nrm(ks[13], (N_EVEN, A_HEADS), 0.5),
        'hgrn_lb_logits': 1.0 + nrm(ks[14], (N_EVEN, B_HEADS * B_KEY_DIM), 0.1),
        'hgrn_norm_gain': 1.0 + nrm(ks[15], (N_EVEN, B_HEADS, B_VAL_DIM), 0.05),
        'w_in_odd': nrm(ks[16], (N_ODD, D_MODEL, 3 * D_MODEL), D_MODEL ** -0.5),
        'conv_w': nrm(ks[17], (N_ODD, CONV_WIDTH, D_MODEL), CONV_WIDTH ** -0.5),
        'w_out_odd': nrm(ks[18], (N_ODD, D_MODEL, D_MODEL), D_MODEL ** -0.5),
    }


def reference(x_prompt, x_sample, cache_swa_k, cache_swa_v, state_hgrn, cache_conv, meta_tokens, norm_gains,
              w_ffn_gate, w_ffn_up, w_ffn_down, w_in_even, w_out_even, attn_sinks, hgrn_lb_logits,
              hgrn_norm_gain, w_in_odd, conv_w, w_out_odd):
    lower_bounds = hgrn_lower_bounds(hgrn_lb_logits)
    bsz = x_prompt.shape[0]
    meta = jnp.broadcast_to(meta_tokens[None].astype(x_prompt.dtype), (bsz, N_META, D_MODEL))
    xp = jnp.concatenate([meta, x_prompt], axis=1)
    yp, kp, vp, sp, cp = run_trunk(xp, None, None, None, None, norm_gains, w_ffn_gate, w_ffn_up, w_ffn_down,
                                   w_in_even, w_out_even, attn_sinks, lower_bounds, hgrn_norm_gain,
                                   w_in_odd, conv_w, w_out_odd)
    y_prompt = yp[:, N_META:]
    y_sample, ks_, vs_, ss_, cs_ = run_trunk(x_sample, cache_swa_k, cache_swa_v, state_hgrn, cache_conv,
                                             norm_gains, w_ffn_gate, w_ffn_up, w_ffn_down, w_in_even,
                                             w_out_even, attn_sinks, lower_bounds, hgrn_norm_gain,
                                             w_in_odd, conv_w, w_out_odd)
    return (y_prompt, y_sample, kp, vp, sp, cp, ks_, vs_, ss_, cs_)
```

```python
import functools

import jax
import jax.numpy as jnp
from jax import lax
from jax.experimental import pallas as pl
from jax.experimental.pallas import tpu as pltpu

F32 = jnp.float32
BF16 = jnp.bfloat16

D_MODEL = 1024
BATCH = 4
SEQ = 4096
DEPTH = 4
DEC_BATCH = 8
DEC_SEQ = 32
CHUNK = 64
N_META = 16
WINDOW = 128
WINDOW_CHUNKS = WINDOW // CHUNK
A_HEADS = 8
A_KV_HEADS = 2
A_HEAD_DIM = 64
A_Q_W = A_HEADS * A_HEAD_DIM
A_KV_W = A_KV_HEADS * A_HEAD_DIM
B_HEADS = 4
B_KEY_DIM = 128
B_VAL_DIM = 128
B_QK_W = B_HEADS * B_KEY_DIM
B_V_W = B_HEADS * B_VAL_DIM
B_BLOCK = 16
CONV_WIDTH = 3
D_FF = 2816
FFN_RESIDUAL = 0.5
N_EVEN = (DEPTH + 1) // 2
N_ODD = DEPTH // 2
EVEN_IN_W = A_Q_W + 2 * A_KV_W + 2 * B_QK_W + 2 * B_V_W
EPS = 1e-6
MASK_VALUE = -1e30
LB_FLOOR = 1e-30

PAD = (-N_META) % CHUNK
LP = PAD + N_META + SEQ
N_CHUNKS = LP // CHUNK
P_ROWS = BATCH * LP
S_ROWS = DEC_BATCH * DEC_SEQ
N_ROWS = P_ROWS + S_ROWS
S_BLOCK0 = P_ROWS // DEC_SEQ

ROW_TILE = 512
FF_CHUNK = 256
HGRN_TILE = 832
ODD_TILE = 640
HALO = 8
VMEM_LIMIT = 56 * 1024 * 1024

assert N_ROWS % ROW_TILE == 0 and D_FF % FF_CHUNK == 0
assert LP % HGRN_TILE == 0 and HGRN_TILE % B_BLOCK == 0 and DEC_SEQ % B_BLOCK == 0
assert P_ROWS % ODD_TILE == 0 and ODD_TILE % HALO == 0 and P_ROWS % DEC_SEQ == 0


def _params(*sem):
    return pltpu.CompilerParams(dimension_semantics=sem, vmem_limit_bytes=VMEM_LIMIT)


def _rms(x, gain):
    return x * lax.rsqrt(jnp.mean(x * x, axis=-1, keepdims=True) + EPS) * gain


def _silu(x):
    return x * jax.nn.sigmoid(x)


def _dot(a, b):
    return jnp.dot(a, b, preferred_element_type=F32)


def _dot_nt(a, b):
    return lax.dot_general(a, b, (((1,), (1,)), ((), ())), preferred_element_type=F32)


def _dot_tn(a, b):
    return lax.dot_general(a, b, (((0,), (0,)), ((), ())), preferred_element_type=F32)


def _is_pad_row(row0, rows):
    r = row0 + lax.broadcasted_iota(jnp.int32, (rows, 1), 0)
    pad = None
    for b in range(BATCH):
        hit = jnp.logical_and(r >= b * LP, r < b * LP + PAD)
        pad = hit if pad is None else jnp.logical_or(pad, hit)
    return pad


def _ffn_body(x_ref, g_ref, wg_ref, wu_ref, wd_ref, o_ref, *, pre, post):
    x = x_ref[...]
    h = _rms(x, g_ref[pre:pre + 1, :]).astype(BF16)
    acc = jnp.zeros(x.shape, F32)
    for c in range(D_FF // FF_CHUNK):
        cols = slice(c * FF_CHUNK, (c + 1) * FF_CHUNK)
        gate = _dot(h, wg_ref[:, cols])
        up = _dot(h, wu_ref[:, cols])
        act = (_silu(gate) * up).astype(BF16)
        acc = acc + _dot(act, wd_ref[cols, :])
    o_ref[...] = x + FFN_RESIDUAL * _rms(acc, g_ref[post:post + 1, :])


def _ffn(x, gains, wg, wu, wd, layer, which):
    pre, post = (0, 1) if which == 0 else (4, 5)
    row = lambda i: (i, 0)
    whole = lambda i: (0, 0)
    wsel = lambda i: (layer, which, 0, 0)
    resident = pl.Buffered(1)
    return pl.pallas_call(
        functools.partial(_ffn_body, pre=pre, post=post),
        out_shape=jax.ShapeDtypeStruct((N_ROWS, D_MODEL), F32),
        grid=(N_ROWS // ROW_TILE,),
        in_specs=[
            pl.BlockSpec((ROW_TILE, D_MODEL), row),
            pl.BlockSpec((6, D_MODEL), whole),
            pl.BlockSpec((None, None, D_MODEL, D_FF), wsel, pipeline_mode=resident),
            pl.BlockSpec((None, None, D_MODEL, D_FF), wsel, pipeline_mode=resident),
            pl.BlockSpec((None, None, D_FF, D_MODEL), wsel, pipeline_mode=resident),
        ],
        out_specs=pl.BlockSpec((ROW_TILE, D_MODEL), row),
        input_output_aliases={0: 0},
        compiler_params=_params("parallel"),
        name=f"ffn_l{layer}_{which}",
    )(x, gains, wg, wu, wd)


def _lower_bound(lbl_ref, j):
    rows = [lbl_ref[i:i + 1, :] for i in range(N_EVEN)]
    top = functools.reduce(jnp.maximum, rows)
    e = [jnp.exp(r - top) for r in rows]
    total = functools.reduce(lambda a, b: a + b, e)
    p = [v / total for v in e]
    cum = functools.reduce(lambda a, b: a + b, p[:j + 1])
    return jnp.maximum(cum - p[0], 0.0)


def _block_cumsum(g):
    pos = lax.broadcasted_iota(jnp.int32, g.shape, 0) & (B_BLOCK - 1)
    s = 1
    while s < B_BLOCK:
        g = g + jnp.where(pos >= s, pltpu.roll(g, s, 0), 0.0)
        s *= 2
    return g


def _even_in_body(x_ref, g_ref, w_ref, lbl_ref, q_ref, k_ref, v_ref,
                  hq_ref, hk_ref, hb_ref, hv_ref, hg_ref, *, j):
    h = _rms(x_ref[...], g_ref[2:3, :]).astype(BF16)
    c0 = 0
    q_ref[...] = (_dot(h, w_ref[:, c0:c0 + A_Q_W]) * (A_HEAD_DIM ** -0.5)).astype(BF16)
    c0 += A_Q_W
    kv = _dot(h, w_ref[:, c0:c0 + 2 * A_KV_W])
    k_ref[...] = kv[:, :A_KV_W]
    v_ref[...] = kv[:, A_KV_W:]
    c0 += 2 * A_KV_W
    hq_ref[...] = _silu(_dot(h, w_ref[:, c0:c0 + B_QK_W]))
    c0 += B_QK_W
    fx = _dot(h, w_ref[:, c0:c0 + B_QK_W])
    c0 += B_QK_W
    lb = _lower_bound(lbl_ref, j)
    soft = jnp.log1p(jnp.exp(-jnp.abs(fx)))
    ls_pos = jnp.minimum(fx, 0.0) - soft
    ls_neg = jnp.minimum(-fx, 0.0) - soft
    other = jnp.log(jnp.maximum(lb, LB_FLOOR)) + ls_neg
    log_f = jnp.maximum(ls_pos, other) + jnp.log1p(jnp.exp(-jnp.abs(ls_pos - other)))
    hk_ref[...] = (1.0 - lb) * jax.nn.sigmoid(-fx)
    hb_ref[...] = _block_cumsum(log_f)
    hv_ref[...] = _dot(h, w_ref[:, c0:c0 + B_V_W]).astype(BF16)
    c0 += B_V_W
    hg_ref[...] = _silu(_dot(h, w_ref[:, c0:c0 + B_V_W]))


def _even_in(x, gains, w_in, lb_logits, j):
    row = lambda i: (i, 0)
    whole = lambda i: (0, 0)
    wide = lambda w, dt: jax.ShapeDtypeStruct((N_ROWS, w), dt)
    spec = lambda w: pl.BlockSpec((ROW_TILE, w), row)
    return pl.pallas_call(
        functools.partial(_even_in_body, j=j),
        out_shape=(wide(A_Q_W, BF16), wide(A_KV_W, F32), wide(A_KV_W, F32),
                   wide(B_QK_W, F32), wide(B_QK_W, F32), wide(B_QK_W, F32),
                   wide(B_V_W, BF16), wide(B_V_W, F32)),
        grid=(N_ROWS // ROW_TILE,),
        in_specs=[
            pl.BlockSpec((ROW_TILE, D_MODEL), row),
            pl.BlockSpec((6, D_MODEL), whole),
            pl.BlockSpec((None, D_MODEL, EVEN_IN_W), lambda i: (j, 0, 0),
                         pipeline_mode=pl.Buffered(1)),
            pl.BlockSpec((N_EVEN, B_QK_W), whole),
        ],
        out_specs=(spec(A_Q_W), spec(A_KV_W), spec(A_KV_W), spec(B_QK_W), spec(B_QK_W),
                   spec(B_QK_W), spec(B_V_W), spec(B_V_W)),
        compiler_params=_params("parallel"),
        name=f"even_in_{j}",
    )(x, gains, w_in, lb_logits)


def _half_lane_variants(a):
    lane = lax.broadcasted_iota(jnp.int32, a.shape, 1)
    low = lane < A_HEAD_DIM
    swapped = pltpu.roll(a, A_HEAD_DIM, 1)
    zero = jnp.zeros_like(a)
    return (jnp.where(low, a, zero).astype(BF16), jnp.where(low, zero, swapped).astype(BF16),
            jnp.where(low, swapped, zero).astype(BF16), jnp.where(low, zero, a).astype(BF16))


def _attend_pair(q2, ka, kb, va, vb, valid, sink_a, sink_b):
    out = None
    for keys, vals, sink in ((ka, va, sink_a), (kb, vb, sink_b)):
        s = _dot_nt(q2, keys)
        if valid is not None:
            s = jnp.where(valid, s, MASK_VALUE)
        m = jnp.maximum(jnp.max(s, axis=-1, keepdims=True), sink)
        p = jnp.exp(s - m)
        denom = jnp.sum(p, axis=-1, keepdims=True) + jnp.exp(sink - m)
        o = _dot((p / denom).astype(BF16), vals)
        out = o if out is None else out + o
    return out


def _attn_prompt_body(sink_ref, q_ref, k_ref, v_ref, o_ref, kk_ref, vv_ref):
    span = (WINDOW_CHUNKS + 1) * CHUNK

    def fill(c, carry):
        rows = pl.ds(pl.multiple_of(c * CHUNK, CHUNK), CHUNK)
        for i, var in enumerate(_half_lane_variants(k_ref[rows, :])):
            kk_ref[i, rows, :] = var
        for i, var in enumerate(_half_lane_variants(v_ref[rows, :])):
            vv_ref[i, rows, :] = var
        return carry

    lax.fori_loop(0, N_CHUNKS, fill, 0)

    def chunk(c, carry):
        r0 = pl.multiple_of(c * CHUNK, CHUNK)
        s0 = pl.multiple_of(jnp.maximum(c - WINDOW_CHUNKS, 0) * CHUNK, CHUNK)
        kpos = s0 + lax.broadcasted_iota(jnp.int32, (1, span), 1)
        valid = jnp.logical_and(kpos >= PAD, kpos < r0 + CHUNK)
        for pair in range(A_HEADS // 2):
            kvh = (2 * pair) // (A_HEADS // A_KV_HEADS)
            cols = slice(pair * 128, (pair + 1) * 128)
            out = _attend_pair(
                q_ref[pl.ds(r0, CHUNK), cols],
                kk_ref[2 * kvh, pl.ds(s0, span), :], kk_ref[2 * kvh + 1, pl.ds(s0, span), :],
                vv_ref[2 * kvh, pl.ds(s0, span), :], vv_ref[2 * kvh + 1, pl.ds(s0, span), :],
                valid, sink_ref[2 * pair], sink_ref[2 * pair + 1])
            o_ref[pl.ds(r0, CHUNK), cols] = out.astype(BF16)
        return carry

    lax.fori_loop(0, N_CHUNKS, chunk, 0)


def _attn_prompt(sink, q, k, v):
    seq = lambda b: (b, 0)
    return pl.pallas_call(
        _attn_prompt_body,
        out_shape=jax.ShapeDtypeStruct((N_ROWS, A_Q_W), BF16),
        grid=(BATCH,),
        in_specs=[
            pl.BlockSpec(memory_space=pltpu.SMEM),
            pl.BlockSpec((LP, A_Q_W), seq),
            pl.BlockSpec((LP, A_KV_W), seq),
            pl.BlockSpec((LP, A_KV_W), seq),
        ],
        out_specs=pl.BlockSpec((LP, A_Q_W), seq),
        scratch_shapes=[pltpu.VMEM((4, LP, A_KV_W), BF16), pltpu.VMEM((4, LP, A_KV_W), BF16)],
        compiler_params=_params("parallel"),
        name="attn_prompt",
    )(sink, q, k, v)


def _attn_sample_body(sink_ref, q_ref, k_ref, v_ref, ck_ref, cv_ref, o_in_ref, o_ref):
    del o_in_ref
    kvar = _half_lane_variants(jnp.concatenate([ck_ref[...], k_ref[...]], axis=0))
    vvar = _half_lane_variants(jnp.concatenate([cv_ref[...], v_ref[...]], axis=0))
    for pair in range(A_HEADS // 2):
        kvh = (2 * pair) // (A_HEADS // A_KV_HEADS)
        cols = slice(pair * 128, (pair + 1) * 128)
        out = _attend_pair(q_ref[:, cols], kvar[2 * kvh], kvar[2 * kvh + 1],
                           vvar[2 * kvh], vvar[2 * kvh + 1], None,
                           sink_ref[2 * pair], sink_ref[2 * pair + 1])
        o_ref[:, cols] = out.astype(BF16)


def _attn_sample(sink, q, k, v, cache_k, cache_v, o_prompt):
    win = cache_k.shape[1]
    new = lambda b: (S_BLOCK0 + b, 0)
    old = lambda b: (b, 0, 0)
    return pl.pallas_call(
        _attn_sample_body,
        out_shape=jax.ShapeDtypeStruct((N_ROWS, A_Q_W), BF16),
        grid=(DEC_BATCH,),
        in_specs=[
            pl.BlockSpec(memory_space=pltpu.SMEM),
            pl.BlockSpec((DEC_SEQ, A_Q_W), new),
            pl.BlockSpec((DEC_SEQ, A_KV_W), new),
            pl.BlockSpec((DEC_SEQ, A_KV_W), new),
            pl.BlockSpec((None, win, A_KV_W), old),
            pl.BlockSpec((None, win, A_KV_W), old),
            pl.BlockSpec(memory_space=pl.ANY),
        ],
        out_specs=pl.BlockSpec((DEC_SEQ, A_Q_W), new),
        input_output_aliases={6: 0},
        compiler_params=_params("parallel"),
        name="attn_sample",
    )(sink, q, k, v, cache_k, cache_v, o_prompt)


def _hgrn_block(rows, hq_ref, hk_ref, hb_ref, hv_ref, hg_ref, gain_ref, o_ref, st_ref, sel, ones):
    for hd in range(B_HEADS):
        cols = slice(hd * B_KEY_DIM, (hd + 1) * B_KEY_DIM)
        q = hq_ref[rows, cols]
        k = hk_ref[rows, cols]
        b = hb_ref[rows, cols]
        v = hv_ref[rows, cols]
        b_last = b[B_BLOCK - 1:B_BLOCK, :]
        state_t = st_ref[hd]
        o = _dot_nt((q * jnp.exp(b)).astype(BF16), state_t.astype(BF16))
        pair = jnp.concatenate(
            [q[t:t + 1, :] * k * jnp.exp(jnp.minimum(b[t:t + 1, :] - b, 0.0))
             for t in range(B_BLOCK)], axis=0).astype(BF16)
        scores = _dot(pair, ones)
        weighted = (scores * jnp.tile(v.astype(F32), (B_BLOCK, 1))).astype(BF16)
        o = o + _dot(sel, weighted)
        k_dec = (k * jnp.exp(b_last - b)).astype(BF16)
        st_ref[hd] = state_t * jnp.exp(b_last) + _dot_tn(v, k_dec)
        normed = _rms(o, gain_ref[hd:hd + 1, :])
        o_ref[rows, cols] = (normed * hg_ref[rows, cols]).astype(BF16)


def _hgrn_consts():
    r = lax.broadcasted_iota(jnp.int32, (B_BLOCK, B_BLOCK * B_BLOCK), 0)
    c = lax.broadcasted_iota(jnp.int32, (B_BLOCK, B_BLOCK * B_BLOCK), 1)
    causal = jnp.logical_and((c // B_BLOCK) == r, (c & (B_BLOCK - 1)) <= r)
    sel = jnp.where(causal, 1.0, 0.0).astype(BF16)
    ones = jnp.ones((B_KEY_DIM, B_VAL_DIM), BF16)
    return sel, ones


def _hgrn_prompt_body(hq_ref, hk_ref, hb_ref, hv_ref, hg_ref, gain_ref, o_ref, s_ref, st_ref):
    @pl.when(pl.program_id(1) == 0)
    def _():
        st_ref[...] = jnp.zeros(st_ref.shape, F32)

    sel, ones = _hgrn_consts()

    def step(n, carry):
        rows = pl.ds(pl.multiple_of(n * B_BLOCK, B_BLOCK), B_BLOCK)
        _hgrn_block(rows, hq_ref, hk_ref, hb_ref, hv_ref, hg_ref, gain_ref, o_ref, st_ref, sel, ones)
        return carry

    lax.fori_loop(0, HGRN_TILE // B_BLOCK, step, 0)

    @pl.when(pl.program_id(1) == pl.num_programs(1) - 1)
    def _():
        for hd in range(B_HEADS):
            s_ref[hd] = st_ref[hd].T


def _hgrn_prompt(hq, hk, hb, hv, hg, gain):
    tiles = LP // HGRN_TILE
    row = lambda b, i: (b * tiles + i, 0)
    spec = pl.BlockSpec((HGRN_TILE, B_QK_W), row)
    return pl.pallas_call(
        _hgrn_prompt_body,
        out_shape=(jax.ShapeDtypeStruct((N_ROWS, B_V_W), BF16),
                   jax.ShapeDtypeStruct((BATCH, B_HEADS, B_KEY_DIM, B_VAL_DIM), F32)),
        grid=(BATCH, tiles),
        in_specs=[spec, spec, spec, spec, spec,
                  pl.BlockSpec((B_HEADS, B_VAL_DIM), lambda b, i: (0, 0))],
        out_specs=(spec, pl.BlockSpec((None, B_HEADS, B_KEY_DIM, B_VAL_DIM),
                                      lambda b, i: (b, 0, 0, 0))),
        scratch_shapes=[pltpu.VMEM((B_HEADS, B_VAL_DIM, B_KEY_DIM), F32)],
        compiler_params=_params("parallel", "arbitrary"),
        name="hgrn_prompt",
    )(hq, hk, hb, hv, hg, gain)


def _hgrn_sample_body(hq_ref, hk_ref, hb_ref, hv_ref, hg_ref, gain_ref, s0_ref, o_in_ref,
                      o_ref, s_ref, st_ref):
    del o_in_ref
    for hd in range(B_HEADS):
        st_ref[hd] = s0_ref[hd].T
    sel, ones = _hgrn_consts()
    for n in range(DEC_SEQ // B_BLOCK):
        rows = pl.ds(n * B_BLOCK, B_BLOCK)
        _hgrn_block(rows, hq_ref, hk_ref, hb_ref, hv_ref, hg_ref, gain_ref, o_ref, st_ref, sel, ones)
    for hd in range(B_HEADS):
        s_ref[hd] = st_ref[hd].T


def _hgrn_sample(hq, hk, hb, hv, hg, gain, s0, o_prompt):
    row = lambda b: (S_BLOCK0 + b, 0)
    spec = pl.BlockSpec((DEC_SEQ, B_QK_W), row)
    state = pl.BlockSpec((None, B_HEADS, B_KEY_DIM, B_VAL_DIM), lambda b: (b, 0, 0, 0))
    return pl.pallas_call(
        _hgrn_sample_body,
        out_shape=(jax.ShapeDtypeStruct((N_ROWS, B_V_W), BF16),
                   jax.ShapeDtypeStruct((DEC_BATCH, B_HEADS, B_KEY_DIM, B_VAL_DIM), F32)),
        grid=(DEC_BATCH,),
        in_specs=[spec, spec, spec, spec, spec,
                  pl.BlockSpec((B_HEADS, B_VAL_DIM), lambda b: (0, 0)),
                  state, pl.BlockSpec(memory_space=pl.ANY)],
        out_specs=(spec, state),
        scratch_shapes=[pltpu.VMEM((B_HEADS, B_VAL_DIM, B_KEY_DIM), F32)],
        input_output_aliases={7: 0},
        compiler_params=_params("parallel"),
        name="hgrn_sample",
    )(hq, hk, hb, hv, hg, gain, s0, o_prompt)


def _even_out_body(x_ref, g_ref, oa_ref, ob_ref, w_ref, o_ref):
    m = _dot(oa_ref[...], w_ref[:A_Q_W, :]) + _dot(ob_ref[...], w_ref[A_Q_W:, :])
    y = x_ref[...] + _rms(m, g_ref[3:4, :])
    pad = _is_pad_row(pl.program_id(0) * ROW_TILE, ROW_TILE)
    o_ref[...] = jnp.where(pad, 0.0, y)


def _even_out(x, gains, oa, ob, w_out, j):
    row = lambda i: (i, 0)
    return pl.pallas_call(
        _even_out_body,
        out_shape=jax.ShapeDtypeStruct((N_ROWS, D_MODEL), F32),
        grid=(N_ROWS // ROW_TILE,),
        in_specs=[
            pl.BlockSpec((ROW_TILE, D_MODEL), row),
            pl.BlockSpec((6, D_MODEL), lambda i: (0, 0)),
            pl.BlockSpec((ROW_TILE, A_Q_W), row),
            pl.BlockSpec((ROW_TILE, B_V_W), row),
            pl.BlockSpec((None, A_Q_W + B_V_W, D_MODEL), lambda i: (j, 0, 0),
                         pipeline_mode=pl.Buffered(1)),
        ],
        out_specs=pl.BlockSpec((ROW_TILE, D_MODEL), row),
        input_output_aliases={0: 0},
        compiler_params=_params("parallel"),
        name=f"even_out_{j}",
    )(x, gains, oa, ob, w_out)


def _odd_in_body(x_ref, g_ref, w_ref, bg_ref, u_ref):
    h = _rms(x_ref[...], g_ref[2:3, :]).astype(BF16)
    bg_ref[...] = _dot(h, w_ref[:, :D_MODEL])
    u_ref[...] = _dot(h, w_ref[:, D_MODEL:2 * D_MODEL]) * _dot(h, w_ref[:, 2 * D_MODEL:])


def _odd_in(x, gains, w_in, j):
    row = lambda i: (i, 0)
    out = jax.ShapeDtypeStruct((N_ROWS, D_MODEL), F32)
    return pl.pallas_call(
        _odd_in_body,
        out_shape=(out, out),
        grid=(N_ROWS // ROW_TILE,),
        in_specs=[
            pl.BlockSpec((ROW_TILE, D_MODEL), row),
            pl.BlockSpec((6, D_MODEL), lambda i: (0, 0)),
            pl.BlockSpec((None, D_MODEL, 3 * D_MODEL), lambda i: (j, 0, 0),
                         pipeline_mode=pl.Buffered(1)),
        ],
        out_specs=(pl.BlockSpec((ROW_TILE, D_MODEL), row), pl.BlockSpec((ROW_TILE, D_MODEL), row)),
        compiler_params=_params("parallel"),
        name=f"odd_in_{j}",
    )(x, gains, w_in)


def _odd_out_body(x_ref, g_ref, bg_ref, u_ref, left_ref, cw_ref, w_ref, o_ref, *, left_rows, rows,
                  prompt):
    u = u_ref[...]
    before2 = left_ref[left_rows[0]:left_rows[0] + 1, :]
    before1 = left_ref[left_rows[1]:left_rows[1] + 1, :]
    pos = lax.broadcasted_iota(jnp.int32, (rows, 1), 0)
    shift1 = jnp.where(pos == 0, before1, pltpu.roll(u, 1, 0))
    shift2 = jnp.where(pos == 0, before2, jnp.where(pos == 1, before1, pltpu.roll(u, 2, 0)))
    y = shift2 * cw_ref[0:1, :] + shift1 * cw_ref[1:2, :] + u * cw_ref[2:3, :]
    m = _dot((bg_ref[...] * y).astype(BF16), w_ref[...])
    out = x_ref[...] + _rms(m, g_ref[3:4, :])
    if prompt:
        out = jnp.where(_is_pad_row(pl.program_id(0) * rows, rows), 0.0, out)
    o_ref[...] = out


def _odd_out_prompt(x, gains, bg, u, conv_w, w_out, j):
    row = lambda i: (i, 0)
    per = ODD_TILE // HALO
    halo = lambda i: (jnp.maximum(i * per - 1, 0), 0)
    return pl.pallas_call(
        functools.partial(_odd_out_body, left_rows=(HALO - 2, HALO - 1), rows=ODD_TILE, prompt=True),
        out_shape=jax.ShapeDtypeStruct((N_ROWS, D_MODEL), F32),
        grid=(P_ROWS // ODD_TILE,),
        in_specs=[
            pl.BlockSpec((ODD_TILE, D_MODEL), row),
            pl.BlockSpec((6, D_MODEL), lambda i: (0, 0)),
            pl.BlockSpec((ODD_TILE, D_MODEL), row),
            pl.BlockSpec((ODD_TILE, D_MODEL), row),
            pl.BlockSpec((HALO, D_MODEL), halo),
            pl.BlockSpec((None, CONV_WIDTH, D_MODEL), lambda i: (j, 0, 0)),
            pl.BlockSpec((None, D_MODEL, D_MODEL), lambda i: (j, 0, 0), pipeline_mode=pl.Buffered(1)),
        ],
        out_specs=pl.BlockSpec((ODD_TILE, D_MODEL), row),
        input_output_aliases={0: 0},
        compiler_params=_params("parallel"),
        name=f"odd_out_prompt_{j}",
    )(x, gains, bg, u, u, conv_w, w_out)


def _odd_out_sample(x, gains, bg, u, cache, conv_w, w_out, j):
    row = lambda b: (S_BLOCK0 + b, 0)
    return pl.pallas_call(
        functools.partial(_odd_out_body, left_rows=(0, 1), rows=DEC_SEQ, prompt=False),
        out_shape=jax.ShapeDtypeStruct((N_ROWS, D_MODEL), F32),
        grid=(DEC_BATCH,),
        in_specs=[
            pl.BlockSpec((DEC_SEQ, D_MODEL), row),
            pl.BlockSpec((6, D_MODEL), lambda b: (0, 0)),
            pl.BlockSpec((DEC_SEQ, D_MODEL), row),
            pl.BlockSpec((DEC_SEQ, D_MODEL), row),
            pl.BlockSpec((None, CONV_WIDTH - 1, D_MODEL), lambda b: (b, 0, 0)),
            pl.BlockSpec((None, CONV_WIDTH, D_MODEL), lambda b: (j, 0, 0)),
            pl.BlockSpec((None, D_MODEL, D_MODEL), lambda b: (j, 0, 0), pipeline_mode=pl.Buffered(1)),
        ],
        out_specs=pl.BlockSpec((DEC_SEQ, D_MODEL), row),
        input_output_aliases={0: 0},
        compiler_params=_params("parallel"),
        name=f"odd_out_sample_{j}",
    )(x, gains, bg, u, cache, conv_w, w_out)


def kernel(x_prompt, x_sample, cache_swa_k, cache_swa_v, state_hgrn, cache_conv, meta_tokens, norm_gains,
           w_ffn_gate, w_ffn_up, w_ffn_down, w_in_even, w_out_even, attn_sinks, hgrn_lb_logits,
           hgrn_norm_gain, w_in_odd, conv_w, w_out_odd):
    win = cache_swa_k.shape[2]
    lead = jnp.concatenate([jnp.zeros((PAD, D_MODEL), F32), meta_tokens.astype(F32)], axis=0)
    xp = jnp.concatenate([jnp.broadcast_to(lead[None], (BATCH, PAD + N_META, D_MODEL)), x_prompt], axis=1)
    x = jnp.concatenate([xp.reshape(P_ROWS, D_MODEL), x_sample.reshape(S_ROWS, D_MODEL)], axis=0)

    wg, wu, wd = (w.astype(BF16) for w in (w_ffn_gate, w_ffn_up, w_ffn_down))
    w_in_e, w_out_e = w_in_even.astype(BF16), w_out_even.astype(BF16)
    w_in_o, w_out_o = w_in_odd.astype(BF16), w_out_odd.astype(BF16)
    cache_k = cache_swa_k.reshape(N_EVEN, DEC_BATCH, win, A_KV_W)
    cache_v = cache_swa_v.reshape(N_EVEN, DEC_BATCH, win, A_KV_W)

    k_p, v_p, s_p, c_p, k_s, v_s, s_s, c_s = ([] for _ in range(8))
    for layer in range(DEPTH):
        gains = norm_gains[layer]
        j = layer // 2
        x = _ffn(x, gains, wg, wu, wd, layer, 0)
        if layer % 2 == 0:
            q, k, v, hq, hk, hb, hv, hg = _even_in(x, gains, w_in_e, hgrn_lb_logits, j)
            oa = _attn_prompt(attn_sinks[j], q, k, v)
            oa = _attn_sample(attn_sinks[j], q, k, v, cache_k[j], cache_v[j], oa)
            ob, st_p = _hgrn_prompt(hq, hk, hb, hv, hg, hgrn_norm_gain[j])
            ob, st_s = _hgrn_sample(hq, hk, hb, hv, hg, hgrn_norm_gain[j], state_hgrn[j], ob)
            x = _even_out(x, gains, oa, ob, w_out_e, j)
            for new, tail_p, tail_s, cache in ((k, k_p, k_s, cache_k[j]), (v, v_p, v_s, cache_v[j])):
                tail_p.append(new[:P_ROWS].reshape(BATCH, LP, A_KV_HEADS, A_HEAD_DIM)[:, LP - WINDOW:])
                both = jnp.concatenate([cache, new[P_ROWS:].reshape(DEC_BATCH, DEC_SEQ, A_KV_W)], axis=1)
                tail_s.append(both[:, both.shape[1] - win:].reshape(DEC_BATCH, win, A_KV_HEADS, A_HEAD_DIM))
            s_p.append(st_p)
            s_s.append(st_s)
        else:
            bg, u = _odd_in(x, gains, w_in_o, j)
            x = _odd_out_prompt(x, gains, bg, u, conv_w, w_out_o, j)
            x = _odd_out_sample(x, gains, bg, u, cache_conv[j], conv_w, w_out_o, j)
            c_p.append(u[:P_ROWS].reshape(BATCH, LP, D_MODEL)[:, LP - (CONV_WIDTH - 1):])
            c_s.append(u[P_ROWS:].reshape(DEC_BATCH, DEC_SEQ, D_MODEL)[:, DEC_SEQ - (CONV_WIDTH - 1):])
        x = _ffn(x, gains, wg, wu, wd, layer, 1)

    y_prompt = x[:P_ROWS].reshape(BATCH, LP, D_MODEL)[:, PAD + N_META:]
    y_sample = x[P_ROWS:].reshape(DEC_BATCH, DEC_SEQ, D_MODEL)
    return (y_prompt, y_sample, jnp.stack(k_p), jnp.stack(v_p), jnp.stack(s_p), jnp.stack(c_p),
            jnp.stack(k_s), jnp.stack(v_s), jnp.stack(s_s), jnp.stack(c_s))
```

```python
import functools

import jax
import jax.numpy as jnp
from jax import lax
from jax.experimental import pallas as pl
from jax.experimental.pallas import tpu as pltpu

F32 = jnp.float32
BF16 = jnp.bfloat16

D_MODEL = 1024
BATCH = 4
SEQ = 4096
DEPTH = 4
DEC_BATCH = 8
DEC_SEQ = 32
CHUNK = 64
N_META = 16
WINDOW = 128
WINDOW_CHUNKS = WINDOW // CHUNK
A_HEADS = 8
A_KV_HEADS = 2
A_HEAD_DIM = 64
A_Q_W = A_HEADS * A_HEAD_DIM
A_KV_W = A_KV_HEADS * A_HEAD_DIM
B_HEADS = 4
B_KEY_DIM = 128
B_VAL_DIM = 128
B_QK_W = B_HEADS * B_KEY_DIM
B_V_W = B_HEADS * B_VAL_DIM
B_BLOCK = 16
CONV_WIDTH = 3
D_FF = 2816
FFN_RESIDUAL = 0.5
N_EVEN = (DEPTH + 1) // 2
N_ODD = DEPTH // 2
EVEN_IN_W = A_Q_W + 2 * A_KV_W + 2 * B_QK_W + 2 * B_V_W
EPS = 1e-6
MASK_VALUE = -1e30
LB_FLOOR = 1e-30

PAD = (-N_META) % CHUNK
LP = PAD + N_META + SEQ
N_CHUNKS = LP // CHUNK
P_ROWS = BATCH * LP
S_ROWS = DEC_BATCH * DEC_SEQ
N_ROWS = P_ROWS + S_ROWS
S_BLOCK0 = P_ROWS // DEC_SEQ

ROW_TILE = 512
FF_CHUNK = 256
HGRN_TILE = 832
HGRN_ROWS = 64
HGRN_SUB = 8
ODD_TILE = 640
HALO = 8
VMEM_LIMIT = 56 * 1024 * 1024

assert N_ROWS % ROW_TILE == 0 and D_FF % FF_CHUNK == 0
assert LP % HGRN_TILE == 0 and HGRN_TILE % HGRN_ROWS == 0 and ROW_TILE % HGRN_ROWS == 0
assert HGRN_ROWS % DEC_SEQ == 0 and DEC_SEQ % HGRN_SUB == 0
assert P_ROWS % ODD_TILE == 0 and ODD_TILE % HALO == 0 and P_ROWS % DEC_SEQ == 0


def _params(*sem):
    return pltpu.CompilerParams(dimension_semantics=sem, vmem_limit_bytes=VMEM_LIMIT)


def _rms(x, gain):
    return x * lax.rsqrt(jnp.mean(x * x, axis=-1, keepdims=True) + EPS) * gain


def _silu(x):
    return x * jax.nn.sigmoid(x)


def _dot(a, b):
    return jnp.dot(a, b, preferred_element_type=F32)


def _dot_nt(a, b):
    return lax.dot_general(a, b, (((1,), (1,)), ((), ())), preferred_element_type=F32)


def _dot_tn(a, b):
    return lax.dot_general(a, b, (((0,), (0,)), ((), ())), preferred_element_type=F32)


def _is_pad_row(row0, rows):
    r = row0 + lax.broadcasted_iota(jnp.int32, (rows, 1), 0)
    pad = None
    for b in range(BATCH):
        hit = jnp.logical_and(r >= b * LP, r < b * LP + PAD)
        pad = hit if pad is None else jnp.logical_or(pad, hit)
    return pad


def _ffn_body(x_ref, g_ref, wg_ref, wu_ref, wd_ref, o_ref, *, pre, post):
    x = x_ref[...]
    h = _rms(x, g_ref[pre:pre + 1, :]).astype(BF16)
    acc = jnp.zeros(x.shape, F32)
    for c in range(D_FF // FF_CHUNK):
        cols = slice(c * FF_CHUNK, (c + 1) * FF_CHUNK)
        gate = _dot(h, wg_ref[:, cols])
        up = _dot(h, wu_ref[:, cols])
        act = (_silu(gate) * up).astype(BF16)
        acc = acc + _dot(act, wd_ref[cols, :])
    o_ref[...] = x + FFN_RESIDUAL * _rms(acc, g_ref[post:post + 1, :])


def _ffn(x, gains, wg, wu, wd, layer, which):
    pre, post = (0, 1) if which == 0 else (4, 5)
    row = lambda i: (i, 0)
    whole = lambda i: (0, 0)
    wsel = lambda i: (layer, which, 0, 0)
    resident = pl.Buffered(1)
    return pl.pallas_call(
        functools.partial(_ffn_body, pre=pre, post=post),
        out_shape=jax.ShapeDtypeStruct((N_ROWS, D_MODEL), F32),
        grid=(N_ROWS // ROW_TILE,),
        in_specs=[
            pl.BlockSpec((ROW_TILE, D_MODEL), row),
            pl.BlockSpec((6, D_MODEL), whole),
            pl.BlockSpec((None, None, D_MODEL, D_FF), wsel, pipeline_mode=resident),
            pl.BlockSpec((None, None, D_MODEL, D_FF), wsel, pipeline_mode=resident),
            pl.BlockSpec((None, None, D_FF, D_MODEL), wsel, pipeline_mode=resident),
        ],
        out_specs=pl.BlockSpec((ROW_TILE, D_MODEL), row),
        input_output_aliases={0: 0},
        compiler_params=_params("parallel"),
        name=f"ffn_l{layer}_{which}",
    )(x, gains, wg, wu, wd)


def _lower_bound(lbl_ref, j):
    rows = [lbl_ref[i:i + 1, :] for i in range(N_EVEN)]
    top = functools.reduce(jnp.maximum, rows)
    e = [jnp.exp(r - top) for r in rows]
    total = functools.reduce(lambda a, b: a + b, e)
    p = [v / total for v in e]
    cum = functools.reduce(lambda a, b: a + b, p[:j + 1])
    return jnp.maximum(cum - p[0], 0.0)


def _span_cumsum(g, row0):
    row = row0 + lax.broadcasted_iota(jnp.int32, (g.shape[0], 1), 0)
    pos = jnp.where(row >= P_ROWS, row & (DEC_SEQ - 1), row & (HGRN_ROWS - 1))
    s = 1
    while s < HGRN_ROWS:
        g = g + jnp.where(pos >= s, pltpu.roll(g, s, 0), 0.0)
        s *= 2
    return g


def _even_in_body(x_ref, g_ref, w_ref, lbl_ref, q_ref, k_ref, v_ref,
                  hq_ref, hk_ref, hb_ref, hv_ref, hg_ref, *, j):
    h = _rms(x_ref[...], g_ref[2:3, :]).astype(BF16)
    c0 = 0
    q_ref[...] = (_dot(h, w_ref[:, c0:c0 + A_Q_W]) * (A_HEAD_DIM ** -0.5)).astype(BF16)
    c0 += A_Q_W
    kv = _dot(h, w_ref[:, c0:c0 + 2 * A_KV_W])
    k_ref[...] = kv[:, :A_KV_W]
    v_ref[...] = kv[:, A_KV_W:]
    c0 += 2 * A_KV_W
    hq_ref[...] = _silu(_dot(h, w_ref[:, c0:c0 + B_QK_W]))
    c0 += B_QK_W
    fx = _dot(h, w_ref[:, c0:c0 + B_QK_W])
    c0 += B_QK_W
    lb = _lower_bound(lbl_ref, j)
    soft = jnp.log1p(jnp.exp(-jnp.abs(fx)))
    ls_pos = jnp.minimum(fx, 0.0) - soft
    ls_neg = jnp.minimum(-fx, 0.0) - soft
    other = jnp.log(jnp.maximum(lb, LB_FLOOR)) + ls_neg
    log_f = jnp.maximum(ls_pos, other) + jnp.log1p(jnp.exp(-jnp.abs(ls_pos - other)))
    hk_ref[...] = (1.0 - lb) * jax.nn.sigmoid(-fx)
    hb_ref[...] = _span_cumsum(log_f, pl.program_id(0) * ROW_TILE)
    hv_ref[...] = _dot(h, w_ref[:, c0:c0 + B_V_W]).astype(BF16)
    c0 += B_V_W
    hg_ref[...] = _silu(_dot(h, w_ref[:, c0:c0 + B_V_W]))


def _even_in(x, gains, w_in, lb_logits, j):
    row = lambda i: (i, 0)
    whole = lambda i: (0, 0)
    wide = lambda w, dt: jax.ShapeDtypeStruct((N_ROWS, w), dt)
    spec = lambda w: pl.BlockSpec((ROW_TILE, w), row)
    return pl.pallas_call(
        functools.partial(_even_in_body, j=j),
        out_shape=(wide(A_Q_W, BF16), wide(A_KV_W, F32), wide(A_KV_W, F32),
                   wide(B_QK_W, F32), wide(B_QK_W, F32), wide(B_QK_W, F32),
                   wide(B_V_W, BF16), wide(B_V_W, F32)),
        grid=(N_ROWS // ROW_TILE,),
        in_specs=[
            pl.BlockSpec((ROW_TILE, D_MODEL), row),
            pl.BlockSpec((6, D_MODEL), whole),
            pl.BlockSpec((None, D_MODEL, EVEN_IN_W), lambda i: (j, 0, 0),
                         pipeline_mode=pl.Buffered(1)),
            pl.BlockSpec((N_EVEN, B_QK_W), whole),
        ],
        out_specs=(spec(A_Q_W), spec(A_KV_W), spec(A_KV_W), spec(B_QK_W), spec(B_QK_W),
                   spec(B_QK_W), spec(B_V_W), spec(B_V_W)),
        compiler_params=_params("parallel"),
        name=f"even_in_{j}",
    )(x, gains, w_in, lb_logits)


def _half_lane_variants(a):
    lane = lax.broadcasted_iota(jnp.int32, a.shape, 1)
    low = lane < A_HEAD_DIM
    swapped = pltpu.roll(a, A_HEAD_DIM, 1)
    zero = jnp.zeros_like(a)
    return (jnp.where(low, a, zero).astype(BF16), jnp.where(low, zero, swapped).astype(BF16),
            jnp.where(low, swapped, zero).astype(BF16), jnp.where(low, zero, a).astype(BF16))


def _softmax_pv(s, valid, sink, vals):
    if valid is not None:
        s = jnp.where(valid, s, MASK_VALUE)
    m = jnp.maximum(jnp.max(s, axis=-1, keepdims=True), sink)
    p = jnp.exp(s - m)
    denom = jnp.sum(p, axis=-1, keepdims=True) + jnp.exp(sink - m)
    return _dot((p / denom).astype(BF16), vals)


def _group_sinks(sink_ref, kvh, rows):
    first = lax.broadcasted_iota(jnp.int32, (2 * rows, 1), 0) < rows
    base = kvh * (A_HEADS // A_KV_HEADS)
    return (jnp.where(first, sink_ref[base], sink_ref[base + 2]),
            jnp.where(first, sink_ref[base + 1], sink_ref[base + 3]))


def _attend_group(q, kvh, keys, vals, valid, sinks):
    rows = q.shape[0]
    c0 = kvh * 256
    q2 = jnp.concatenate([q[:, c0:c0 + 128], q[:, c0 + 128:c0 + 256]], axis=0)
    out = (_softmax_pv(_dot_nt(q2, keys[2 * kvh]), valid, sinks[0], vals[2 * kvh])
           + _softmax_pv(_dot_nt(q2, keys[2 * kvh + 1]), valid, sinks[1], vals[2 * kvh + 1]))
    return jnp.concatenate([out[:rows], out[rows:]], axis=1)


def _attn_prompt_body(sink_ref, q_ref, k_ref, v_ref, o_ref, kk_ref, vv_ref):
    span = (WINDOW_CHUNKS + 1) * CHUNK

    def fill(c, carry):
        rows = pl.ds(pl.multiple_of(c * CHUNK, CHUNK), CHUNK)
        kvar = _half_lane_variants(k_ref[rows, :])
        vvar = _half_lane_variants(v_ref[rows, :])
        for i in range(4):
            kk_ref[i, rows, :] = kvar[i]
            vv_ref[i, rows, :] = vvar[i]
        return carry

    lax.fori_loop(0, N_CHUNKS, fill, 0)
    sinks = [_group_sinks(sink_ref, kvh, CHUNK) for kvh in range(A_KV_HEADS)]

    def chunk(c, carry):
        r0 = pl.multiple_of(c * CHUNK, CHUNK)
        s0 = pl.multiple_of(jnp.maximum(c - WINDOW_CHUNKS, 0) * CHUNK, CHUNK)
        kpos = s0 + lax.broadcasted_iota(jnp.int32, (1, span), 1)
        valid = jnp.logical_and(kpos >= PAD, kpos < r0 + CHUNK)
        q = q_ref[pl.ds(r0, CHUNK), :]
        keys = [kk_ref[i, pl.ds(s0, span), :] for i in range(4)]
        vals = [vv_ref[i, pl.ds(s0, span), :] for i in range(4)]
        out = [_attend_group(q, kvh, keys, vals, valid, sinks[kvh]) for kvh in range(A_KV_HEADS)]
        o_ref[pl.ds(r0, CHUNK), :] = jnp.concatenate(out, axis=1).astype(BF16)
        return carry

    lax.fori_loop(0, N_CHUNKS, chunk, 0)


def _attn_prompt(sink, q, k, v):
    seq = lambda b: (b, 0)
    return pl.pallas_call(
        _attn_prompt_body,
        out_shape=jax.ShapeDtypeStruct((N_ROWS, A_Q_W), BF16),
        grid=(BATCH,),
        in_specs=[
            pl.BlockSpec(memory_space=pltpu.SMEM),
            pl.BlockSpec((LP, A_Q_W), seq),
            pl.BlockSpec((LP, A_KV_W), seq),
            pl.BlockSpec((LP, A_KV_W), seq),
        ],
        out_specs=pl.BlockSpec((LP, A_Q_W), seq),
        scratch_shapes=[pltpu.VMEM((4, LP, A_KV_W), BF16), pltpu.VMEM((4, LP, A_KV_W), BF16)],
        compiler_params=_params("parallel"),
        name="attn_prompt",
    )(sink, q, k, v)


def _attn_sample_body(sink_ref, q_ref, k_ref, v_ref, ck_ref, cv_ref, o_in_ref, o_ref):
    del o_in_ref
    keys = _half_lane_variants(jnp.concatenate([ck_ref[...], k_ref[...]], axis=0))
    vals = _half_lane_variants(jnp.concatenate([cv_ref[...], v_ref[...]], axis=0))
    q = q_ref[...]
    out = [_attend_group(q, kvh, keys, vals, None, _group_sinks(sink_ref, kvh, DEC_SEQ))
           for kvh in range(A_KV_HEADS)]
    o_ref[...] = jnp.concatenate(out, axis=1).astype(BF16)


def _attn_sample(sink, q, k, v, cache_k, cache_v, o_prompt):
    win = cache_k.shape[1]
    new = lambda b: (S_BLOCK0 + b, 0)
    old = lambda b: (b, 0, 0)
    return pl.pallas_call(
        _attn_sample_body,
        out_shape=jax.ShapeDtypeStruct((N_ROWS, A_Q_W), BF16),
        grid=(DEC_BATCH,),
        in_specs=[
            pl.BlockSpec(memory_space=pltpu.SMEM),
            pl.BlockSpec((DEC_SEQ, A_Q_W), new),
            pl.BlockSpec((DEC_SEQ, A_KV_W), new),
            pl.BlockSpec((DEC_SEQ, A_KV_W), new),
            pl.BlockSpec((None, win, A_KV_W), old),
            pl.BlockSpec((None, win, A_KV_W), old),
            pl.BlockSpec(memory_space=pl.ANY),
        ],
        out_specs=pl.BlockSpec((DEC_SEQ, A_Q_W), new),
        input_output_aliases={6: 0},
        compiler_params=_params("parallel"),
        name="attn_sample",
    )(sink, q, k, v, cache_k, cache_v, o_prompt)


def _group_row(a, group, row):
    spans = a.shape[0] // group
    picked = a.reshape(spans, group, a.shape[1])[:, row:row + 1, :]
    return jnp.broadcast_to(picked, (spans, group, a.shape[1])).reshape(a.shape)


def _hgrn_pairs(q, k, c):
    parts = []
    for s in range(HGRN_SUB):
        decay = jnp.exp(jnp.minimum(c - _group_row(c, HGRN_SUB, s), 0.0))
        parts.append((q * _group_row(k, HGRN_SUB, s) * decay).astype(BF16))
    return jnp.concatenate(parts, axis=1)


def _hgrn_pick(rows):
    shape = (HGRN_SUB * B_KEY_DIM, rows)
    block = lax.broadcasted_iota(jnp.int32, shape, 0) // B_KEY_DIM
    col = lax.broadcasted_iota(jnp.int32, shape, 1) & (HGRN_SUB - 1)
    return jnp.where(block == col, 1.0, 0.0).astype(BF16)


def _hgrn_head(q, k, c, v, state_t, intra):
    rows = q.shape[0]
    t_idx = lax.broadcasted_iota(jnp.int32, (rows, rows), 0)
    u_idx = lax.broadcasted_iota(jnp.int32, (rows, rows), 1)
    apart = t_idx ^ u_idx
    in_group = jnp.logical_and(apart < HGRN_SUB, u_idx <= t_idx)
    scores = jnp.where(in_group, intra, 0.0)
    pos = lax.broadcasted_iota(jnp.int32, q.shape, 0)
    w = HGRN_SUB
    while w < rows:
        rho = _group_row(c, 2 * w, w - 1)
        upper = (pos & w) != 0
        q_up = jnp.where(upper, q * jnp.exp(jnp.minimum(c - rho, 0.0)), 0.0).astype(BF16)
        k_low = jnp.where(upper, 0.0, k * jnp.exp(jnp.minimum(rho - c, 0.0))).astype(BF16)
        level = _dot_nt(q_up, k_low)
        scores = scores + (level if 2 * w == rows else jnp.where(apart < 2 * w, level, 0.0))
        w *= 2
    o = _dot(scores.astype(BF16), v) + _dot_nt((q * jnp.exp(c)).astype(BF16), state_t.astype(BF16))
    c_last = c[rows - 1:rows, :]
    k_dec = (k * jnp.exp(c_last - c)).astype(BF16)
    return o, state_t * jnp.exp(c_last) + _dot_tn(v, k_dec)


def _hgrn_tile(rows, n, hq_ref, hk_ref, hc_ref, hv_ref, hg_ref, gain_ref, o_ref, st_ref, pick):
    q, k, c, v, g = hq_ref[rows, :], hk_ref[rows, :], hc_ref[rows, :], hv_ref[rows, :], hg_ref[rows, :]
    states = [st_ref[hd] for hd in range(B_HEADS)]
    heads = [slice(hd * B_KEY_DIM, (hd + 1) * B_KEY_DIM) for hd in range(B_HEADS)]
    pairs = jnp.concatenate([_hgrn_pairs(q[:, h], k[:, h], c[:, h]) for h in heads], axis=0)
    intra = _dot(pairs, pick)
    outs = []
    for hd, h in enumerate(heads):
        o, states[hd] = _hgrn_head(q[:, h], k[:, h], c[:, h], v[:, h], states[hd],
                                   intra[hd * n:(hd + 1) * n, :])
        outs.append(_rms(o, gain_ref[hd:hd + 1, :]) * g[:, h])
    o_ref[rows, :] = jnp.concatenate(outs, axis=1).astype(BF16)
    for hd in range(B_HEADS):
        st_ref[hd] = states[hd]


def _hgrn_prompt_body(hq_ref, hk_ref, hc_ref, hv_ref, hg_ref, gain_ref, o_ref, s_ref, st_ref):
    @pl.when(pl.program_id(1) == 0)
    def _():
        st_ref[...] = jnp.zeros(st_ref.shape, F32)

    pick = _hgrn_pick(HGRN_ROWS)

    def step(n, carry):
        rows = pl.ds(pl.multiple_of(n * HGRN_ROWS, HGRN_ROWS), HGRN_ROWS)
        _hgrn_tile(rows, HGRN_ROWS, hq_ref, hk_ref, hc_ref, hv_ref, hg_ref, gain_ref, o_ref, st_ref, pick)
        return carry

    lax.fori_loop(0, HGRN_TILE // HGRN_ROWS, step, 0)

    @pl.when(pl.program_id(1) == pl.num_programs(1) - 1)
    def _():
        for hd in range(B_HEADS):
            s_ref[hd] = st_ref[hd].T


def _hgrn_prompt(hq, hk, hb, hv, hg, gain):
    tiles = LP // HGRN_TILE
    row = lambda b, i: (b * tiles + i, 0)
    spec = pl.BlockSpec((HGRN_TILE, B_QK_W), row)
    return pl.pallas_call(
        _hgrn_prompt_body,
        out_shape=(jax.ShapeDtypeStruct((N_ROWS, B_V_W), BF16),
                   jax.ShapeDtypeStruct((BATCH, B_HEADS, B_KEY_DIM, B_VAL_DIM), F32)),
        grid=(BATCH, tiles),
        in_specs=[spec, spec, spec, spec, spec,
                  pl.BlockSpec((B_HEADS, B_VAL_DIM), lambda b, i: (0, 0))],
        out_specs=(spec, pl.BlockSpec((None, B_HEADS, B_KEY_DIM, B_VAL_DIM),
                                      lambda b, i: (b, 0, 0, 0))),
        scratch_shapes=[pltpu.VMEM((B_HEADS, B_VAL_DIM, B_KEY_DIM), F32)],
        compiler_params=_params("parallel", "arbitrary"),
        name="hgrn_prompt",
    )(hq, hk, hb, hv, hg, gain)


def _hgrn_sample_body(hq_ref, hk_ref, hb_ref, hv_ref, hg_ref, gain_ref, s0_ref, o_in_ref,
                      o_ref, s_ref, st_ref):
    del o_in_ref
    for hd in range(B_HEADS):
        st_ref[hd] = s0_ref[hd].T
    _hgrn_tile(pl.ds(0, DEC_SEQ), DEC_SEQ, hq_ref, hk_ref, hb_ref, hv_ref, hg_ref, gain_ref, o_ref,
               st_ref, _hgrn_pick(DEC_SEQ))
    for hd in range(B_HEADS):
        s_ref[hd] = st_ref[hd].T


def _hgrn_sample(hq, hk, hb, hv, hg, gain, s0, o_prompt):
    row = lambda b: (S_BLOCK0 + b, 0)
    spec = pl.BlockSpec((DEC_SEQ, B_QK_W), row)
    state = pl.BlockSpec((None, B_HEADS, B_KEY_DIM, B_VAL_DIM), lambda b: (b, 0, 0, 0))
    return pl.pallas_call(
        _hgrn_sample_body,
        out_shape=(jax.ShapeDtypeStruct((N_ROWS, B_V_W), BF16),
                   jax.ShapeDtypeStruct((DEC_BATCH, B_HEADS, B_KEY_DIM, B_VAL_DIM), F32)),
        grid=(DEC_BATCH,),
        in_specs=[spec, spec, spec, spec, spec,
                  pl.BlockSpec((B_HEADS, B_VAL_DIM), lambda b: (0, 0)),
                  state, pl.BlockSpec(memory_space=pl.ANY)],
        out_specs=(spec, state),
        scratch_shapes=[pltpu.VMEM((B_HEADS, B_VAL_DIM, B_KEY_DIM), F32)],
        input_output_aliases={7: 0},
        compiler_params=_params("parallel"),
        name="hgrn_sample",
    )(hq, hk, hb, hv, hg, gain, s0, o_prompt)


def _even_out_body(x_ref, g_ref, oa_ref, ob_ref, w_ref, o_ref):
    m = _dot(oa_ref[...], w_ref[:A_Q_W, :]) + _dot(ob_ref[...], w_ref[A_Q_W:, :])
    y = x_ref[...] + _rms(m, g_ref[3:4, :])
    pad = _is_pad_row(pl.program_id(0) * ROW_TILE, ROW_TILE)
    o_ref[...] = jnp.where(pad, 0.0, y)


def _even_out(x, gains, oa, ob, w_out, j):
    row = lambda i: (i, 0)
    return pl.pallas_call(
        _even_out_body,
        out_shape=jax.ShapeDtypeStruct((N_ROWS, D_MODEL), F32),
        grid=(N_ROWS // ROW_TILE,),
        in_specs=[
            pl.BlockSpec((ROW_TILE, D_MODEL), row),
            pl.BlockSpec((6, D_MODEL), lambda i: (0, 0)),
            pl.BlockSpec((ROW_TILE, A_Q_W), row),
            pl.BlockSpec((ROW_TILE, B_V_W), row),
            pl.BlockSpec((None, A_Q_W + B_V_W, D_MODEL), lambda i: (j, 0, 0),
                         pipeline_mode=pl.Buffered(1)),
        ],
        out_specs=pl.BlockSpec((ROW_TILE, D_MODEL), row),
        input_output_aliases={0: 0},
        compiler_params=_params("parallel"),
        name=f"even_out_{j}",
    )(x, gains, oa, ob, w_out)


def _odd_in_body(x_ref, g_ref, w_ref, bg_ref, u_ref):
    h = _rms(x_ref[...], g_ref[2:3, :]).astype(BF16)
    bg_ref[...] = _dot(h, w_ref[:, :D_MODEL])
    u_ref[...] = _dot(h, w_ref[:, D_MODEL:2 * D_MODEL]) * _dot(h, w_ref[:, 2 * D_MODEL:])


def _odd_in(x, gains, w_in, j):
    row = lambda i: (i, 0)
    out = jax.ShapeDtypeStruct((N_ROWS, D_MODEL), F32)
    return pl.pallas_call(
        _odd_in_body,
        out_shape=(out, out),
        grid=(N_ROWS // ROW_TILE,),
        in_specs=[
            pl.BlockSpec((ROW_TILE, D_MODEL), row),
            pl.BlockSpec((6, D_MODEL), lambda i: (0, 0)),
            pl.BlockSpec((None, D_MODEL, 3 * D_MODEL), lambda i: (j, 0, 0),
                         pipeline_mode=pl.Buffered(1)),
        ],
        out_specs=(pl.BlockSpec((ROW_TILE, D_MODEL), row), pl.BlockSpec((ROW_TILE, D_MODEL), row)),
        compiler_params=_params("parallel"),
        name=f"odd_in_{j}",
    )(x, gains, w_in)


def _odd_out_body(x_ref, g_ref, bg_ref, u_ref, left_ref, cw_ref, w_ref, o_ref, *, left_rows, rows,
                  prompt):
    u = u_ref[...]
    before2 = left_ref[left_rows[0]:left_rows[0] + 1, :]
    before1 = left_ref[left_rows[1]:left_rows[1] + 1, :]
    pos = lax.broadcasted_iota(jnp.int32, (rows, 1), 0)
    shift1 = jnp.where(pos == 0, before1, pltpu.roll(u, 1, 0))
    shift2 = jnp.where(pos == 0, before2, jnp.where(pos == 1, before1, pltpu.roll(u, 2, 0)))
    y = shift2 * cw_ref[0:1, :] + shift1 * cw_ref[1:2, :] + u * cw_ref[2:3, :]
    m = _dot((bg_ref[...] * y).astype(BF16), w_ref[...])
    out = x_ref[...] + _rms(m, g_ref[3:4, :])
    if prompt:
        out = jnp.where(_is_pad_row(pl.program_id(0) * rows, rows), 0.0, out)
    o_ref[...] = out


def _odd_out_prompt(x, gains, bg, u, conv_w, w_out, j):
    row = lambda i: (i, 0)
    per = ODD_TILE // HALO
    halo = lambda i: (jnp.maximum(i * per - 1, 0), 0)
    return pl.pallas_call(
        functools.partial(_odd_out_body, left_rows=(HALO - 2, HALO - 1), rows=ODD_TILE, prompt=True),
        out_shape=jax.ShapeDtypeStruct((N_ROWS, D_MODEL), F32),
        grid=(P_ROWS // ODD_TILE,),
        in_specs=[
            pl.BlockSpec((ODD_TILE, D_MODEL), row),
            pl.BlockSpec((6, D_MODEL), lambda i: (0, 0)),
            pl.BlockSpec((ODD_TILE, D_MODEL), row),
            pl.BlockSpec((ODD_TILE, D_MODEL), row),
            pl.BlockSpec((HALO, D_MODEL), halo),
            pl.BlockSpec((None, CONV_WIDTH, D_MODEL), lambda i: (j, 0, 0)),
            pl.BlockSpec((None, D_MODEL, D_MODEL), lambda i: (j, 0, 0), pipeline_mode=pl.Buffered(1)),
        ],
        out_specs=pl.BlockSpec((ODD_TILE, D_MODEL), row),
        input_output_aliases={0: 0},
        compiler_params=_params("parallel"),
        name=f"odd_out_prompt_{j}",
    )(x, gains, bg, u, u, conv_w, w_out)


def _odd_out_sample(x, gains, bg, u, cache, conv_w, w_out, j):
    row = lambda b: (S_BLOCK0 + b, 0)
    return pl.pallas_call(
        functools.partial(_odd_out_body, left_rows=(0, 1), rows=DEC_SEQ, prompt=False),
        out_shape=jax.ShapeDtypeStruct((N_ROWS, D_MODEL), F32),
        grid=(DEC_BATCH,),
        in_specs=[
            pl.BlockSpec((DEC_SEQ, D_MODEL), row),
            pl.BlockSpec((6, D_MODEL), lambda b: (0, 0)),
            pl.BlockSpec((DEC_SEQ, D_MODEL), row),
            pl.BlockSpec((DEC_SEQ, D_MODEL), row),
            pl.BlockSpec((None, CONV_WIDTH - 1, D_MODEL), lambda b: (b, 0, 0)),
            pl.BlockSpec((None, CONV_WIDTH, D_MODEL), lambda b: (j, 0, 0)),
            pl.BlockSpec((None, D_MODEL, D_MODEL), lambda b: (j, 0, 0), pipeline_mode=pl.Buffered(1)),
        ],
        out_specs=pl.BlockSpec((DEC_SEQ, D_MODEL), row),
        input_output_aliases={0: 0},
        compiler_params=_params("parallel"),
        name=f"odd_out_sample_{j}",
    )(x, gains, bg, u, cache, conv_w, w_out)


def _stream_tails(a, row0, streams, length, n):
    ends = [row0 + (s + 1) * length for s in range(streams)]
    return jnp.stack([lax.slice_in_dim(a, e - n, e, axis=0) for e in ends])


def kernel(x_prompt, x_sample, cache_swa_k, cache_swa_v, state_hgrn, cache_conv, meta_tokens, norm_gains,
           w_ffn_gate, w_ffn_up, w_ffn_down, w_in_even, w_out_even, attn_sinks, hgrn_lb_logits,
           hgrn_norm_gain, w_in_odd, conv_w, w_out_odd):
    win = cache_swa_k.shape[2]
    lead = jnp.concatenate([jnp.zeros((PAD, D_MODEL), F32), meta_tokens.astype(F32)], axis=0)
    xp = jnp.concatenate([jnp.broadcast_to(lead[None], (BATCH, PAD + N_META, D_MODEL)), x_prompt], axis=1)
    x = jnp.concatenate([xp.reshape(P_ROWS, D_MODEL), x_sample.reshape(S_ROWS, D_MODEL)], axis=0)

    wg, wu, wd = (w.astype(BF16) for w in (w_ffn_gate, w_ffn_up, w_ffn_down))
    w_in_e, w_out_e = w_in_even.astype(BF16), w_out_even.astype(BF16)
    w_in_o, w_out_o = w_in_odd.astype(BF16), w_out_odd.astype(BF16)
    cache_k = cache_swa_k.reshape(N_EVEN, DEC_BATCH, win, A_KV_W)
    cache_v = cache_swa_v.reshape(N_EVEN, DEC_BATCH, win, A_KV_W)

    k_p, v_p, s_p, c_p, k_s, v_s, s_s, c_s = ([] for _ in range(8))
    for layer in range(DEPTH):
        gains = norm_gains[layer]
        j = layer // 2
        x = _ffn(x, gains, wg, wu, wd, layer, 0)
        if layer % 2 == 0:
            q, k, v, hq, hk, hb, hv, hg = _even_in(x, gains, w_in_e, hgrn_lb_logits, j)
            oa = _attn_prompt(attn_sinks[j], q, k, v)
            oa = _attn_sample(attn_sinks[j], q, k, v, cache_k[j], cache_v[j], oa)
            ob, st_p = _hgrn_prompt(hq, hk, hb, hv, hg, hgrn_norm_gain[j])
            ob, st_s = _hgrn_sample(hq, hk, hb, hv, hg, hgrn_norm_gain[j], state_hgrn[j], ob)
            x = _even_out(x, gains, oa, ob, w_out_e, j)
            for new, tail_p, tail_s, cache in ((k, k_p, k_s, cache_k[j]), (v, v_p, v_s, cache_v[j])):
                tail_p.append(_stream_tails(new, 0, BATCH, LP, WINDOW).reshape(
                    BATCH, WINDOW, A_KV_HEADS, A_HEAD_DIM))
                both = jnp.concatenate([cache, new[P_ROWS:].reshape(DEC_BATCH, DEC_SEQ, A_KV_W)], axis=1)
                tail_s.append(both[:, both.shape[1] - win:].reshape(DEC_BATCH, win, A_KV_HEADS, A_HEAD_DIM))
            s_p.append(st_p)
            s_s.append(st_s)
        else:
            bg, u = _odd_in(x, gains, w_in_o, j)
            x = _odd_out_prompt(x, gains, bg, u, conv_w, w_out_o, j)
            x = _odd_out_sample(x, gains, bg, u, cache_conv[j], conv_w, w_out_o, j)
            c_p.append(_stream_tails(u, 0, BATCH, LP, CONV_WIDTH - 1))
            c_s.append(_stream_tails(u, P_ROWS, DEC_BATCH, DEC_SEQ, CONV_WIDTH - 1))
        x = _ffn(x, gains, wg, wu, wd, layer, 1)

    y_prompt = x[:P_ROWS].reshape(BATCH, LP, D_MODEL)[:, PAD + N_META:]
    y_sample = x[P_ROWS:].reshape(DEC_BATCH, DEC_SEQ, D_MODEL)
    return (y_prompt, y_sample, jnp.stack(k_p), jnp.stack(v_p), jnp.stack(s_p), jnp.stack(c_p),
            jnp.stack(k_s), jnp.stack(v_s), jnp.stack(s_s), jnp.stack(c_s))
```

```python
import functools

import jax
import jax.numpy as jnp
from jax import lax
from jax.experimental import pallas as pl
from jax.experimental.pallas import tpu as pltpu

F32 = jnp.float32
BF16 = jnp.bfloat16

D_MODEL = 1024
BATCH = 4
SEQ = 4096
DEPTH = 4
DEC_BATCH = 8
DEC_SEQ = 32
CHUNK = 64
N_META = 16
WINDOW = 128
WINDOW_CHUNKS = WINDOW // CHUNK
A_HEADS = 8
A_KV_HEADS = 2
A_HEAD_DIM = 64
A_Q_W = A_HEADS * A_HEAD_DIM
A_KV_W = A_KV_HEADS * A_HEAD_DIM
B_HEADS = 4
B_KEY_DIM = 128
B_VAL_DIM = 128
B_QK_W = B_HEADS * B_KEY_DIM
B_V_W = B_HEADS * B_VAL_DIM
B_BLOCK = 16
CONV_WIDTH = 3
D_FF = 2816
FFN_RESIDUAL = 0.5
N_EVEN = (DEPTH + 1) // 2
N_ODD = DEPTH // 2
EVEN_IN_W = A_Q_W + 2 * A_KV_W + 2 * B_QK_W + 2 * B_V_W
EPS = 1e-6
MASK_VALUE = -1e30
LB_FLOOR = 1e-30

PAD = (-N_META) % CHUNK
LP = PAD + N_META + SEQ
N_CHUNKS = LP // CHUNK
P_ROWS = BATCH * LP
S_ROWS = DEC_BATCH * DEC_SEQ
N_ROWS = P_ROWS + S_ROWS
S_BLOCK0 = P_ROWS // DEC_SEQ

ROW_TILE = 512
FF_CHUNK = 256
HGRN_TILE = 832
HGRN_ROWS = 64
HGRN_SUB = 8
HGRN_GROUP = 2
LOG2_E = 1.4426950408889634
CUMSUM_ROWS = 256
FFN_TILE = 768
FFN_SUB = 768
ATTN_CHUNKS = 5
ODD_TILE = 640
HALO = 8
VMEM_LIMIT = 56 * 1024 * 1024

assert N_ROWS % ROW_TILE == 0 and D_FF % FF_CHUNK == 0
assert LP % HGRN_TILE == 0 and HGRN_TILE % HGRN_ROWS == 0 and ROW_TILE % HGRN_ROWS == 0
assert HGRN_ROWS % DEC_SEQ == 0 and DEC_SEQ % HGRN_SUB == 0
assert ROW_TILE % CUMSUM_ROWS == 0 and CUMSUM_ROWS % HGRN_ROWS == 0
assert N_ROWS % FFN_TILE == 0 and FFN_TILE % FFN_SUB == 0 and N_CHUNKS % ATTN_CHUNKS == 0
assert P_ROWS % ODD_TILE == 0 and ODD_TILE % HALO == 0 and P_ROWS % DEC_SEQ == 0


def _params(*sem):
    return pltpu.CompilerParams(dimension_semantics=sem, vmem_limit_bytes=VMEM_LIMIT)


def _rms(x, gain):
    return x * lax.rsqrt(jnp.mean(x * x, axis=-1, keepdims=True) + EPS) * gain


def _silu(x):
    return x * jax.nn.sigmoid(x)


def _dot(a, b):
    return jnp.dot(a, b, preferred_element_type=F32)


def _dot_nt(a, b):
    return lax.dot_general(a, b, (((1,), (1,)), ((), ())), preferred_element_type=F32)


def _dot_tn(a, b):
    return lax.dot_general(a, b, (((0,), (0,)), ((), ())), preferred_element_type=F32)


def _is_pad_row(row0, rows):
    r = row0 + lax.broadcasted_iota(jnp.int32, (rows, 1), 0)
    pad = None
    for b in range(BATCH):
        hit = jnp.logical_and(r >= b * LP, r < b * LP + PAD)
        pad = hit if pad is None else jnp.logical_or(pad, hit)
    return pad


def _ffn_body(x_ref, g_ref, wg_ref, wu_ref, wd_ref, o_ref, *, pre, post):
    for lo in range(0, FFN_TILE, FFN_SUB):
        x = x_ref[lo:lo + FFN_SUB, :]
        h = _rms(x, g_ref[pre:pre + 1, :]).astype(BF16)
        acc = jnp.zeros(x.shape, F32)
        chunks = [slice(c * FF_CHUNK, (c + 1) * FF_CHUNK) for c in range(D_FF // FF_CHUNK)]
        gate_up = (_dot(h, wg_ref[:, chunks[0]]), _dot(h, wu_ref[:, chunks[0]]))
        for c, cols in enumerate(chunks):
            gate, up = gate_up
            if c + 1 < len(chunks):
                gate_up = (_dot(h, wg_ref[:, chunks[c + 1]]), _dot(h, wu_ref[:, chunks[c + 1]]))
            act = (_silu(gate) * up).astype(BF16)
            acc = acc + _dot(act, wd_ref[cols, :])
        o_ref[lo:lo + FFN_SUB, :] = x + FFN_RESIDUAL * _rms(acc, g_ref[post:post + 1, :])


def _ffn(x, gains, wg, wu, wd, layer, which):
    pre, post = (0, 1) if which == 0 else (4, 5)
    row = lambda i: (i, 0)
    whole = lambda i: (0, 0)
    wsel = lambda i: (layer, which, 0, 0)
    resident = pl.Buffered(1)
    return pl.pallas_call(
        functools.partial(_ffn_body, pre=pre, post=post),
        out_shape=jax.ShapeDtypeStruct((N_ROWS, D_MODEL), F32),
        grid=(N_ROWS // FFN_TILE,),
        in_specs=[
            pl.BlockSpec((FFN_TILE, D_MODEL), row),
            pl.BlockSpec((6, D_MODEL), whole),
            pl.BlockSpec((None, None, D_MODEL, D_FF), wsel, pipeline_mode=resident),
            pl.BlockSpec((None, None, D_MODEL, D_FF), wsel, pipeline_mode=resident),
            pl.BlockSpec((None, None, D_FF, D_MODEL), wsel, pipeline_mode=resident),
        ],
        out_specs=pl.BlockSpec((FFN_TILE, D_MODEL), row),
        input_output_aliases={0: 0},
        compiler_params=_params("parallel"),
        name=f"ffn_l{layer}_{which}",
    )(x, gains, wg, wu, wd)


def _lower_bound(lbl_ref, j):
    rows = [lbl_ref[i:i + 1, :] for i in range(N_EVEN)]
    top = functools.reduce(jnp.maximum, rows)
    e = [jnp.exp(r - top) for r in rows]
    total = functools.reduce(lambda a, b: a + b, e)
    p = [v / total for v in e]
    cum = functools.reduce(lambda a, b: a + b, p[:j + 1])
    return jnp.maximum(cum - p[0], 0.0)


def _span_cumsum(g, row0):
    out = []
    for lo in range(0, g.shape[0], CUMSUM_ROWS):
        part = g[lo:lo + CUMSUM_ROWS, :]
        t = row0 + lo + lax.broadcasted_iota(jnp.int32, (CUMSUM_ROWS, CUMSUM_ROWS), 0)
        u = row0 + lo + lax.broadcasted_iota(jnp.int32, (CUMSUM_ROWS, CUMSUM_ROWS), 1)
        span = jnp.where(t >= P_ROWS, DEC_SEQ, HGRN_ROWS)
        tri = jnp.where(jnp.logical_and((t ^ u) < span, u <= t), 1.0, 0.0).astype(BF16)
        total = None
        for _ in range(3):
            term = part.astype(BF16)
            part = part - term.astype(F32)
            total = _dot(tri, term) if total is None else total + _dot(tri, term)
        out.append(total)
    return jnp.concatenate(out, axis=0)


def _even_in_body(x_ref, g_ref, w_ref, lbl_ref, q_ref, k_ref, v_ref,
                  hq_ref, hk_ref, hb_ref, hv_ref, hg_ref, *, j):
    h = _rms(x_ref[...], g_ref[2:3, :]).astype(BF16)
    widths = (A_Q_W, 2 * A_KV_W, B_QK_W, B_QK_W, B_V_W, B_V_W)
    starts = [sum(widths[:i]) for i in range(len(widths))]
    proj = lambda i: _dot(h, w_ref[:, starts[i]:starts[i] + widths[i]])
    fx = proj(3)
    q_ref[...] = (proj(0) * (A_HEAD_DIM ** -0.5)).astype(BF16)
    kv = proj(1)
    k_ref[...] = kv[:, :A_KV_W]
    v_ref[...] = kv[:, A_KV_W:]
    hq_ref[...] = _silu(proj(2))
    hv_ref[...] = proj(4).astype(BF16)
    hg_ref[...] = _silu(proj(5))
    lb = _lower_bound(lbl_ref, j)
    soft = jnp.log1p(jnp.exp(-jnp.abs(fx)))
    ls_pos = jnp.minimum(fx, 0.0) - soft
    ls_neg = jnp.minimum(-fx, 0.0) - soft
    other = jnp.log(jnp.maximum(lb, LB_FLOOR)) + ls_neg
    log_f = jnp.maximum(ls_pos, other) + jnp.log1p(jnp.exp(-jnp.abs(ls_pos - other)))
    hk_ref[...] = (1.0 - lb) * jax.nn.sigmoid(-fx)
    hb_ref[...] = _span_cumsum(log_f, pl.program_id(0) * ROW_TILE)


def _even_in(x, gains, w_in, lb_logits, j):
    row = lambda i: (i, 0)
    whole = lambda i: (0, 0)
    wide = lambda w, dt: jax.ShapeDtypeStruct((N_ROWS, w), dt)
    spec = lambda w: pl.BlockSpec((ROW_TILE, w), row)
    return pl.pallas_call(
        functools.partial(_even_in_body, j=j),
        out_shape=(wide(A_Q_W, BF16), wide(A_KV_W, F32), wide(A_KV_W, F32),
                   wide(B_QK_W, F32), wide(B_QK_W, F32), wide(B_QK_W, F32),
                   wide(B_V_W, BF16), wide(B_V_W, F32)),
        grid=(N_ROWS // ROW_TILE,),
        in_specs=[
            pl.BlockSpec((ROW_TILE, D_MODEL), row),
            pl.BlockSpec((6, D_MODEL), whole),
            pl.BlockSpec((None, D_MODEL, EVEN_IN_W), lambda i: (j, 0, 0),
                         pipeline_mode=pl.Buffered(1)),
            pl.BlockSpec((N_EVEN, B_QK_W), whole),
        ],
        out_specs=(spec(A_Q_W), spec(A_KV_W), spec(A_KV_W), spec(B_QK_W), spec(B_QK_W),
                   spec(B_QK_W), spec(B_V_W), spec(B_V_W)),
        compiler_params=_params("parallel"),
        name=f"even_in_{j}",
    )(x, gains, w_in, lb_logits)


def _half_lane_variants(a):
    lane = lax.broadcasted_iota(jnp.int32, a.shape, 1)
    low = lane < A_HEAD_DIM
    swapped = pltpu.roll(a, A_HEAD_DIM, 1)
    zero = jnp.zeros_like(a)
    return (jnp.where(low, a, zero).astype(BF16), jnp.where(low, zero, swapped).astype(BF16),
            jnp.where(low, swapped, zero).astype(BF16), jnp.where(low, zero, a).astype(BF16))


def _sink_softmax(s, valid, sink):
    if valid is not None:
        s = jnp.where(valid, s, MASK_VALUE)
    m = jnp.maximum(jnp.max(s, axis=-1, keepdims=True), sink)
    p = jnp.exp(s - m)
    denom = jnp.sum(p, axis=-1, keepdims=True) + jnp.exp(sink - m)
    return (p / denom).astype(BF16)


def _group_sinks(sink_ref, kvh, rows):
    first = lax.broadcasted_iota(jnp.int32, (2 * rows, 1), 0) < rows
    base = kvh * (A_HEADS // A_KV_HEADS)
    return (jnp.where(first, sink_ref[base], sink_ref[base + 2]),
            jnp.where(first, sink_ref[base + 1], sink_ref[base + 3]))


def _attend(items, sinks):
    scores = []
    for q, keys, _, _ in items:
        for kvh in range(A_KV_HEADS):
            c0 = kvh * 256
            q2 = jnp.concatenate([q[:, c0:c0 + 128], q[:, c0 + 128:c0 + 256]], axis=0)
            scores += [_dot_nt(q2, keys[2 * kvh]), _dot_nt(q2, keys[2 * kvh + 1])]
    weights = []
    for n, (_, _, _, valid) in enumerate(items):
        for kvh in range(A_KV_HEADS):
            for half in range(2):
                weights.append(_sink_softmax(scores[4 * n + 2 * kvh + half], valid, sinks[kvh][half]))
    outs = []
    for n, (q, _, vals, _) in enumerate(items):
        rows = q.shape[0]
        groups = []
        for kvh in range(A_KV_HEADS):
            w = weights[4 * n + 2 * kvh:4 * n + 2 * kvh + 2]
            o = _dot(w[0], vals[2 * kvh]) + _dot(w[1], vals[2 * kvh + 1])
            groups.append(jnp.concatenate([o[:rows], o[rows:]], axis=1))
        outs.append(jnp.concatenate(groups, axis=1))
    return outs


def _attn_prompt_body(sink_ref, q_ref, k_ref, v_ref, o_ref, kk_ref, vv_ref):
    span = (WINDOW_CHUNKS + 1) * CHUNK

    def fill(c, carry):
        rows = pl.ds(pl.multiple_of(c * CHUNK, CHUNK), CHUNK)
        kvar = _half_lane_variants(k_ref[rows, :])
        vvar = _half_lane_variants(v_ref[rows, :])
        for i in range(4):
            kk_ref[i, rows, :] = kvar[i]
            vv_ref[i, rows, :] = vvar[i]
        return carry

    lax.fori_loop(0, N_CHUNKS, fill, 0)
    sinks = [_group_sinks(sink_ref, kvh, CHUNK) for kvh in range(A_KV_HEADS)]

    def step(n, carry):
        starts, items = [], []
        for i in range(ATTN_CHUNKS):
            c = n * ATTN_CHUNKS + i
            r0 = pl.multiple_of(c * CHUNK, CHUNK)
            s0 = pl.multiple_of(jnp.maximum(c - WINDOW_CHUNKS, 0) * CHUNK, CHUNK)
            kpos = s0 + lax.broadcasted_iota(jnp.int32, (1, span), 1)
            valid = jnp.logical_and(kpos >= PAD, kpos < r0 + CHUNK)
            starts.append(r0)
            items.append((q_ref[pl.ds(r0, CHUNK), :],
                          [kk_ref[v, pl.ds(s0, span), :] for v in range(4)],
                          [vv_ref[v, pl.ds(s0, span), :] for v in range(4)], valid))
        for r0, out in zip(starts, _attend(items, sinks)):
            o_ref[pl.ds(r0, CHUNK), :] = out.astype(BF16)
        return carry

    lax.fori_loop(0, N_CHUNKS // ATTN_CHUNKS, step, 0)


def _attn_prompt(sink, q, k, v):
    seq = lambda b: (b, 0)
    return pl.pallas_call(
        _attn_prompt_body,
        out_shape=jax.ShapeDtypeStruct((N_ROWS, A_Q_W), BF16),
        grid=(BATCH,),
        in_specs=[
            pl.BlockSpec(memory_space=pltpu.SMEM),
            pl.BlockSpec((LP, A_Q_W), seq),
            pl.BlockSpec((LP, A_KV_W), seq),
            pl.BlockSpec((LP, A_KV_W), seq),
        ],
        out_specs=pl.BlockSpec((LP, A_Q_W), seq),
        scratch_shapes=[pltpu.VMEM((4, LP, A_KV_W), BF16), pltpu.VMEM((4, LP, A_KV_W), BF16)],
        compiler_params=_params("parallel"),
        name="attn_prompt",
    )(sink, q, k, v)


def _attn_sample_body(sink_ref, q_ref, k_ref, v_ref, ck_ref, cv_ref, o_in_ref, o_ref):
    del o_in_ref
    keys = _half_lane_variants(jnp.concatenate([ck_ref[...], k_ref[...]], axis=0))
    vals = _half_lane_variants(jnp.concatenate([cv_ref[...], v_ref[...]], axis=0))
    sinks = [_group_sinks(sink_ref, kvh, DEC_SEQ) for kvh in range(A_KV_HEADS)]
    o_ref[...] = _attend([(q_ref[...], keys, vals, None)], sinks)[0].astype(BF16)


def _attn_sample(sink, q, k, v, cache_k, cache_v, o_prompt):
    win = cache_k.shape[1]
    new = lambda b: (S_BLOCK0 + b, 0)
    old = lambda b: (b, 0, 0)
    return pl.pallas_call(
        _attn_sample_body,
        out_shape=jax.ShapeDtypeStruct((N_ROWS, A_Q_W), BF16),
        grid=(DEC_BATCH,),
        in_specs=[
            pl.BlockSpec(memory_space=pltpu.SMEM),
            pl.BlockSpec((DEC_SEQ, A_Q_W), new),
            pl.BlockSpec((DEC_SEQ, A_KV_W), new),
            pl.BlockSpec((DEC_SEQ, A_KV_W), new),
            pl.BlockSpec((None, win, A_KV_W), old),
            pl.BlockSpec((None, win, A_KV_W), old),
            pl.BlockSpec(memory_space=pl.ANY),
        ],
        out_specs=pl.BlockSpec((DEC_SEQ, A_Q_W), new),
        input_output_aliases={6: 0},
        compiler_params=_params("parallel"),
        name="attn_sample",
    )(sink, q, k, v, cache_k, cache_v, o_prompt)


def _group_row(a, group, row):
    spans = a.shape[0] // group
    picked = a.reshape(spans, group, a.shape[1])[:, row:row + 1, :]
    return jnp.broadcast_to(picked, (spans, group, a.shape[1])).reshape(a.shape)


def _hgrn_pairs(q, k, c2):
    parts = []
    for s in range(HGRN_SUB):
        decay = jnp.exp2(jnp.minimum(c2 - _group_row(c2, HGRN_SUB, s), 0.0))
        parts.append((q * _group_row(k, HGRN_SUB, s) * decay).astype(BF16))
    return jnp.concatenate(parts, axis=1)


def _hgrn_pick(rows):
    shape = (HGRN_SUB * B_KEY_DIM, rows)
    block = lax.broadcasted_iota(jnp.int32, shape, 0) // B_KEY_DIM
    col = lax.broadcasted_iota(jnp.int32, shape, 1) & (HGRN_SUB - 1)
    return jnp.where(block == col, 1.0, 0.0).astype(BF16)


def _hgrn_head(q, c, v, state_t, intra, levels, update):
    rows = q.shape[0]
    t_idx = lax.broadcasted_iota(jnp.int32, (rows, rows), 0)
    u_idx = lax.broadcasted_iota(jnp.int32, (rows, rows), 1)
    apart = t_idx ^ u_idx
    in_group = jnp.logical_and(apart < HGRN_SUB, u_idx <= t_idx)
    scores = jnp.where(in_group, intra, 0.0)
    for w, level in levels:
        scores = scores + (level if 2 * w == rows else jnp.where(apart < 2 * w, level, 0.0))
    o = _dot(scores.astype(BF16), v) + _dot_nt((q * jnp.exp2(c)).astype(BF16), state_t.astype(BF16))
    return o, state_t * jnp.exp2(c[rows - 1:rows, :]) + update


def _hgrn_levels(q, k, c):
    rows = q.shape[0]
    pos = lax.broadcasted_iota(jnp.int32, q.shape, 0)
    levels = []
    w = HGRN_SUB
    while w < rows:
        rho = _group_row(c, 2 * w, w - 1)
        upper = (pos & w) != 0
        q_up = jnp.where(upper, q * jnp.exp2(jnp.minimum(c - rho, 0.0)), 0.0).astype(BF16)
        k_low = jnp.where(upper, 0.0, k * jnp.exp2(jnp.minimum(rho - c, 0.0))).astype(BF16)
        levels.append((w, _dot_nt(q_up, k_low)))
        w *= 2
    return levels


def _hgrn_tiles(tiles, n, hq_ref, hk_ref, hc_ref, hv_ref, hg_ref, gain_ref, o_ref, st_ref, pick):
    loaded = [(hq_ref[r, :], hk_ref[r, :], hc_ref[r, :] * LOG2_E, hv_ref[r, :], hg_ref[r, :])
              for r in tiles]
    states = [st_ref[hd] for hd in range(B_HEADS)]
    heads = [slice(hd * B_KEY_DIM, (hd + 1) * B_KEY_DIM) for hd in range(B_HEADS)]
    intras = [_dot(jnp.concatenate([_hgrn_pairs(q[:, h], k[:, h], c[:, h]) for h in heads], axis=0), pick)
              for q, k, c, _, _ in loaded]
    levels = [[_hgrn_levels(q[:, h], k[:, h], c[:, h]) for h in heads] for q, k, c, _, _ in loaded]
    updates = [[_dot_tn(v[:, h], (k[:, h] * jnp.exp2(c[n - 1:n, h] - c[:, h])).astype(BF16))
                for h in heads] for _, k, c, v, _ in loaded]
    results = []
    for t, (q, k, c, v, g) in enumerate(loaded):
        outs = []
        for hd, h in enumerate(heads):
            o, states[hd] = _hgrn_head(q[:, h], c[:, h], v[:, h], states[hd],
                                       intras[t][hd * n:(hd + 1) * n, :], levels[t][hd], updates[t][hd])
            outs.append(_rms(o, gain_ref[hd:hd + 1, :]) * g[:, h])
        results.append(jnp.concatenate(outs, axis=1).astype(BF16))
    for r, res in zip(tiles, results):
        o_ref[r, :] = res
    for hd in range(B_HEADS):
        st_ref[hd] = states[hd]


def _hgrn_prompt_body(hq_ref, hk_ref, hc_ref, hv_ref, hg_ref, gain_ref, o_ref, s_ref, st_ref):
    @pl.when(pl.program_id(1) == 0)
    def _():
        st_ref[...] = jnp.zeros(st_ref.shape, F32)

    pick = _hgrn_pick(HGRN_ROWS)
    refs = (hq_ref, hk_ref, hc_ref, hv_ref, hg_ref, gain_ref, o_ref, st_ref, pick)
    n_tiles = HGRN_TILE // HGRN_ROWS

    def step(n, carry):
        first = n * HGRN_GROUP
        tiles = [pl.ds(pl.multiple_of((first + i) * HGRN_ROWS, HGRN_ROWS), HGRN_ROWS)
                 for i in range(HGRN_GROUP)]
        _hgrn_tiles(tiles, HGRN_ROWS, *refs)
        return carry

    lax.fori_loop(0, n_tiles // HGRN_GROUP, step, 0)
    rest = [pl.ds(i * HGRN_ROWS, HGRN_ROWS) for i in range(n_tiles - n_tiles % HGRN_GROUP, n_tiles)]
    if rest:
        _hgrn_tiles(rest, HGRN_ROWS, *refs)

    @pl.when(pl.program_id(1) == pl.num_programs(1) - 1)
    def _():
        for hd in range(B_HEADS):
            s_ref[hd] = st_ref[hd].T


def _hgrn_prompt(hq, hk, hb, hv, hg, gain):
    tiles = LP // HGRN_TILE
    row = lambda b, i: (b * tiles + i, 0)
    spec = pl.BlockSpec((HGRN_TILE, B_QK_W), row)
    return pl.pallas_call(
        _hgrn_prompt_body,
        out_shape=(jax.ShapeDtypeStruct((N_ROWS, B_V_W), BF16),
                   jax.ShapeDtypeStruct((BATCH, B_HEADS, B_KEY_DIM, B_VAL_DIM), F32)),
        grid=(BATCH, tiles),
        in_specs=[spec, spec, spec, spec, spec,
                  pl.BlockSpec((B_HEADS, B_VAL_DIM), lambda b, i: (0, 0))],
        out_specs=(spec, pl.BlockSpec((None, B_HEADS, B_KEY_DIM, B_VAL_DIM),
                                      lambda b, i: (b, 0, 0, 0))),
        scratch_shapes=[pltpu.VMEM((B_HEADS, B_VAL_DIM, B_KEY_DIM), F32)],
        compiler_params=_params("parallel", "arbitrary"),
        name="hgrn_prompt",
    )(hq, hk, hb, hv, hg, gain)


def _hgrn_sample_body(hq_ref, hk_ref, hb_ref, hv_ref, hg_ref, gain_ref, s0_ref, o_in_ref,
                      o_ref, s_ref, st_ref):
    del o_in_ref
    for hd in range(B_HEADS):
        st_ref[hd] = s0_ref[hd].T
    _hgrn_tiles([pl.ds(0, DEC_SEQ)], DEC_SEQ, hq_ref, hk_ref, hb_ref, hv_ref, hg_ref, gain_ref, o_ref,
                st_ref, _hgrn_pick(DEC_SEQ))
    for hd in range(B_HEADS):
        s_ref[hd] = st_ref[hd].T


def _hgrn_sample(hq, hk, hb, hv, hg, gain, s0, o_prompt):
    row = lambda b: (S_BLOCK0 + b, 0)
    spec = pl.BlockSpec((DEC_SEQ, B_QK_W), row)
    state = pl.BlockSpec((None, B_HEADS, B_KEY_DIM, B_VAL_DIM), lambda b: (b, 0, 0, 0))
    return pl.pallas_call(
        _hgrn_sample_body,
        out_shape=(jax.ShapeDtypeStruct((N_ROWS, B_V_W), BF16),
                   jax.ShapeDtypeStruct((DEC_BATCH, B_HEADS, B_KEY_DIM, B_VAL_DIM), F32)),
        grid=(DEC_BATCH,),
        in_specs=[spec, spec, spec, spec, spec,
                  pl.BlockSpec((B_HEADS, B_VAL_DIM), lambda b: (0, 0)),
                  state, pl.BlockSpec(memory_space=pl.ANY)],
        out_specs=(spec, state),
        scratch_shapes=[pltpu.VMEM((B_HEADS, B_VAL_DIM, B_KEY_DIM), F32)],
        input_output_aliases={7: 0},
        compiler_params=_params("parallel"),
        name="hgrn_sample",
    )(hq, hk, hb, hv, hg, gain, s0, o_prompt)


def _even_out_body(x_ref, g_ref, oa_ref, ob_ref, w_ref, o_ref):
    m = _dot(oa_ref[...], w_ref[:A_Q_W, :]) + _dot(ob_ref[...], w_ref[A_Q_W:, :])
    y = x_ref[...] + _rms(m, g_ref[3:4, :])
    pad = _is_pad_row(pl.program_id(0) * ROW_TILE, ROW_TILE)
    o_ref[...] = jnp.where(pad, 0.0, y)


def _even_out(x, gains, oa, ob, w_out, j):
    row = lambda i: (i, 0)
    return pl.pallas_call(
        _even_out_body,
        out_shape=jax.ShapeDtypeStruct((N_ROWS, D_MODEL), F32),
        grid=(N_ROWS // ROW_TILE,),
        in_specs=[
            pl.BlockSpec((ROW_TILE, D_MODEL), row),
            pl.BlockSpec((6, D_MODEL), lambda i: (0, 0)),
            pl.BlockSpec((ROW_TILE, A_Q_W), row),
            pl.BlockSpec((ROW_TILE, B_V_W), row),
            pl.BlockSpec((None, A_Q_W + B_V_W, D_MODEL), lambda i: (j, 0, 0),
                         pipeline_mode=pl.Buffered(1)),
        ],
        out_specs=pl.BlockSpec((ROW_TILE, D_MODEL), row),
        input_output_aliases={0: 0},
        compiler_params=_params("parallel"),
        name=f"even_out_{j}",
    )(x, gains, oa, ob, w_out)


def _odd_in_body(x_ref, g_ref, w_ref, bg_ref, u_ref):
    h = _rms(x_ref[...], g_ref[2:3, :]).astype(BF16)
    bg_ref[...] = _dot(h, w_ref[:, :D_MODEL])
    u_ref[...] = _dot(h, w_ref[:, D_MODEL:2 * D_MODEL]) * _dot(h, w_ref[:, 2 * D_MODEL:])


def _odd_in(x, gains, w_in, j):
    row = lambda i: (i, 0)
    out = jax.ShapeDtypeStruct((N_ROWS, D_MODEL), F32)
    return pl.pallas_call(
        _odd_in_body,
        out_shape=(out, out),
        grid=(N_ROWS // ROW_TILE,),
        in_specs=[
            pl.BlockSpec((ROW_TILE, D_MODEL), row),
            pl.BlockSpec((6, D_MODEL), lambda i: (0, 0)),
            pl.BlockSpec((None, D_MODEL, 3 * D_MODEL), lambda i: (j, 0, 0),
                         pipeline_mode=pl.Buffered(1)),
        ],
        out_specs=(pl.BlockSpec((ROW_TILE, D_MODEL), row), pl.BlockSpec((ROW_TILE, D_MODEL), row)),
        compiler_params=_params("parallel"),
        name=f"odd_in_{j}",
    )(x, gains, w_in)


def _odd_out_body(x_ref, g_ref, bg_ref, u_ref, left_ref, cw_ref, w_ref, o_ref, *, left_rows, rows,
                  prompt):
    u = u_ref[...]
    before2 = left_ref[left_rows[0]:left_rows[0] + 1, :]
    before1 = left_ref[left_rows[1]:left_rows[1] + 1, :]
    pos = lax.broadcasted_iota(jnp.int32, (rows, 1), 0)
    shift1 = jnp.where(pos == 0, before1, pltpu.roll(u, 1, 0))
    shift2 = jnp.where(pos == 0, before2, jnp.where(pos == 1, before1, pltpu.roll(u, 2, 0)))
    y = shift2 * cw_ref[0:1, :] + shift1 * cw_ref[1:2, :] + u * cw_ref[2:3, :]
    m = _dot((bg_ref[...] * y).astype(BF16), w_ref[...])
    out = x_ref[...] + _rms(m, g_ref[3:4, :])
    if prompt:
        out = jnp.where(_is_pad_row(pl.program_id(0) * rows, rows), 0.0, out)
    o_ref[...] = out


def _odd_out_prompt(x, gains, bg, u, conv_w, w_out, j):
    row = lambda i: (i, 0)
    per = ODD_TILE // HALO
    halo = lambda i: (jnp.maximum(i * per - 1, 0), 0)
    return pl.pallas_call(
        functools.partial(_odd_out_body, left_rows=(HALO - 2, HALO - 1), rows=ODD_TILE, prompt=True),
        out_shape=jax.ShapeDtypeStruct((N_ROWS, D_MODEL), F32),
        grid=(P_ROWS // ODD_TILE,),
        in_specs=[
            pl.BlockSpec((ODD_TILE, D_MODEL), row),
            pl.BlockSpec((6, D_MODEL), lambda i: (0, 0)),
            pl.BlockSpec((ODD_TILE, D_MODEL), row),
            pl.BlockSpec((ODD_TILE, D_MODEL), row),
            pl.BlockSpec((HALO, D_MODEL), halo),
            pl.BlockSpec((None, CONV_WIDTH, D_MODEL), lambda i: (j, 0, 0)),
            pl.BlockSpec((None, D_MODEL, D_MODEL), lambda i: (j, 0, 0), pipeline_mode=pl.Buffered(1)),
        ],
        out_specs=pl.BlockSpec((ODD_TILE, D_MODEL), row),
        input_output_aliases={0: 0},
        compiler_params=_params("parallel"),
        name=f"odd_out_prompt_{j}",
    )(x, gains, bg, u, u, conv_w, w_out)


def _odd_out_sample(x, gains, bg, u, cache, conv_w, w_out, j):
    row = lambda b: (S_BLOCK0 + b, 0)
    return pl.pallas_call(
        functools.partial(_odd_out_body, left_rows=(0, 1), rows=DEC_SEQ, prompt=False),
        out_shape=jax.ShapeDtypeStruct((N_ROWS, D_MODEL), F32),
        grid=(DEC_BATCH,),
        in_specs=[
            pl.BlockSpec((DEC_SEQ, D_MODEL), row),
            pl.BlockSpec((6, D_MODEL), lambda b: (0, 0)),
            pl.BlockSpec((DEC_SEQ, D_MODEL), row),
            pl.BlockSpec((DEC_SEQ, D_MODEL), row),
            pl.BlockSpec((None, CONV_WIDTH - 1, D_MODEL), lambda b: (b, 0, 0)),
            pl.BlockSpec((None, CONV_WIDTH, D_MODEL), lambda b: (j, 0, 0)),
            pl.BlockSpec((None, D_MODEL, D_MODEL), lambda b: (j, 0, 0), pipeline_mode=pl.Buffered(1)),
        ],
        out_specs=pl.BlockSpec((DEC_SEQ, D_MODEL), row),
        input_output_aliases={0: 0},
        compiler_params=_params("parallel"),
        name=f"odd_out_sample_{j}",
    )(x, gains, bg, u, cache, conv_w, w_out)


def _stream_tails(a, row0, streams, length, n):
    ends = [row0 + (s + 1) * length for s in range(streams)]
    return jnp.stack([lax.slice_in_dim(a, e - n, e, axis=0) for e in ends])


def kernel(x_prompt, x_sample, cache_swa_k, cache_swa_v, state_hgrn, cache_conv, meta_tokens, norm_gains,
           w_ffn_gate, w_ffn_up, w_ffn_down, w_in_even, w_out_even, attn_sinks, hgrn_lb_logits,
           hgrn_norm_gain, w_in_odd, conv_w, w_out_odd):
    win = cache_swa_k.shape[2]
    lead = jnp.concatenate([jnp.zeros((PAD, D_MODEL), F32), meta_tokens.astype(F32)], axis=0)
    xp = jnp.concatenate([jnp.broadcast_to(lead[None], (BATCH, PAD + N_META, D_MODEL)), x_prompt], axis=1)
    x = jnp.concatenate([xp.reshape(P_ROWS, D_MODEL), x_sample.reshape(S_ROWS, D_MODEL)], axis=0)

    wg, wu, wd = (w.astype(BF16) for w in (w_ffn_gate, w_ffn_up, w_ffn_down))
    w_in_e, w_out_e = w_in_even.astype(BF16), w_out_even.astype(BF16)
    w_in_o, w_out_o = w_in_odd.astype(BF16), w_out_odd.astype(BF16)
    cache_k = cache_swa_k.reshape(N_EVEN, DEC_BATCH, win, A_KV_W)
    cache_v = cache_swa_v.reshape(N_EVEN, DEC_BATCH, win, A_KV_W)

    k_p, v_p, s_p, c_p, k_s, v_s, s_s, c_s = ([] for _ in range(8))
    for layer in range(DEPTH):
        gains = norm_gains[layer]
        j = layer // 2
        x = _ffn(x, gains, wg, wu, wd, layer, 0)
        if layer % 2 == 0:
            q, k, v, hq, hk, hb, hv, hg = _even_in(x, gains, w_in_e, hgrn_lb_logits, j)
            oa = _attn_prompt(attn_sinks[j], q, k, v)
            oa = _attn_sample(attn_sinks[j], q, k, v, cache_k[j], cache_v[j], oa)
            ob, st_p = _hgrn_prompt(hq, hk, hb, hv, hg, hgrn_norm_gain[j])
            ob, st_s = _hgrn_sample(hq, hk, hb, hv, hg, hgrn_norm_gain[j], state_hgrn[j], ob)
            x = _even_out(x, gains, oa, ob, w_out_e, j)
            for new, tail_p, tail_s, cache in ((k, k_p, k_s, cache_k[j]), (v, v_p, v_s, cache_v[j])):
                tail_p.append(_stream_tails(new, 0, BATCH, LP, WINDOW).reshape(
                    BATCH, WINDOW, A_KV_HEADS, A_HEAD_DIM))
                both = jnp.concatenate([cache, new[P_ROWS:].reshape(DEC_BATCH, DEC_SEQ, A_KV_W)], axis=1)
                tail_s.append(both[:, both.shape[1] - win:].reshape(DEC_BATCH, win, A_KV_HEADS, A_HEAD_DIM))
            s_p.append(st_p)
            s_s.append(st_s)
        else:
            bg, u = _odd_in(x, gains, w_in_o, j)
            x = _odd_out_prompt(x, gains, bg, u, conv_w, w_out_o, j)
            x = _odd_out_sample(x, gains, bg, u, cache_conv[j], conv_w, w_out_o, j)
            c_p.append(_stream_tails(u, 0, BATCH, LP, CONV_WIDTH - 1))
            c_s.append(_stream_tails(u, P_ROWS, DEC_BATCH, DEC_SEQ, CONV_WIDTH - 1))
        x = _ffn(x, gains, wg, wu, wd, layer, 1)

    y_prompt = _stream_tails(x, 0, BATCH, LP, SEQ)
    y_sample = x[P_ROWS:].reshape(DEC_BATCH, DEC_SEQ, D_MODEL)
    return (y_prompt, y_sample, jnp.stack(k_p), jnp.stack(v_p), jnp.stack(s_p), jnp.stack(c_p),
            jnp.stack(k_s), jnp.stack(v_s), jnp.stack(s_s), jnp.stack(c_s))
```

```python
import functools

import jax
import jax.numpy as jnp
from jax import lax
from jax.experimental import pallas as pl
from jax.experimental.pallas import tpu as pltpu

F32 = jnp.float32
BF16 = jnp.bfloat16

D_MODEL = 1024
BATCH = 4
SEQ = 4096
DEPTH = 4
DEC_BATCH = 8
DEC_SEQ = 32
CHUNK = 64
N_META = 16
WINDOW = 128
WINDOW_CHUNKS = WINDOW // CHUNK
A_HEADS = 8
A_KV_HEADS = 2
A_HEAD_DIM = 64
A_Q_W = A_HEADS * A_HEAD_DIM
A_KV_W = A_KV_HEADS * A_HEAD_DIM
B_HEADS = 4
B_KEY_DIM = 128
B_VAL_DIM = 128
B_QK_W = B_HEADS * B_KEY_DIM
B_V_W = B_HEADS * B_VAL_DIM
B_BLOCK = 16
CONV_WIDTH = 3
D_FF = 2816
FFN_RESIDUAL = 0.5
N_EVEN = (DEPTH + 1) // 2
N_ODD = DEPTH // 2
EVEN_IN_W = A_Q_W + 2 * A_KV_W + 2 * B_QK_W + 2 * B_V_W
EPS = 1e-6
MASK_VALUE = -1e30
LB_FLOOR = 1e-30

PAD = (-N_META) % CHUNK
LP = PAD + N_META + SEQ
N_CHUNKS = LP // CHUNK
P_ROWS = BATCH * LP
S_ROWS = DEC_BATCH * DEC_SEQ
N_ROWS = P_ROWS + S_ROWS
S_BLOCK0 = P_ROWS // DEC_SEQ

ROW_TILE = 512
FF_CHUNK = 256
HGRN_TILE = 832
HGRN_ROWS = 64
HGRN_SUB = 8
HGRN_GROUP = 2
LOG2_E = 1.4426950408889634
CUMSUM_ROWS = 256
FFN_TILE = 1536
FFN_SUB = 512
ATTN_CHUNKS = 5
ODD_TILE = 832
ODD_HALO = 16
TAIL_ROWS = 8
VMEM_LIMIT = 56 * 1024 * 1024

assert N_ROWS % ROW_TILE == 0 and D_FF % FF_CHUNK == 0
assert LP % HGRN_TILE == 0 and HGRN_TILE % HGRN_ROWS == 0 and ROW_TILE % HGRN_ROWS == 0
assert HGRN_ROWS % DEC_SEQ == 0 and DEC_SEQ % HGRN_SUB == 0
assert ROW_TILE % CUMSUM_ROWS == 0 and CUMSUM_ROWS % HGRN_ROWS == 0
assert N_ROWS % FFN_TILE == 0 and FFN_TILE % FFN_SUB == 0 and N_CHUNKS % ATTN_CHUNKS == 0
assert LP % ODD_TILE == 0 and ODD_TILE % ODD_HALO == 0 and P_ROWS % DEC_SEQ == 0
assert P_ROWS % S_ROWS == 0 and S_ROWS <= ODD_TILE and TAIL_ROWS >= CONV_WIDTH - 1


def _params(*sem):
    return pltpu.CompilerParams(dimension_semantics=sem, vmem_limit_bytes=VMEM_LIMIT)


def _rms(x, gain):
    return x * lax.rsqrt(jnp.mean(x * x, axis=-1, keepdims=True) + EPS) * gain


def _silu(x):
    return x * jax.nn.sigmoid(x)


def _dot(a, b):
    return jnp.dot(a, b, preferred_element_type=F32)


def _dot_nt(a, b):
    return lax.dot_general(a, b, (((1,), (1,)), ((), ())), preferred_element_type=F32)


def _dot_tn(a, b):
    return lax.dot_general(a, b, (((0,), (0,)), ((), ())), preferred_element_type=F32)


def _is_pad_row(row0, rows):
    r = row0 + lax.broadcasted_iota(jnp.int32, (rows, 1), 0)
    pad = None
    for b in range(BATCH):
        hit = jnp.logical_and(r >= b * LP, r < b * LP + PAD)
        pad = hit if pad is None else jnp.logical_or(pad, hit)
    return pad


def _ffn_body(sel_ref, x_ref, g_ref, wg_ref, wu_ref, wd_ref, o_ref):
    pre = sel_ref[1] * 4
    post = pre + 1
    pieces = [slice(lo, lo + FFN_SUB) for lo in range(0, FFN_TILE, FFN_SUB)]
    chunks = [slice(c * FF_CHUNK, (c + 1) * FF_CHUNK) for c in range(D_FF // FF_CHUNK)]
    xs = [x_ref[rows, :] for rows in pieces]
    hs = [_rms(x, g_ref[pl.ds(pre, 1), :]).astype(BF16) for x in xs]
    accs = []
    for h in hs:
        acc = jnp.zeros((FFN_SUB, D_MODEL), F32)
        for cols in chunks:
            act = (_silu(_dot(h, wg_ref[:, cols])) * _dot(h, wu_ref[:, cols])).astype(BF16)
            acc = acc + _dot(act, wd_ref[cols, :])
        accs.append(acc)
    for rows, x, acc in zip(pieces, xs, accs):
        o_ref[rows, :] = x + FFN_RESIDUAL * _rms(acc, g_ref[pl.ds(post, 1), :])


def _ffn(x, norm_gains, wg, wu, wd, layer, which):
    row = lambda i, sel: (i, 0)
    wsel = lambda i, sel: (sel[0], sel[1], 0, 0)
    resident = pl.Buffered(1)
    grid_spec = pltpu.PrefetchScalarGridSpec(
        num_scalar_prefetch=1,
        grid=(N_ROWS // FFN_TILE,),
        in_specs=[
            pl.BlockSpec((FFN_TILE, D_MODEL), row),
            pl.BlockSpec((None, 6, D_MODEL), lambda i, sel: (sel[0], 0, 0)),
            pl.BlockSpec((None, None, D_MODEL, D_FF), wsel, pipeline_mode=resident),
            pl.BlockSpec((None, None, D_MODEL, D_FF), wsel, pipeline_mode=resident),
            pl.BlockSpec((None, None, D_FF, D_MODEL), wsel, pipeline_mode=resident),
        ],
        out_specs=pl.BlockSpec((FFN_TILE, D_MODEL), row),
    )
    return pl.pallas_call(
        _ffn_body,
        out_shape=jax.ShapeDtypeStruct((N_ROWS, D_MODEL), F32),
        grid_spec=grid_spec,
        input_output_aliases={1: 0},
        compiler_params=_params("parallel"),
        name="ffn",
    )(jnp.array([layer, which], jnp.int32), x, norm_gains, wg, wu, wd)


def _lower_bound(lbl_ref, j):
    rows = [lbl_ref[i:i + 1, :] for i in range(N_EVEN)]
    top = functools.reduce(jnp.maximum, rows)
    e = [jnp.exp(r - top) for r in rows]
    total = functools.reduce(lambda a, b: a + b, e)
    p = [v / total for v in e]
    cum = functools.reduce(lambda a, b: a + b, p[:j + 1])
    return jnp.maximum(cum - p[0], 0.0)


def _span_cumsum(g, row0):
    out = []
    for lo in range(0, g.shape[0], CUMSUM_ROWS):
        part = g[lo:lo + CUMSUM_ROWS, :]
        t = row0 + lo + lax.broadcasted_iota(jnp.int32, (CUMSUM_ROWS, CUMSUM_ROWS), 0)
        u = row0 + lo + lax.broadcasted_iota(jnp.int32, (CUMSUM_ROWS, CUMSUM_ROWS), 1)
        span = jnp.where(t >= P_ROWS, DEC_SEQ, HGRN_ROWS)
        tri = jnp.where(jnp.logical_and((t ^ u) < span, u <= t), 1.0, 0.0).astype(BF16)
        total = None
        for _ in range(3):
            term = part.astype(BF16)
            part = part - term.astype(F32)
            total = _dot(tri, term) if total is None else total + _dot(tri, term)
        out.append(total)
    return jnp.concatenate(out, axis=0)


def _even_in_body(x_ref, g_ref, w_ref, lbl_ref, q_ref, k_ref, v_ref, kk_ref, vv_ref,
                  hq_ref, hk_ref, hb_ref, hv_ref, hg_ref, *, j):
    h = _rms(x_ref[...], g_ref[2:3, :]).astype(BF16)
    widths = (A_Q_W, 2 * A_KV_W, B_QK_W, B_QK_W, B_V_W, B_V_W)
    starts = [sum(widths[:i]) for i in range(len(widths))]
    proj = lambda i: _dot(h, w_ref[:, starts[i]:starts[i] + widths[i]])
    fx = proj(3)
    q_ref[...] = (proj(0) * (A_HEAD_DIM ** -0.5)).astype(BF16)
    kv = proj(1)
    k_ref[...] = kv[:, :A_KV_W]
    v_ref[...] = kv[:, A_KV_W:]
    kk_ref[...] = jnp.concatenate(_half_lane_variants(kv[:, :A_KV_W]), axis=1)
    vv_ref[...] = jnp.concatenate(_half_lane_variants(kv[:, A_KV_W:]), axis=1)
    hq_ref[...] = _silu(proj(2))
    hv_ref[...] = proj(4).astype(BF16)
    hg_ref[...] = _silu(proj(5))
    lb = _lower_bound(lbl_ref, j)
    soft = jnp.log1p(jnp.exp(-jnp.abs(fx)))
    ls_pos = jnp.minimum(fx, 0.0) - soft
    ls_neg = jnp.minimum(-fx, 0.0) - soft
    other = jnp.log(jnp.maximum(lb, LB_FLOOR)) + ls_neg
    log_f = jnp.maximum(ls_pos, other) + jnp.log1p(jnp.exp(-jnp.abs(ls_pos - other)))
    hk_ref[...] = (1.0 - lb) * jax.nn.sigmoid(-fx)
    hb_ref[...] = _span_cumsum(log_f, pl.program_id(0) * ROW_TILE)


def _even_in(x, gains, w_in, lb_logits, j):
    row = lambda i: (i, 0)
    whole = lambda i: (0, 0)
    wide = lambda w, dt: jax.ShapeDtypeStruct((N_ROWS, w), dt)
    spec = lambda w: pl.BlockSpec((ROW_TILE, w), row)
    return pl.pallas_call(
        functools.partial(_even_in_body, j=j),
        out_shape=(wide(A_Q_W, BF16), wide(A_KV_W, F32), wide(A_KV_W, F32),
                   wide(4 * A_KV_W, BF16), wide(4 * A_KV_W, BF16),
                   wide(B_QK_W, F32), wide(B_QK_W, F32), wide(B_QK_W, F32),
                   wide(B_V_W, BF16), wide(B_V_W, F32)),
        grid=(N_ROWS // ROW_TILE,),
        in_specs=[
            pl.BlockSpec((ROW_TILE, D_MODEL), row),
            pl.BlockSpec((6, D_MODEL), whole),
            pl.BlockSpec((None, D_MODEL, EVEN_IN_W), lambda i: (j, 0, 0),
                         pipeline_mode=pl.Buffered(1)),
            pl.BlockSpec((N_EVEN, B_QK_W), whole),
        ],
        out_specs=(spec(A_Q_W), spec(A_KV_W), spec(A_KV_W), spec(4 * A_KV_W), spec(4 * A_KV_W),
                   spec(B_QK_W), spec(B_QK_W), spec(B_QK_W), spec(B_V_W), spec(B_V_W)),
        compiler_params=_params("parallel"),
        name=f"even_in_{j}",
    )(x, gains, w_in, lb_logits)


def _half_lane_variants(a):
    lane = lax.broadcasted_iota(jnp.int32, a.shape, 1)
    low = lane < A_HEAD_DIM
    swapped = pltpu.roll(a, A_HEAD_DIM, 1)
    zero = jnp.zeros_like(a)
    return (jnp.where(low, a, zero).astype(BF16), jnp.where(low, zero, swapped).astype(BF16),
            jnp.where(low, swapped, zero).astype(BF16), jnp.where(low, zero, a).astype(BF16))


def _sink_softmax(s, valid, sink):
    if valid is not None:
        s = jnp.where(valid, s, MASK_VALUE)
    m = jnp.maximum(jnp.max(s, axis=-1, keepdims=True), sink)
    p = jnp.exp(s - m)
    denom = jnp.sum(p, axis=-1, keepdims=True) + jnp.exp(sink - m)
    return (p / denom).astype(BF16)


def _group_sinks(sink_ref, kvh, rows):
    first = lax.broadcasted_iota(jnp.int32, (2 * rows, 1), 0) < rows
    base = kvh * (A_HEADS // A_KV_HEADS)
    return (jnp.where(first, sink_ref[base], sink_ref[base + 2]),
            jnp.where(first, sink_ref[base + 1], sink_ref[base + 3]))


def _attend(items, sinks):
    scores = []
    for q, keys, _, _ in items:
        for kvh in range(A_KV_HEADS):
            c0 = kvh * 256
            q2 = jnp.concatenate([q[:, c0:c0 + 128], q[:, c0 + 128:c0 + 256]], axis=0)
            scores += [_dot_nt(q2, keys[2 * kvh]), _dot_nt(q2, keys[2 * kvh + 1])]
    weights = []
    for n, (_, _, _, valid) in enumerate(items):
        for kvh in range(A_KV_HEADS):
            for half in range(2):
                weights.append(_sink_softmax(scores[4 * n + 2 * kvh + half], valid, sinks[kvh][half]))
    outs = []
    for n, (q, _, vals, _) in enumerate(items):
        rows = q.shape[0]
        groups = []
        for kvh in range(A_KV_HEADS):
            w = weights[4 * n + 2 * kvh:4 * n + 2 * kvh + 2]
            o = _dot(w[0], vals[2 * kvh]) + _dot(w[1], vals[2 * kvh + 1])
            groups.append(jnp.concatenate([o[:rows], o[rows:]], axis=1))
        outs.append(jnp.concatenate(groups, axis=1))
    return outs


def _attn_prompt_body(sink_ref, q_ref, kk_ref, vv_ref, o_ref):
    span = (WINDOW_CHUNKS + 1) * CHUNK
    variant = [slice(v * A_KV_W, (v + 1) * A_KV_W) for v in range(4)]
    sinks = [_group_sinks(sink_ref, kvh, CHUNK) for kvh in range(A_KV_HEADS)]

    def step(n, carry):
        starts, items = [], []
        for i in range(ATTN_CHUNKS):
            c = n * ATTN_CHUNKS + i
            r0 = pl.multiple_of(c * CHUNK, CHUNK)
            s0 = pl.multiple_of(jnp.maximum(c - WINDOW_CHUNKS, 0) * CHUNK, CHUNK)
            kpos = s0 + lax.broadcasted_iota(jnp.int32, (1, span), 1)
            valid = jnp.logical_and(kpos >= PAD, kpos < r0 + CHUNK)
            starts.append(r0)
            items.append((q_ref[pl.ds(r0, CHUNK), :],
                          [kk_ref[pl.ds(s0, span), v] for v in variant],
                          [vv_ref[pl.ds(s0, span), v] for v in variant], valid))
        for r0, out in zip(starts, _attend(items, sinks)):
            o_ref[pl.ds(r0, CHUNK), :] = out.astype(BF16)
        return carry

    lax.fori_loop(0, N_CHUNKS // ATTN_CHUNKS, step, 0)


def _attn_prompt(sink, q, kk, vv):
    seq = lambda b: (b, 0)
    return pl.pallas_call(
        _attn_prompt_body,
        out_shape=jax.ShapeDtypeStruct((N_ROWS, A_Q_W), BF16),
        grid=(BATCH,),
        in_specs=[
            pl.BlockSpec(memory_space=pltpu.SMEM),
            pl.BlockSpec((LP, A_Q_W), seq),
            pl.BlockSpec((LP, 4 * A_KV_W), seq),
            pl.BlockSpec((LP, 4 * A_KV_W), seq),
        ],
        out_specs=pl.BlockSpec((LP, A_Q_W), seq),
        compiler_params=_params("parallel"),
        name="attn_prompt",
    )(sink, q, kk, vv)


def _attn_sample_body(sink_ref, q_ref, k_ref, v_ref, ck_ref, cv_ref, o_in_ref, o_ref):
    del o_in_ref
    keys = _half_lane_variants(jnp.concatenate([ck_ref[...], k_ref[...]], axis=0))
    vals = _half_lane_variants(jnp.concatenate([cv_ref[...], v_ref[...]], axis=0))
    sinks = [_group_sinks(sink_ref, kvh, DEC_SEQ) for kvh in range(A_KV_HEADS)]
    o_ref[...] = _attend([(q_ref[...], keys, vals, None)], sinks)[0].astype(BF16)


def _attn_sample(sink, q, k, v, cache_k, cache_v, o_prompt):
    win = cache_k.shape[1]
    new = lambda b: (S_BLOCK0 + b, 0)
    old = lambda b: (b, 0, 0)
    return pl.pallas_call(
        _attn_sample_body,
        out_shape=jax.ShapeDtypeStruct((N_ROWS, A_Q_W), BF16),
        grid=(DEC_BATCH,),
        in_specs=[
            pl.BlockSpec(memory_space=pltpu.SMEM),
            pl.BlockSpec((DEC_SEQ, A_Q_W), new),
            pl.BlockSpec((DEC_SEQ, A_KV_W), new),
            pl.BlockSpec((DEC_SEQ, A_KV_W), new),
            pl.BlockSpec((None, win, A_KV_W), old),
            pl.BlockSpec((None, win, A_KV_W), old),
            pl.BlockSpec(memory_space=pl.ANY),
        ],
        out_specs=pl.BlockSpec((DEC_SEQ, A_Q_W), new),
        input_output_aliases={6: 0},
        compiler_params=_params("parallel"),
        name="attn_sample",
    )(sink, q, k, v, cache_k, cache_v, o_prompt)


def _group_row(a, group, row):
    spans = a.shape[0] // group
    picked = a.reshape(spans, group, a.shape[1])[:, row:row + 1, :]
    return jnp.broadcast_to(picked, (spans, group, a.shape[1])).reshape(a.shape)


def _hgrn_pairs(q, k, c2):
    parts = []
    for s in range(HGRN_SUB):
        decay = jnp.exp2(jnp.minimum(c2 - _group_row(c2, HGRN_SUB, s), 0.0))
        parts.append((q * _group_row(k, HGRN_SUB, s) * decay).astype(BF16))
    return jnp.concatenate(parts, axis=1)


def _hgrn_pick(rows):
    shape = (HGRN_SUB * B_KEY_DIM, rows)
    block = lax.broadcasted_iota(jnp.int32, shape, 0) // B_KEY_DIM
    col = lax.broadcasted_iota(jnp.int32, shape, 1) & (HGRN_SUB - 1)
    return jnp.where(block == col, 1.0, 0.0).astype(BF16)


def _hgrn_head(q, c, v, state_t, intra, levels, update):
    rows = q.shape[0]
    t_idx = lax.broadcasted_iota(jnp.int32, (rows, rows), 0)
    u_idx = lax.broadcasted_iota(jnp.int32, (rows, rows), 1)
    apart = t_idx ^ u_idx
    in_group = jnp.logical_and(apart < HGRN_SUB, u_idx <= t_idx)
    scores = jnp.where(in_group, intra, 0.0)
    for w, level in levels:
        scores = scores + (level if 2 * w == rows else jnp.where(apart < 2 * w, level, 0.0))
    o = _dot(scores.astype(BF16), v) + _dot_nt((q * jnp.exp2(c)).astype(BF16), state_t.astype(BF16))
    return o, state_t * jnp.exp2(c[rows - 1:rows, :]) + update


def _hgrn_levels(q, k, c):
    rows = q.shape[0]
    pos = lax.broadcasted_iota(jnp.int32, q.shape, 0)
    levels = []
    w = HGRN_SUB
    while w < rows:
        rho = _group_row(c, 2 * w, w - 1)
        upper = (pos & w) != 0
        q_up = jnp.where(upper, q * jnp.exp2(jnp.minimum(c - rho, 0.0)), 0.0).astype(BF16)
        k_low = jnp.where(upper, 0.0, k * jnp.exp2(jnp.minimum(rho - c, 0.0))).astype(BF16)
        levels.append((w, _dot_nt(q_up, k_low)))
        w *= 2
    return levels


def _hgrn_tiles(tiles, n, hq_ref, hk_ref, hc_ref, hv_ref, hg_ref, gain_ref, o_ref, st_ref, pick):
    loaded = [(hq_ref[r, :], hk_ref[r, :], hc_ref[r, :] * LOG2_E, hv_ref[r, :], hg_ref[r, :])
              for r in tiles]
    states = [st_ref[hd] for hd in range(B_HEADS)]
    heads = [slice(hd * B_KEY_DIM, (hd + 1) * B_KEY_DIM) for hd in range(B_HEADS)]
    intras = [_dot(jnp.concatenate([_hgrn_pairs(q[:, h], k[:, h], c[:, h]) for h in heads], axis=0), pick)
              for q, k, c, _, _ in loaded]
    levels = [[_hgrn_levels(q[:, h], k[:, h], c[:, h]) for h in heads] for q, k, c, _, _ in loaded]
    updates = [[_dot_tn(v[:, h], (k[:, h] * jnp.exp2(c[n - 1:n, h] - c[:, h])).astype(BF16))
                for h in heads] for _, k, c, v, _ in loaded]
    results = []
    for t, (q, k, c, v, g) in enumerate(loaded):
        outs = []
        for hd, h in enumerate(heads):
            o, states[hd] = _hgrn_head(q[:, h], c[:, h], v[:, h], states[hd],
                                       intras[t][hd * n:(hd + 1) * n, :], levels[t][hd], updates[t][hd])
            outs.append(_rms(o, gain_ref[hd:hd + 1, :]) * g[:, h])
        results.append(jnp.concatenate(outs, axis=1).astype(BF16))
    for r, res in zip(tiles, results):
        o_ref[r, :] = res
    for hd in range(B_HEADS):
        st_ref[hd] = states[hd]


def _hgrn_prompt_body(hq_ref, hk_ref, hc_ref, hv_ref, hg_ref, gain_ref, o_ref, s_ref, st_ref):
    @pl.when(pl.program_id(1) == 0)
    def _():
        st_ref[...] = jnp.zeros(st_ref.shape, F32)

    pick = _hgrn_pick(HGRN_ROWS)
    refs = (hq_ref, hk_ref, hc_ref, hv_ref, hg_ref, gain_ref, o_ref, st_ref, pick)
    n_tiles = HGRN_TILE // HGRN_ROWS

    def step(n, carry):
        first = n * HGRN_GROUP
        tiles = [pl.ds(pl.multiple_of((first + i) * HGRN_ROWS, HGRN_ROWS), HGRN_ROWS)
                 for i in range(HGRN_GROUP)]
        _hgrn_tiles(tiles, HGRN_ROWS, *refs)
        return carry

    lax.fori_loop(0, n_tiles // HGRN_GROUP, step, 0)
    rest = [pl.ds(i * HGRN_ROWS, HGRN_ROWS) for i in range(n_tiles - n_tiles % HGRN_GROUP, n_tiles)]
    if rest:
        _hgrn_tiles(rest, HGRN_ROWS, *refs)

    @pl.when(pl.program_id(1) == pl.num_programs(1) - 1)
    def _():
        for hd in range(B_HEADS):
            s_ref[hd] = st_ref[hd].T


def _hgrn_prompt(hq, hk, hb, hv, hg, gain):
    tiles = LP // HGRN_TILE
    row = lambda b, i: (b * tiles + i, 0)
    spec = pl.BlockSpec((HGRN_TILE, B_QK_W), row)
    return pl.pallas_call(
        _hgrn_prompt_body,
        out_shape=(jax.ShapeDtypeStruct((N_ROWS, B_V_W), BF16),
                   jax.ShapeDtypeStruct((BATCH, B_HEADS, B_KEY_DIM, B_VAL_DIM), F32)),
        grid=(BATCH, tiles),
        in_specs=[spec, spec, spec, spec, spec,
                  pl.BlockSpec((B_HEADS, B_VAL_DIM), lambda b, i: (0, 0))],
        out_specs=(spec, pl.BlockSpec((None, B_HEADS, B_KEY_DIM, B_VAL_DIM),
                                      lambda b, i: (b, 0, 0, 0))),
        scratch_shapes=[pltpu.VMEM((B_HEADS, B_VAL_DIM, B_KEY_DIM), F32)],
        compiler_params=_params("parallel", "arbitrary"),
        name="hgrn_prompt",
    )(hq, hk, hb, hv, hg, gain)


def _hgrn_sample_body(hq_ref, hk_ref, hb_ref, hv_ref, hg_ref, gain_ref, s0_ref, o_in_ref,
                      o_ref, s_ref, st_ref):
    del o_in_ref
    for hd in range(B_HEADS):
        st_ref[hd] = s0_ref[hd].T
    _hgrn_tiles([pl.ds(0, DEC_SEQ)], DEC_SEQ, hq_ref, hk_ref, hb_ref, hv_ref, hg_ref, gain_ref, o_ref,
                st_ref, _hgrn_pick(DEC_SEQ))
    for hd in range(B_HEADS):
        s_ref[hd] = st_ref[hd].T


def _hgrn_sample(hq, hk, hb, hv, hg, gain, s0, o_prompt):
    row = lambda b: (S_BLOCK0 + b, 0)
    spec = pl.BlockSpec((DEC_SEQ, B_QK_W), row)
    state = pl.BlockSpec((None, B_HEADS, B_KEY_DIM, B_VAL_DIM), lambda b: (b, 0, 0, 0))
    return pl.pallas_call(
        _hgrn_sample_body,
        out_shape=(jax.ShapeDtypeStruct((N_ROWS, B_V_W), BF16),
                   jax.ShapeDtypeStruct((DEC_BATCH, B_HEADS, B_KEY_DIM, B_VAL_DIM), F32)),
        grid=(DEC_BATCH,),
        in_specs=[spec, spec, spec, spec, spec,
                  pl.BlockSpec((B_HEADS, B_VAL_DIM), lambda b: (0, 0)),
                  state, pl.BlockSpec(memory_space=pl.ANY)],
        out_specs=(spec, state),
        scratch_shapes=[pltpu.VMEM((B_HEADS, B_VAL_DIM, B_KEY_DIM), F32)],
        input_output_aliases={7: 0},
        compiler_params=_params("parallel"),
        name="hgrn_sample",
    )(hq, hk, hb, hv, hg, gain, s0, o_prompt)


def _even_out_body(x_ref, g_ref, oa_ref, ob_ref, w_ref, o_ref):
    m = _dot(oa_ref[...], w_ref[:A_Q_W, :]) + _dot(ob_ref[...], w_ref[A_Q_W:, :])
    y = x_ref[...] + _rms(m, g_ref[3:4, :])
    pad = _is_pad_row(pl.program_id(0) * ROW_TILE, ROW_TILE)
    o_ref[...] = jnp.where(pad, 0.0, y)


def _even_out(x, gains, oa, ob, w_out, j):
    row = lambda i: (i, 0)
    return pl.pallas_call(
        _even_out_body,
        out_shape=jax.ShapeDtypeStruct((N_ROWS, D_MODEL), F32),
        grid=(N_ROWS // ROW_TILE,),
        in_specs=[
            pl.BlockSpec((ROW_TILE, D_MODEL), row),
            pl.BlockSpec((6, D_MODEL), lambda i: (0, 0)),
            pl.BlockSpec((ROW_TILE, A_Q_W), row),
            pl.BlockSpec((ROW_TILE, B_V_W), row),
            pl.BlockSpec((None, A_Q_W + B_V_W, D_MODEL), lambda i: (j, 0, 0),
                         pipeline_mode=pl.Buffered(1)),
        ],
        out_specs=pl.BlockSpec((ROW_TILE, D_MODEL), row),
        input_output_aliases={0: 0},
        compiler_params=_params("parallel"),
        name=f"even_out_{j}",
    )(x, gains, oa, ob, w_out)


def _conv3(u, cw_ref, before1=None, before2=None):
    shift1, shift2 = pltpu.roll(u, 1, 0), pltpu.roll(u, 2, 0)
    if before1 is not None:
        pos = lax.broadcasted_iota(jnp.int32, (u.shape[0], 1), 0)
        shift1 = jnp.where(pos == 0, before1, shift1)
        shift2 = jnp.where(pos == 0, before2, jnp.where(pos == 1, before1, shift2))
    return shift2 * cw_ref[0:1, :] + shift1 * cw_ref[1:2, :] + u * cw_ref[2:3, :]


def _odd_prompt_body(x_ref, halo_ref, g_ref, w_in_ref, cw_ref, w_out_ref, o_ref, tail_ref):
    is_prompt = pl.program_id(0) < P_ROWS // ODD_TILE

    @pl.when(is_prompt)
    def _():
        x = x_ref[...]
        h = _rms(x, g_ref[2:3, :]).astype(BF16)
        h_ext = jnp.concatenate([_rms(halo_ref[...], g_ref[2:3, :]).astype(BF16), h], axis=0)
        bg = _dot(h, w_in_ref[:, :D_MODEL])
        u_ext = _dot(h_ext, w_in_ref[:, D_MODEL:2 * D_MODEL]) * _dot(h_ext, w_in_ref[:, 2 * D_MODEL:])
        tail_ref[...] = u_ext[ODD_HALO + ODD_TILE - TAIL_ROWS:, :]
        y = _conv3(u_ext, cw_ref)[ODD_HALO:, :]
        m = _dot((bg * y).astype(BF16), w_out_ref[...])
        out = x + _rms(m, g_ref[3:4, :])
        o_ref[...] = jnp.where(_is_pad_row(pl.program_id(0) * ODD_TILE, ODD_TILE), 0.0, out)

    @pl.when(jnp.logical_not(is_prompt))
    def _():
        o_ref[...] = x_ref[...]
        tail_ref[...] = jnp.zeros(tail_ref.shape, F32)


def _odd_prompt(x, gains, w_in, conv_w, w_out, j):
    tiles = pl.cdiv(N_ROWS, ODD_TILE)
    row = lambda i: (i, 0)
    per = ODD_TILE // ODD_HALO
    halo = lambda i: (jnp.maximum(i * per - 1, 0), 0)
    return pl.pallas_call(
        _odd_prompt_body,
        out_shape=(jax.ShapeDtypeStruct((N_ROWS, D_MODEL), F32),
                   jax.ShapeDtypeStruct((tiles, TAIL_ROWS, D_MODEL), F32)),
        grid=(tiles,),
        in_specs=[
            pl.BlockSpec((ODD_TILE, D_MODEL), row),
            pl.BlockSpec((ODD_HALO, D_MODEL), halo),
            pl.BlockSpec((6, D_MODEL), lambda i: (0, 0)),
            pl.BlockSpec((None, D_MODEL, 3 * D_MODEL), lambda i: (j, 0, 0), pipeline_mode=pl.Buffered(1)),
            pl.BlockSpec((None, CONV_WIDTH, D_MODEL), lambda i: (j, 0, 0)),
            pl.BlockSpec((None, D_MODEL, D_MODEL), lambda i: (j, 0, 0), pipeline_mode=pl.Buffered(1)),
        ],
        out_specs=(pl.BlockSpec((ODD_TILE, D_MODEL), row),
                   pl.BlockSpec((None, TAIL_ROWS, D_MODEL), lambda i: (i, 0, 0))),
        compiler_params=_params("parallel"),
        name=f"odd_prompt_{j}",
    )(x, x, gains, w_in, conv_w, w_out)


def _odd_sample_body(x_ref, g_ref, w_in_ref, cw_ref, w_out_ref, cache_ref, o_ref, tail_ref):
    x = x_ref[...]
    h = _rms(x, g_ref[2:3, :]).astype(BF16)
    bg = _dot(h, w_in_ref[:, :D_MODEL])
    u = _dot(h, w_in_ref[:, D_MODEL:2 * D_MODEL]) * _dot(h, w_in_ref[:, 2 * D_MODEL:])
    ys = []
    for b in range(DEC_BATCH):
        ub = u[b * DEC_SEQ:(b + 1) * DEC_SEQ, :]
        ys.append(_conv3(ub, cw_ref, before1=cache_ref[b, 1:2, :], before2=cache_ref[b, 0:1, :]))
        tail_ref[b] = ub[DEC_SEQ - (CONV_WIDTH - 1):, :]
    m = _dot((bg * jnp.concatenate(ys, axis=0)).astype(BF16), w_out_ref[...])
    o_ref[...] = x + _rms(m, g_ref[3:4, :])


def _odd_sample(x, gains, w_in, conv_w, w_out, cache, j):
    rows = lambda i: (P_ROWS // S_ROWS, 0)
    whole3 = lambda i: (0, 0, 0)
    return pl.pallas_call(
        _odd_sample_body,
        out_shape=(jax.ShapeDtypeStruct((N_ROWS, D_MODEL), F32),
                   jax.ShapeDtypeStruct((DEC_BATCH, CONV_WIDTH - 1, D_MODEL), F32)),
        grid=(1,),
        in_specs=[
            pl.BlockSpec((S_ROWS, D_MODEL), rows),
            pl.BlockSpec((6, D_MODEL), lambda i: (0, 0)),
            pl.BlockSpec((None, D_MODEL, 3 * D_MODEL), lambda i: (j, 0, 0), pipeline_mode=pl.Buffered(1)),
            pl.BlockSpec((None, CONV_WIDTH, D_MODEL), lambda i: (j, 0, 0)),
            pl.BlockSpec((None, D_MODEL, D_MODEL), lambda i: (j, 0, 0), pipeline_mode=pl.Buffered(1)),
            pl.BlockSpec((DEC_BATCH, CONV_WIDTH - 1, D_MODEL), whole3),
        ],
        out_specs=(pl.BlockSpec((S_ROWS, D_MODEL), rows),
                   pl.BlockSpec((DEC_BATCH, CONV_WIDTH - 1, D_MODEL), whole3)),
        input_output_aliases={0: 0},
        compiler_params=_params("arbitrary"),
        name=f"odd_sample_{j}",
    )(x, gains, w_in, conv_w, w_out, cache)


def _stream_tails(a, row0, streams, length, n):
    ends = [row0 + (s + 1) * length for s in range(streams)]
    return jnp.stack([lax.slice_in_dim(a, e - n, e, axis=0) for e in ends])


def kernel(x_prompt, x_sample, cache_swa_k, cache_swa_v, state_hgrn, cache_conv, meta_tokens, norm_gains,
           w_ffn_gate, w_ffn_up, w_ffn_down, w_in_even, w_out_even, attn_sinks, hgrn_lb_logits,
           hgrn_norm_gain, w_in_odd, conv_w, w_out_odd):
    win = cache_swa_k.shape[2]
    lead = jnp.concatenate([jnp.zeros((PAD, D_MODEL), F32), meta_tokens.astype(F32)], axis=0)
    pieces = [p for b in range(BATCH) for p in (lead, x_prompt[b])]
    x = jnp.concatenate(pieces + [x_sample.reshape(S_ROWS, D_MODEL)], axis=0)

    wg, wu, wd = (w.astype(BF16) for w in (w_ffn_gate, w_ffn_up, w_ffn_down))
    w_in_e, w_out_e = w_in_even.astype(BF16), w_out_even.astype(BF16)
    w_in_o, w_out_o = w_in_odd.astype(BF16), w_out_odd.astype(BF16)
    cache_k = cache_swa_k.reshape(N_EVEN, DEC_BATCH, win, A_KV_W)
    cache_v = cache_swa_v.reshape(N_EVEN, DEC_BATCH, win, A_KV_W)

    k_p, v_p, s_p, c_p, k_s, v_s, s_s, c_s = ([] for _ in range(8))
    for layer in range(DEPTH):
        gains = norm_gains[layer]
        j = layer // 2
        x = _ffn(x, norm_gains, wg, wu, wd, layer, 0)
        if layer % 2 == 0:
            q, k, v, kk, vv, hq, hk, hb, hv, hg = _even_in(x, gains, w_in_e, hgrn_lb_logits, j)
            oa = _attn_prompt(attn_sinks[j], q, kk, vv)
            oa = _attn_sample(attn_sinks[j], q, k, v, cache_k[j], cache_v[j], oa)
            ob, st_p = _hgrn_prompt(hq, hk, hb, hv, hg, hgrn_norm_gain[j])
            ob, st_s = _hgrn_sample(hq, hk, hb, hv, hg, hgrn_norm_gain[j], state_hgrn[j], ob)
            x = _even_out(x, gains, oa, ob, w_out_e, j)
            for new, tail_p, tail_s, cache in ((k, k_p, k_s, cache_k[j]), (v, v_p, v_s, cache_v[j])):
                tail_p.append(_stream_tails(new, 0, BATCH, LP, WINDOW).reshape(
                    BATCH, WINDOW, A_KV_HEADS, A_HEAD_DIM))
                both = jnp.concatenate([cache, new[P_ROWS:].reshape(DEC_BATCH, DEC_SEQ, A_KV_W)], axis=1)
                tail_s.append(both[:, both.shape[1] - win:].reshape(DEC_BATCH, win, A_KV_HEADS, A_HEAD_DIM))
            s_p.append(st_p)
            s_s.append(st_s)
        else:
            x, conv_s = _odd_sample(x, gains, w_in_o, conv_w, w_out_o, cache_conv[j], j)
            x, tails = _odd_prompt(x, gains, w_in_o, conv_w, w_out_o, j)
            per_stream = LP // ODD_TILE
            c_p.append(jnp.stack([tails[(b + 1) * per_stream - 1, TAIL_ROWS - (CONV_WIDTH - 1):]
                                  for b in range(BATCH)]))
            c_s.append(conv_s)
        x = _ffn(x, norm_gains, wg, wu, wd, layer, 1)

    y_prompt = _stream_tails(x, 0, BATCH, LP, SEQ)
    y_sample = x[P_ROWS:].reshape(DEC_BATCH, DEC_SEQ, D_MODEL)
    return (y_prompt, y_sample, jnp.stack(k_p), jnp.stack(v_p), jnp.stack(s_p), jnp.stack(c_p),
            jnp.stack(k_s), jnp.stack(v_s), jnp.stack(s_s), jnp.stack(c_s))
```

```python
import functools

import jax
import jax.numpy as jnp
from jax import lax
from jax.experimental import pallas as pl
from jax.experimental.pallas import tpu as pltpu

F32 = jnp.float32
BF16 = jnp.bfloat16

D_MODEL = 1024
BATCH = 4
SEQ = 4096
DEPTH = 4
DEC_BATCH = 8
DEC_SEQ = 32
CHUNK = 64
N_META = 16
WINDOW = 128
WINDOW_CHUNKS = WINDOW // CHUNK
A_HEADS = 8
A_KV_HEADS = 2
A_HEAD_DIM = 64
A_Q_W = A_HEADS * A_HEAD_DIM
A_KV_W = A_KV_HEADS * A_HEAD_DIM
B_HEADS = 4
B_KEY_DIM = 128
B_VAL_DIM = 128
B_QK_W = B_HEADS * B_KEY_DIM
B_V_W = B_HEADS * B_VAL_DIM
B_BLOCK = 16
CONV_WIDTH = 3
D_FF = 2816
FFN_RESIDUAL = 0.5
N_EVEN = (DEPTH + 1) // 2
N_ODD = DEPTH // 2
EVEN_IN_W = A_Q_W + 2 * A_KV_W + 2 * B_QK_W + 2 * B_V_W
EPS = 1e-6
MASK_VALUE = -1e30
LB_FLOOR = 1e-30

PAD = (-N_META) % CHUNK
LP = PAD + N_META + SEQ
N_CHUNKS = LP // CHUNK
P_ROWS = BATCH * LP
S_ROWS = DEC_BATCH * DEC_SEQ
N_ROWS = P_ROWS + S_ROWS
S_BLOCK0 = P_ROWS // DEC_SEQ

ROW_TILE = 512
FF_CHUNK = 256
HGRN_TILE = 832
HGRN_ROWS = 64
HGRN_SUB = 8
HGRN_GROUP = 2
LOG2_E = 1.4426950408889634
CUMSUM_ROWS = 256
FFN_TILE = 1536
FFN_SUB = 512
ATTN_CHUNKS = 5
CAST_ROWS = 256
CAST_STEPS = D_FF // CAST_ROWS
ODD_TILE = 832
ODD_HALO = 16
TAIL_ROWS = 8
VMEM_LIMIT = 56 * 1024 * 1024

assert N_ROWS % ROW_TILE == 0 and D_FF % FF_CHUNK == 0
assert LP % HGRN_TILE == 0 and HGRN_TILE % HGRN_ROWS == 0 and ROW_TILE % HGRN_ROWS == 0
assert HGRN_ROWS % DEC_SEQ == 0 and DEC_SEQ % HGRN_SUB == 0
assert ROW_TILE % CUMSUM_ROWS == 0 and CUMSUM_ROWS % HGRN_ROWS == 0
assert N_ROWS % FFN_TILE == 0 and FFN_TILE % FFN_SUB == 0 and N_CHUNKS % ATTN_CHUNKS == 0
assert D_FF % CAST_ROWS == 0 and CAST_STEPS <= min(P_ROWS // ODD_TILE, P_ROWS // HGRN_TILE)
assert LP % ODD_TILE == 0 and ODD_TILE % ODD_HALO == 0 and P_ROWS % DEC_SEQ == 0
assert P_ROWS % S_ROWS == 0 and S_ROWS <= ODD_TILE and TAIL_ROWS >= CONV_WIDTH - 1


def _params(*sem):
    return pltpu.CompilerParams(dimension_semantics=sem, vmem_limit_bytes=VMEM_LIMIT)


def _rms(x, gain):
    return x * lax.rsqrt(jnp.mean(x * x, axis=-1, keepdims=True) + EPS) * gain


def _silu(x):
    return x * jax.nn.sigmoid(x)


def _dot(a, b):
    return jnp.dot(a, b, preferred_element_type=F32)


def _dot_nt(a, b):
    return lax.dot_general(a, b, (((1,), (1,)), ((), ())), preferred_element_type=F32)


def _dot_tn(a, b):
    return lax.dot_general(a, b, (((0,), (0,)), ((), ())), preferred_element_type=F32)


def _is_pad_row(row0, rows):
    r = row0 + lax.broadcasted_iota(jnp.int32, (rows, 1), 0)
    pad = None
    for b in range(BATCH):
        hit = jnp.logical_and(r >= b * LP, r < b * LP + PAD)
        pad = hit if pad is None else jnp.logical_or(pad, hit)
    return pad


def _ffn_body(sel_ref, x_ref, g_ref, wg_ref, wu_ref, wd_ref, o_ref):
    pre = sel_ref[1] * 4
    post = pre + 1
    pieces = [slice(lo, lo + FFN_SUB) for lo in range(0, FFN_TILE, FFN_SUB)]
    chunks = [slice(c * FF_CHUNK, (c + 1) * FF_CHUNK) for c in range(D_FF // FF_CHUNK)]
    xs = [x_ref[rows, :] for rows in pieces]
    hs = [_rms(x, g_ref[pl.ds(pre, 1), :]).astype(BF16) for x in xs]
    accs = []
    for h in hs:
        acc = jnp.zeros((FFN_SUB, D_MODEL), F32)
        for cols in chunks:
            act = (_silu(_dot(h, wg_ref[:, cols])) * _dot(h, wu_ref[:, cols])).astype(BF16)
            acc = acc + _dot(act, wd_ref[cols, :])
        accs.append(acc)
    for rows, x, acc in zip(pieces, xs, accs):
        o_ref[rows, :] = x + FFN_RESIDUAL * _rms(acc, g_ref[pl.ds(post, 1), :])


def _ffn(x, norm_gains, weights, layer, which):
    row = lambda i, sel: (i, 0)
    whole = lambda i, sel: (0, 0)
    resident = pl.Buffered(1)
    grid_spec = pltpu.PrefetchScalarGridSpec(
        num_scalar_prefetch=1,
        grid=(N_ROWS // FFN_TILE,),
        in_specs=[
            pl.BlockSpec((FFN_TILE, D_MODEL), row),
            pl.BlockSpec((None, 6, D_MODEL), lambda i, sel: (sel[0], 0, 0)),
            pl.BlockSpec((D_MODEL, D_FF), whole, pipeline_mode=resident),
            pl.BlockSpec((D_MODEL, D_FF), whole, pipeline_mode=resident),
            pl.BlockSpec((D_FF, D_MODEL), whole, pipeline_mode=resident),
        ],
        out_specs=pl.BlockSpec((FFN_TILE, D_MODEL), row),
    )
    return pl.pallas_call(
        _ffn_body,
        out_shape=jax.ShapeDtypeStruct((N_ROWS, D_MODEL), F32),
        grid_spec=grid_spec,
        input_output_aliases={1: 0},
        compiler_params=_params("parallel"),
        name="ffn",
    )(jnp.array([layer, which], jnp.int32), x, norm_gains, *weights)


def _cast_plan(targets, step_of):
    slab = lambda *ids: jnp.minimum(step_of(*ids), CAST_STEPS - 1)
    in_specs, out_specs = [], []
    for layer, which in targets:
        for _ in range(3):
            in_specs.append(pl.BlockSpec((None, None, CAST_ROWS, D_MODEL),
                                         lambda *ids, l=layer, w=which: (l, w, slab(*ids), 0)))
            out_specs.append(pl.BlockSpec((CAST_ROWS, D_MODEL), lambda *ids: (slab(*ids), 0)))
    shapes = [jax.ShapeDtypeStruct((D_FF, D_MODEL), BF16)] * len(in_specs)
    return in_specs, out_specs, shapes


def _cast_slabs(step, srcs, dsts):
    @pl.when(step < CAST_STEPS)
    def _():
        for src, dst in zip(srcs, dsts):
            dst[...] = src[...].astype(BF16)


def _lower_bound(lbl_ref, j):
    rows = [lbl_ref[i:i + 1, :] for i in range(N_EVEN)]
    top = functools.reduce(jnp.maximum, rows)
    e = [jnp.exp(r - top) for r in rows]
    total = functools.reduce(lambda a, b: a + b, e)
    p = [v / total for v in e]
    cum = functools.reduce(lambda a, b: a + b, p[:j + 1])
    return jnp.maximum(cum - p[0], 0.0)


def _span_cumsum(g, row0):
    out = []
    for lo in range(0, g.shape[0], CUMSUM_ROWS):
        part = g[lo:lo + CUMSUM_ROWS, :]
        t = row0 + lo + lax.broadcasted_iota(jnp.int32, (CUMSUM_ROWS, CUMSUM_ROWS), 0)
        u = row0 + lo + lax.broadcasted_iota(jnp.int32, (CUMSUM_ROWS, CUMSUM_ROWS), 1)
        span = jnp.where(t >= P_ROWS, DEC_SEQ, HGRN_ROWS)
        tri = jnp.where(jnp.logical_and((t ^ u) < span, u <= t), 1.0, 0.0).astype(BF16)
        total = None
        for _ in range(3):
            term = part.astype(BF16)
            part = part - term.astype(F32)
            total = _dot(tri, term) if total is None else total + _dot(tri, term)
        out.append(total)
    return jnp.concatenate(out, axis=0)


def _even_in_body(x_ref, g_ref, w_ref, lbl_ref, q_ref, k_ref, v_ref, kk_ref, vv_ref,
                  hq_ref, hk_ref, hb_ref, hv_ref, hg_ref, *, j):
    h = _rms(x_ref[...], g_ref[2:3, :]).astype(BF16)
    widths = (A_Q_W, 2 * A_KV_W, B_QK_W, B_QK_W, B_V_W, B_V_W)
    starts = [sum(widths[:i]) for i in range(len(widths))]
    proj = lambda i: _dot(h, w_ref[:, starts[i]:starts[i] + widths[i]])
    fx = proj(3)
    q_ref[...] = (proj(0) * (A_HEAD_DIM ** -0.5)).astype(BF16)
    kv = proj(1)
    k_ref[...] = kv[:, :A_KV_W]
    v_ref[...] = kv[:, A_KV_W:]
    kk_ref[...] = jnp.concatenate(_half_lane_variants(kv[:, :A_KV_W]), axis=1)
    vv_ref[...] = jnp.concatenate(_half_lane_variants(kv[:, A_KV_W:]), axis=1)
    hq_ref[...] = _silu(proj(2))
    hv_ref[...] = proj(4).astype(BF16)
    hg_ref[...] = _silu(proj(5))
    lb = _lower_bound(lbl_ref, j)
    soft = jnp.log1p(jnp.exp(-jnp.abs(fx)))
    ls_pos = jnp.minimum(fx, 0.0) - soft
    ls_neg = jnp.minimum(-fx, 0.0) - soft
    other = jnp.log(jnp.maximum(lb, LB_FLOOR)) + ls_neg
    log_f = jnp.maximum(ls_pos, other) + jnp.log1p(jnp.exp(-jnp.abs(ls_pos - other)))
    hk_ref[...] = (1.0 - lb) * jax.nn.sigmoid(-fx)
    hb_ref[...] = _span_cumsum(log_f, pl.program_id(0) * ROW_TILE)


def _even_in(x, gains, w_in, lb_logits, j):
    row = lambda i: (i, 0)
    whole = lambda i: (0, 0)
    wide = lambda w, dt: jax.ShapeDtypeStruct((N_ROWS, w), dt)
    spec = lambda w: pl.BlockSpec((ROW_TILE, w), row)
    return pl.pallas_call(
        functools.partial(_even_in_body, j=j),
        out_shape=(wide(A_Q_W, BF16), wide(A_KV_W, F32), wide(A_KV_W, F32),
                   wide(4 * A_KV_W, BF16), wide(4 * A_KV_W, BF16),
                   wide(B_QK_W, F32), wide(B_QK_W, F32), wide(B_QK_W, F32),
                   wide(B_V_W, BF16), wide(B_V_W, F32)),
        grid=(N_ROWS // ROW_TILE,),
        in_specs=[
            pl.BlockSpec((ROW_TILE, D_MODEL), row),
            pl.BlockSpec((6, D_MODEL), whole),
            pl.BlockSpec((None, D_MODEL, EVEN_IN_W), lambda i: (j, 0, 0),
                         pipeline_mode=pl.Buffered(1)),
            pl.BlockSpec((N_EVEN, B_QK_W), whole),
        ],
        out_specs=(spec(A_Q_W), spec(A_KV_W), spec(A_KV_W), spec(4 * A_KV_W), spec(4 * A_KV_W),
                   spec(B_QK_W), spec(B_QK_W), spec(B_QK_W), spec(B_V_W), spec(B_V_W)),
        compiler_params=_params("parallel"),
        name=f"even_in_{j}",
    )(x, gains, w_in, lb_logits)


def _half_lane_variants(a):
    lane = lax.broadcasted_iota(jnp.int32, a.shape, 1)
    low = lane < A_HEAD_DIM
    swapped = pltpu.roll(a, A_HEAD_DIM, 1)
    zero = jnp.zeros_like(a)
    return (jnp.where(low, a, zero).astype(BF16), jnp.where(low, zero, swapped).astype(BF16),
            jnp.where(low, swapped, zero).astype(BF16), jnp.where(low, zero, a).astype(BF16))


def _sink_softmax(s, valid, sink):
    if valid is not None:
        s = jnp.where(valid, s, MASK_VALUE)
    m = jnp.maximum(jnp.max(s, axis=-1, keepdims=True), sink)
    p = jnp.exp(s - m)
    denom = jnp.sum(p, axis=-1, keepdims=True) + jnp.exp(sink - m)
    return (p / denom).astype(BF16)


def _group_sinks(sink_ref, kvh, rows):
    first = lax.broadcasted_iota(jnp.int32, (2 * rows, 1), 0) < rows
    base = kvh * (A_HEADS // A_KV_HEADS)
    return (jnp.where(first, sink_ref[base], sink_ref[base + 2]),
            jnp.where(first, sink_ref[base + 1], sink_ref[base + 3]))


def _attend(items, sinks):
    scores = []
    for q, keys, _, _ in items:
        for kvh in range(A_KV_HEADS):
            c0 = kvh * 256
            q2 = jnp.concatenate([q[:, c0:c0 + 128], q[:, c0 + 128:c0 + 256]], axis=0)
            scores += [_dot_nt(q2, keys[2 * kvh]), _dot_nt(q2, keys[2 * kvh + 1])]
    weights = []
    for n, (_, _, _, valid) in enumerate(items):
        for kvh in range(A_KV_HEADS):
            for half in range(2):
                weights.append(_sink_softmax(scores[4 * n + 2 * kvh + half], valid, sinks[kvh][half]))
    outs = []
    for n, (q, _, vals, _) in enumerate(items):
        rows = q.shape[0]
        groups = []
        for kvh in range(A_KV_HEADS):
            w = weights[4 * n + 2 * kvh:4 * n + 2 * kvh + 2]
            o = _dot(w[0], vals[2 * kvh]) + _dot(w[1], vals[2 * kvh + 1])
            groups.append(jnp.concatenate([o[:rows], o[rows:]], axis=1))
        outs.append(jnp.concatenate(groups, axis=1))
    return outs


def _attn_prompt_body(sink_ref, q_ref, kk_ref, vv_ref, o_ref):
    span = (WINDOW_CHUNKS + 1) * CHUNK
    variant = [slice(v * A_KV_W, (v + 1) * A_KV_W) for v in range(4)]
    sinks = [_group_sinks(sink_ref, kvh, CHUNK) for kvh in range(A_KV_HEADS)]

    def step(n, carry):
        starts, items = [], []
        for i in range(ATTN_CHUNKS):
            c = n * ATTN_CHUNKS + i
            r0 = pl.multiple_of(c * CHUNK, CHUNK)
            s0 = pl.multiple_of(jnp.maximum(c - WINDOW_CHUNKS, 0) * CHUNK, CHUNK)
            kpos = s0 + lax.broadcasted_iota(jnp.int32, (1, span), 1)
            valid = jnp.logical_and(kpos >= PAD, kpos < r0 + CHUNK)
            starts.append(r0)
            items.append((q_ref[pl.ds(r0, CHUNK), :],
                          [kk_ref[pl.ds(s0, span), v] for v in variant],
                          [vv_ref[pl.ds(s0, span), v] for v in variant], valid))
        for r0, out in zip(starts, _attend(items, sinks)):
            o_ref[pl.ds(r0, CHUNK), :] = out.astype(BF16)
        return carry

    lax.fori_loop(0, N_CHUNKS // ATTN_CHUNKS, step, 0)


def _attn_prompt(sink, q, kk, vv):
    seq = lambda b: (b, 0)
    return pl.pallas_call(
        _attn_prompt_body,
        out_shape=jax.ShapeDtypeStruct((N_ROWS, A_Q_W), BF16),
        grid=(BATCH,),
        in_specs=[
            pl.BlockSpec(memory_space=pltpu.SMEM),
            pl.BlockSpec((LP, A_Q_W), seq),
            pl.BlockSpec((LP, 4 * A_KV_W), seq),
            pl.BlockSpec((LP, 4 * A_KV_W), seq),
        ],
        out_specs=pl.BlockSpec((LP, A_Q_W), seq),
        compiler_params=_params("parallel"),
        name="attn_prompt",
    )(sink, q, kk, vv)


def _attn_sample_body(sink_ref, q_ref, k_ref, v_ref, ck_ref, cv_ref, o_in_ref, o_ref):
    del o_in_ref
    keys = _half_lane_variants(jnp.concatenate([ck_ref[...], k_ref[...]], axis=0))
    vals = _half_lane_variants(jnp.concatenate([cv_ref[...], v_ref[...]], axis=0))
    sinks = [_group_sinks(sink_ref, kvh, DEC_SEQ) for kvh in range(A_KV_HEADS)]
    o_ref[...] = _attend([(q_ref[...], keys, vals, None)], sinks)[0].astype(BF16)


def _attn_sample(sink, q, k, v, cache_k, cache_v, o_prompt):
    win = cache_k.shape[1]
    new = lambda b: (S_BLOCK0 + b, 0)
    old = lambda b: (b, 0, 0)
    return pl.pallas_call(
        _attn_sample_body,
        out_shape=jax.ShapeDtypeStruct((N_ROWS, A_Q_W), BF16),
        grid=(DEC_BATCH,),
        in_specs=[
            pl.BlockSpec(memory_space=pltpu.SMEM),
            pl.BlockSpec((DEC_SEQ, A_Q_W), new),
            pl.BlockSpec((DEC_SEQ, A_KV_W), new),
            pl.BlockSpec((DEC_SEQ, A_KV_W), new),
            pl.BlockSpec((None, win, A_KV_W), old),
            pl.BlockSpec((None, win, A_KV_W), old),
            pl.BlockSpec(memory_space=pl.ANY),
        ],
        out_specs=pl.BlockSpec((DEC_SEQ, A_Q_W), new),
        input_output_aliases={6: 0},
        compiler_params=_params("parallel"),
        name="attn_sample",
    )(sink, q, k, v, cache_k, cache_v, o_prompt)


def _group_row(a, group, row):
    spans = a.shape[0] // group
    picked = a.reshape(spans, group, a.shape[1])[:, row:row + 1, :]
    return jnp.broadcast_to(picked, (spans, group, a.shape[1])).reshape(a.shape)


def _hgrn_pairs(q, k, c2):
    parts = []
    for s in range(HGRN_SUB):
        decay = jnp.exp2(jnp.minimum(c2 - _group_row(c2, HGRN_SUB, s), 0.0))
        parts.append((q * _group_row(k, HGRN_SUB, s) * decay).astype(BF16))
    return jnp.concatenate(parts, axis=1)


def _hgrn_pick(rows):
    shape = (HGRN_SUB * B_KEY_DIM, rows)
    block = lax.broadcasted_iota(jnp.int32, shape, 0) // B_KEY_DIM
    col = lax.broadcasted_iota(jnp.int32, shape, 1) & (HGRN_SUB - 1)
    return jnp.where(block == col, 1.0, 0.0).astype(BF16)


def _hgrn_head(q, c, v, state_t, intra, levels, update):
    rows = q.shape[0]
    t_idx = lax.broadcasted_iota(jnp.int32, (rows, rows), 0)
    u_idx = lax.broadcasted_iota(jnp.int32, (rows, rows), 1)
    apart = t_idx ^ u_idx
    in_group = jnp.logical_and(apart < HGRN_SUB, u_idx <= t_idx)
    scores = jnp.where(in_group, intra, 0.0)
    for w, level in levels:
        scores = scores + (level if 2 * w == rows else jnp.where(apart < 2 * w, level, 0.0))
    o = _dot(scores.astype(BF16), v) + _dot_nt((q * jnp.exp2(c)).astype(BF16), state_t.astype(BF16))
    return o, state_t * jnp.exp2(c[rows - 1:rows, :]) + update


def _hgrn_levels(q, k, c):
    rows = q.shape[0]
    pos = lax.broadcasted_iota(jnp.int32, q.shape, 0)
    levels = []
    w = HGRN_SUB
    while w < rows:
        rho = _group_row(c, 2 * w, w - 1)
        upper = (pos & w) != 0
        q_up = jnp.where(upper, q * jnp.exp2(jnp.minimum(c - rho, 0.0)), 0.0).astype(BF16)
        k_low = jnp.where(upper, 0.0, k * jnp.exp2(jnp.minimum(rho - c, 0.0))).astype(BF16)
        levels.append((w, _dot_nt(q_up, k_low)))
        w *= 2
    return levels


def _hgrn_tiles(tiles, n, hq_ref, hk_ref, hc_ref, hv_ref, hg_ref, gain_ref, o_ref, st_ref, pick):
    loaded = [(hq_ref[r, :], hk_ref[r, :], hc_ref[r, :] * LOG2_E, hv_ref[r, :], hg_ref[r, :])
              for r in tiles]
    states = [st_ref[hd] for hd in range(B_HEADS)]
    heads = [slice(hd * B_KEY_DIM, (hd + 1) * B_KEY_DIM) for hd in range(B_HEADS)]
    intras = [_dot(jnp.concatenate([_hgrn_pairs(q[:, h], k[:, h], c[:, h]) for h in heads], axis=0), pick)
              for q, k, c, _, _ in loaded]
    levels = [[_hgrn_levels(q[:, h], k[:, h], c[:, h]) for h in heads] for q, k, c, _, _ in loaded]
    updates = [[_dot_tn(v[:, h], (k[:, h] * jnp.exp2(c[n - 1:n, h] - c[:, h])).astype(BF16))
                for h in heads] for _, k, c, v, _ in loaded]
    results = []
    for t, (q, k, c, v, g) in enumerate(loaded):
        outs = []
        for hd, h in enumerate(heads):
            o, states[hd] = _hgrn_head(q[:, h], c[:, h], v[:, h], states[hd],
                                       intras[t][hd * n:(hd + 1) * n, :], levels[t][hd], updates[t][hd])
            outs.append(_rms(o, gain_ref[hd:hd + 1, :]) * g[:, h])
        results.append(jnp.concatenate(outs, axis=1).astype(BF16))
    for r, res in zip(tiles, results):
        o_ref[r, :] = res
    for hd in range(B_HEADS):
        st_ref[hd] = states[hd]


def _hgrn_prompt_body(*refs, n_cast):
    hq_ref, hk_ref, hc_ref, hv_ref, hg_ref, gain_ref = refs[:6]
    o_ref, s_ref = refs[6 + n_cast:8 + n_cast]
    st_ref = refs[-1]
    _cast_slabs(pl.program_id(0) * pl.num_programs(1) + pl.program_id(1),
                refs[6:6 + n_cast], refs[8 + n_cast:8 + 2 * n_cast])

    @pl.when(pl.program_id(1) == 0)
    def _():
        st_ref[...] = jnp.zeros(st_ref.shape, F32)

    pick = _hgrn_pick(HGRN_ROWS)
    refs = (hq_ref, hk_ref, hc_ref, hv_ref, hg_ref, gain_ref, o_ref, st_ref, pick)
    n_tiles = HGRN_TILE // HGRN_ROWS

    def step(n, carry):
        first = n * HGRN_GROUP
        tiles = [pl.ds(pl.multiple_of((first + i) * HGRN_ROWS, HGRN_ROWS), HGRN_ROWS)
                 for i in range(HGRN_GROUP)]
        _hgrn_tiles(tiles, HGRN_ROWS, *refs)
        return carry

    lax.fori_loop(0, n_tiles // HGRN_GROUP, step, 0)
    rest = [pl.ds(i * HGRN_ROWS, HGRN_ROWS) for i in range(n_tiles - n_tiles % HGRN_GROUP, n_tiles)]
    if rest:
        _hgrn_tiles(rest, HGRN_ROWS, *refs)

    @pl.when(pl.program_id(1) == pl.num_programs(1) - 1)
    def _():
        for hd in range(B_HEADS):
            s_ref[hd] = st_ref[hd].T


def _hgrn_prompt(hq, hk, hb, hv, hg, gain, ffn_views, cast_targets):
    tiles = LP // HGRN_TILE
    row = lambda b, i: (b * tiles + i, 0)
    spec = pl.BlockSpec((HGRN_TILE, B_QK_W), row)
    cast_in, cast_out, cast_shapes = _cast_plan(cast_targets, lambda b, i: b * tiles + i)
    out = pl.pallas_call(
        functools.partial(_hgrn_prompt_body, n_cast=len(cast_in)),
        out_shape=(jax.ShapeDtypeStruct((N_ROWS, B_V_W), BF16),
                   jax.ShapeDtypeStruct((BATCH, B_HEADS, B_KEY_DIM, B_VAL_DIM), F32), *cast_shapes),
        grid=(BATCH, tiles),
        in_specs=[spec, spec, spec, spec, spec,
                  pl.BlockSpec((B_HEADS, B_VAL_DIM), lambda b, i: (0, 0)), *cast_in],
        out_specs=(spec, pl.BlockSpec((None, B_HEADS, B_KEY_DIM, B_VAL_DIM),
                                      lambda b, i: (b, 0, 0, 0)), *cast_out),
        scratch_shapes=[pltpu.VMEM((B_HEADS, B_VAL_DIM, B_KEY_DIM), F32)],
        compiler_params=_params("arbitrary", "arbitrary"),
        name="hgrn_prompt",
    )(hq, hk, hb, hv, hg, gain, *(ffn_views * len(cast_targets)))
    return out[0], out[1], out[2:]


def _hgrn_sample_body(hq_ref, hk_ref, hb_ref, hv_ref, hg_ref, gain_ref, s0_ref, o_in_ref,
                      o_ref, s_ref, st_ref):
    del o_in_ref
    for hd in range(B_HEADS):
        st_ref[hd] = s0_ref[hd].T
    _hgrn_tiles([pl.ds(0, DEC_SEQ)], DEC_SEQ, hq_ref, hk_ref, hb_ref, hv_ref, hg_ref, gain_ref, o_ref,
                st_ref, _hgrn_pick(DEC_SEQ))
    for hd in range(B_HEADS):
        s_ref[hd] = st_ref[hd].T


def _hgrn_sample(hq, hk, hb, hv, hg, gain, s0, o_prompt):
    row = lambda b: (S_BLOCK0 + b, 0)
    spec = pl.BlockSpec((DEC_SEQ, B_QK_W), row)
    state = pl.BlockSpec((None, B_HEADS, B_KEY_DIM, B_VAL_DIM), lambda b: (b, 0, 0, 0))
    return pl.pallas_call(
        _hgrn_sample_body,
        out_shape=(jax.ShapeDtypeStruct((N_ROWS, B_V_W), BF16),
                   jax.ShapeDtypeStruct((DEC_BATCH, B_HEADS, B_KEY_DIM, B_VAL_DIM), F32)),
        grid=(DEC_BATCH,),
        in_specs=[spec, spec, spec, spec, spec,
                  pl.BlockSpec((B_HEADS, B_VAL_DIM), lambda b: (0, 0)),
                  state, pl.BlockSpec(memory_space=pl.ANY)],
        out_specs=(spec, state),
        scratch_shapes=[pltpu.VMEM((B_HEADS, B_VAL_DIM, B_KEY_DIM), F32)],
        input_output_aliases={7: 0},
        compiler_params=_params("parallel"),
        name="hgrn_sample",
    )(hq, hk, hb, hv, hg, gain, s0, o_prompt)


def _even_out_body(x_ref, g_ref, oa_ref, ob_ref, w_ref, o_ref):
    m = _dot(oa_ref[...], w_ref[:A_Q_W, :]) + _dot(ob_ref[...], w_ref[A_Q_W:, :])
    y = x_ref[...] + _rms(m, g_ref[3:4, :])
    pad = _is_pad_row(pl.program_id(0) * ROW_TILE, ROW_TILE)
    o_ref[...] = jnp.where(pad, 0.0, y)


def _even_out(x, gains, oa, ob, w_out, j):
    row = lambda i: (i, 0)
    return pl.pallas_call(
        _even_out_body,
        out_shape=jax.ShapeDtypeStruct((N_ROWS, D_MODEL), F32),
        grid=(N_ROWS // ROW_TILE,),
        in_specs=[
            pl.BlockSpec((ROW_TILE, D_MODEL), row),
            pl.BlockSpec((6, D_MODEL), lambda i: (0, 0)),
            pl.BlockSpec((ROW_TILE, A_Q_W), row),
            pl.BlockSpec((ROW_TILE, B_V_W), row),
            pl.BlockSpec((None, A_Q_W + B_V_W, D_MODEL), lambda i: (j, 0, 0),
                         pipeline_mode=pl.Buffered(1)),
        ],
        out_specs=pl.BlockSpec((ROW_TILE, D_MODEL), row),
        input_output_aliases={0: 0},
        compiler_params=_params("parallel"),
        name=f"even_out_{j}",
    )(x, gains, oa, ob, w_out)


def _conv3(u, cw_ref, before1=None, before2=None):
    shift1, shift2 = pltpu.roll(u, 1, 0), pltpu.roll(u, 2, 0)
    if before1 is not None:
        pos = lax.broadcasted_iota(jnp.int32, (u.shape[0], 1), 0)
        shift1 = jnp.where(pos == 0, before1, shift1)
        shift2 = jnp.where(pos == 0, before2, jnp.where(pos == 1, before1, shift2))
    return shift2 * cw_ref[0:1, :] + shift1 * cw_ref[1:2, :] + u * cw_ref[2:3, :]


def _odd_prompt_body(*refs, n_cast):
    x_ref, halo_ref, g_ref, w_in_ref, cw_ref, w_out_ref = refs[:6]
    o_ref, tail_ref = refs[6 + n_cast:8 + n_cast]
    _cast_slabs(pl.program_id(0), refs[6:6 + n_cast], refs[8 + n_cast:])
    is_prompt = pl.program_id(0) < P_ROWS // ODD_TILE

    @pl.when(is_prompt)
    def _():
        x = x_ref[...]
        h = _rms(x, g_ref[2:3, :]).astype(BF16)
        h_ext = jnp.concatenate([_rms(halo_ref[...], g_ref[2:3, :]).astype(BF16), h], axis=0)
        bg = _dot(h, w_in_ref[:, :D_MODEL])
        u_ext = _dot(h_ext, w_in_ref[:, D_MODEL:2 * D_MODEL]) * _dot(h_ext, w_in_ref[:, 2 * D_MODEL:])
        tail_ref[...] = u_ext[ODD_HALO + ODD_TILE - TAIL_ROWS:, :]
        y = _conv3(u_ext, cw_ref)[ODD_HALO:, :]
        m = _dot((bg * y).astype(BF16), w_out_ref[...])
        out = x + _rms(m, g_ref[3:4, :])
        o_ref[...] = jnp.where(_is_pad_row(pl.program_id(0) * ODD_TILE, ODD_TILE), 0.0, out)

    @pl.when(jnp.logical_not(is_prompt))
    def _():
        o_ref[...] = x_ref[...]
        tail_ref[...] = jnp.zeros(tail_ref.shape, F32)


def _odd_prompt(x, gains, w_in, conv_w, w_out, j, ffn_views, cast_targets):
    tiles = pl.cdiv(N_ROWS, ODD_TILE)
    row = lambda i: (i, 0)
    per = ODD_TILE // ODD_HALO
    halo = lambda i: (jnp.maximum(i * per - 1, 0), 0)
    cast_in, cast_out, cast_shapes = _cast_plan(cast_targets, lambda i: i)
    out = pl.pallas_call(
        functools.partial(_odd_prompt_body, n_cast=len(cast_in)),
        out_shape=(jax.ShapeDtypeStruct((N_ROWS, D_MODEL), F32),
                   jax.ShapeDtypeStruct((tiles, TAIL_ROWS, D_MODEL), F32), *cast_shapes),
        grid=(tiles,),
        in_specs=[
            pl.BlockSpec((ODD_TILE, D_MODEL), row),
            pl.BlockSpec((ODD_HALO, D_MODEL), halo),
            pl.BlockSpec((6, D_MODEL), lambda i: (0, 0)),
            pl.BlockSpec((None, D_MODEL, 3 * D_MODEL), lambda i: (j, 0, 0), pipeline_mode=pl.Buffered(1)),
            pl.BlockSpec((None, CONV_WIDTH, D_MODEL), lambda i: (j, 0, 0)),
            pl.BlockSpec((None, D_MODEL, D_MODEL), lambda i: (j, 0, 0), pipeline_mode=pl.Buffered(1)),
            *cast_in,
        ],
        out_specs=(pl.BlockSpec((ODD_TILE, D_MODEL), row),
                   pl.BlockSpec((None, TAIL_ROWS, D_MODEL), lambda i: (i, 0, 0)), *cast_out),
        compiler_params=_params("arbitrary"),
        name=f"odd_prompt_{j}",
    )(x, x, gains, w_in, conv_w, w_out, *(ffn_views * len(cast_targets)))
    return out[0], out[1], out[2:]


def _odd_sample_body(x_ref, g_ref, w_in_ref, cw_ref, w_out_ref, cache_ref, o_ref, tail_ref):
    x = x_ref[...]
    h = _rms(x, g_ref[2:3, :]).astype(BF16)
    bg = _dot(h, w_in_ref[:, :D_MODEL])
    u = _dot(h, w_in_ref[:, D_MODEL:2 * D_MODEL]) * _dot(h, w_in_ref[:, 2 * D_MODEL:])
    ys = []
    for b in range(DEC_BATCH):
        ub = u[b * DEC_SEQ:(b + 1) * DEC_SEQ, :]
        ys.append(_conv3(ub, cw_ref, before1=cache_ref[b, 1:2, :], before2=cache_ref[b, 0:1, :]))
        tail_ref[b] = ub[DEC_SEQ - (CONV_WIDTH - 1):, :]
    m = _dot((bg * jnp.concatenate(ys, axis=0)).astype(BF16), w_out_ref[...])
    o_ref[...] = x + _rms(m, g_ref[3:4, :])


def _odd_sample(x, gains, w_in, conv_w, w_out, cache, j):
    rows = lambda i: (P_ROWS // S_ROWS, 0)
    whole3 = lambda i: (0, 0, 0)
    return pl.pallas_call(
        _odd_sample_body,
        out_shape=(jax.ShapeDtypeStruct((N_ROWS, D_MODEL), F32),
                   jax.ShapeDtypeStruct((DEC_BATCH, CONV_WIDTH - 1, D_MODEL), F32)),
        grid=(1,),
        in_specs=[
            pl.BlockSpec((S_ROWS, D_MODEL), rows),
            pl.BlockSpec((6, D_MODEL), lambda i: (0, 0)),
            pl.BlockSpec((None, D_MODEL, 3 * D_MODEL), lambda i: (j, 0, 0), pipeline_mode=pl.Buffered(1)),
            pl.BlockSpec((None, CONV_WIDTH, D_MODEL), lambda i: (j, 0, 0)),
            pl.BlockSpec((None, D_MODEL, D_MODEL), lambda i: (j, 0, 0), pipeline_mode=pl.Buffered(1)),
            pl.BlockSpec((DEC_BATCH, CONV_WIDTH - 1, D_MODEL), whole3),
        ],
        out_specs=(pl.BlockSpec((S_ROWS, D_MODEL), rows),
                   pl.BlockSpec((DEC_BATCH, CONV_WIDTH - 1, D_MODEL), whole3)),
        input_output_aliases={0: 0},
        compiler_params=_params("arbitrary"),
        name=f"odd_sample_{j}",
    )(x, gains, w_in, conv_w, w_out, cache)


def _stream_tails(a, row0, streams, length, n):
    ends = [row0 + (s + 1) * length for s in range(streams)]
    return jnp.stack([lax.slice_in_dim(a, e - n, e, axis=0) for e in ends])


def kernel(x_prompt, x_sample, cache_swa_k, cache_swa_v, state_hgrn, cache_conv, meta_tokens, norm_gains,
           w_ffn_gate, w_ffn_up, w_ffn_down, w_in_even, w_out_even, attn_sinks, hgrn_lb_logits,
           hgrn_norm_gain, w_in_odd, conv_w, w_out_odd):
    win = cache_swa_k.shape[2]
    lead = jnp.concatenate([jnp.zeros((PAD, D_MODEL), F32), meta_tokens.astype(F32)], axis=0)
    pieces = [p for b in range(BATCH) for p in (lead, x_prompt[b])]
    x = jnp.concatenate(pieces + [x_sample.reshape(S_ROWS, D_MODEL)], axis=0)

    ffn_views = (w_ffn_gate.reshape(DEPTH, 2, D_FF, D_MODEL), w_ffn_up.reshape(DEPTH, 2, D_FF, D_MODEL),
                 w_ffn_down)
    ffn_w = {(0, 0): (w_ffn_gate[0, 0].astype(BF16), w_ffn_up[0, 0].astype(BF16),
                      w_ffn_down[0, 0].astype(BF16))}

    def keep_converted(targets, converted):
        for n, key in enumerate(targets):
            gate, up, down = converted[3 * n:3 * n + 3]
            ffn_w[key] = (gate.reshape(D_MODEL, D_FF), up.reshape(D_MODEL, D_FF), down)

    w_in_e, w_out_e = w_in_even.astype(BF16), w_out_even.astype(BF16)
    w_in_o, w_out_o = w_in_odd.astype(BF16), w_out_odd.astype(BF16)
    cache_k = cache_swa_k.reshape(N_EVEN, DEC_BATCH, win, A_KV_W)
    cache_v = cache_swa_v.reshape(N_EVEN, DEC_BATCH, win, A_KV_W)

    k_p, v_p, s_p, c_p, k_s, v_s, s_s, c_s = ([] for _ in range(8))
    for layer in range(DEPTH):
        gains = norm_gains[layer]
        j = layer // 2
        x = _ffn(x, norm_gains, ffn_w[(layer, 0)], layer, 0)
        targets = [(layer, 1)] + ([(layer + 1, 0)] if layer + 1 < DEPTH else [])
        if layer % 2 == 0:
            q, k, v, kk, vv, hq, hk, hb, hv, hg = _even_in(x, gains, w_in_e, hgrn_lb_logits, j)
            oa = _attn_prompt(attn_sinks[j], q, kk, vv)
            oa = _attn_sample(attn_sinks[j], q, k, v, cache_k[j], cache_v[j], oa)
            ob, st_p, converted = _hgrn_prompt(hq, hk, hb, hv, hg, hgrn_norm_gain[j], ffn_views, targets)
            keep_converted(targets, converted)
            ob, st_s = _hgrn_sample(hq, hk, hb, hv, hg, hgrn_norm_gain[j], state_hgrn[j], ob)
            x = _even_out(x, gains, oa, ob, w_out_e, j)
            for new, tail_p, tail_s, cache in ((k, k_p, k_s, cache_k[j]), (v, v_p, v_s, cache_v[j])):
                tail_p.append(_stream_tails(new, 0, BATCH, LP, WINDOW).reshape(
                    BATCH, WINDOW, A_KV_HEADS, A_HEAD_DIM))
                both = jnp.concatenate([cache, new[P_ROWS:].reshape(DEC_BATCH, DEC_SEQ, A_KV_W)], axis=1)
                tail_s.append(both[:, both.shape[1] - win:].reshape(DEC_BATCH, win, A_KV_HEADS, A_HEAD_DIM))
            s_p.append(st_p)
            s_s.append(st_s)
        else:
            x, conv_s = _odd_sample(x, gains, w_in_o, conv_w, w_out_o, cache_conv[j], j)
            x, tails, converted = _odd_prompt(x, gains, w_in_o, conv_w, w_out_o, j, ffn_views, targets)
            keep_converted(targets, converted)
            per_stream = LP // ODD_TILE
            c_p.append(jnp.stack([tails[(b + 1) * per_stream - 1, TAIL_ROWS - (CONV_WIDTH - 1):]
                                  for b in range(BATCH)]))
            c_s.append(conv_s)
        x = _ffn(x, norm_gains, ffn_w[(layer, 1)], layer, 1)

    y_prompt = _stream_tails(x, 0, BATCH, LP, SEQ)
    y_sample = x[P_ROWS:].reshape(DEC_BATCH, DEC_SEQ, D_MODEL)
    return (y_prompt, y_sample, jnp.stack(k_p), jnp.stack(v_p), jnp.stack(s_p), jnp.stack(c_p),
            jnp.stack(k_s), jnp.stack(v_s), jnp.stack(s_s), jnp.stack(c_s))
```

```python
import functools

import jax
import jax.numpy as jnp
from jax import lax
from jax.experimental import pallas as pl
from jax.experimental.pallas import tpu as pltpu

F32 = jnp.float32
BF16 = jnp.bfloat16

D_MODEL = 1024
BATCH = 4
SEQ = 4096
DEPTH = 4
DEC_BATCH = 8
DEC_SEQ = 32
CHUNK = 64
N_META = 16
WINDOW = 128
WINDOW_CHUNKS = WINDOW // CHUNK
A_HEADS = 8
A_KV_HEADS = 2
A_HEAD_DIM = 64
A_Q_W = A_HEADS * A_HEAD_DIM
A_KV_W = A_KV_HEADS * A_HEAD_DIM
B_HEADS = 4
B_KEY_DIM = 128
B_VAL_DIM = 128
B_QK_W = B_HEADS * B_KEY_DIM
B_V_W = B_HEADS * B_VAL_DIM
B_BLOCK = 16
CONV_WIDTH = 3
D_FF = 2816
FFN_RESIDUAL = 0.5
N_EVEN = (DEPTH + 1) // 2
N_ODD = DEPTH // 2
EVEN_IN_W = A_Q_W + 2 * A_KV_W + 2 * B_QK_W + 2 * B_V_W
EPS = 1e-6
MASK_VALUE = -1e30
LB_FLOOR = 1e-30

PAD = (-N_META) % CHUNK
LP = PAD + N_META + SEQ
N_CHUNKS = LP // CHUNK
P_ROWS = BATCH * LP
S_ROWS = DEC_BATCH * DEC_SEQ
N_ROWS = P_ROWS + S_ROWS
S_BLOCK0 = P_ROWS // DEC_SEQ

ROW_TILE = 512
FF_CHUNK = 256
HGRN_TILE = 832
HGRN_ROWS = 64
HGRN_SUB = 8
HGRN_GROUP = 2
LOG2_E = 1.4426950408889634
CUMSUM_ROWS = 256
FFN_TILE = 1536
FFN_SUB = 512
ATTN_CHUNKS = 5
CAST_SLAB = 256
CAST_STEPS = D_FF // CAST_SLAB
ODD_TILE = 832
ODD_HALO = 16
TAIL_ROWS = 8
VMEM_LIMIT = 56 * 1024 * 1024

assert N_ROWS % ROW_TILE == 0 and D_FF % FF_CHUNK == 0
assert LP % HGRN_TILE == 0 and HGRN_TILE % HGRN_ROWS == 0 and ROW_TILE % HGRN_ROWS == 0
assert HGRN_ROWS % DEC_SEQ == 0 and DEC_SEQ % HGRN_SUB == 0
assert ROW_TILE % CUMSUM_ROWS == 0 and CUMSUM_ROWS % HGRN_ROWS == 0
assert N_ROWS % FFN_TILE == 0 and FFN_TILE % FFN_SUB == 0 and N_CHUNKS % ATTN_CHUNKS == 0
assert D_FF % CAST_SLAB == 0 and CAST_STEPS <= min(P_ROWS // ODD_TILE, P_ROWS // HGRN_TILE)
assert LP % ODD_TILE == 0 and ODD_TILE % ODD_HALO == 0 and P_ROWS % DEC_SEQ == 0
assert P_ROWS % S_ROWS == 0 and S_ROWS <= ODD_TILE and TAIL_ROWS >= CONV_WIDTH - 1


def _params(*sem):
    return pltpu.CompilerParams(dimension_semantics=sem, vmem_limit_bytes=VMEM_LIMIT)


def _rms(x, gain):
    return x * lax.rsqrt(jnp.mean(x * x, axis=-1, keepdims=True) + EPS) * gain


def _silu(x):
    return x * jax.nn.sigmoid(x)


def _dot(a, b):
    return jnp.dot(a, b, preferred_element_type=F32)


def _dot_nt(a, b):
    return lax.dot_general(a, b, (((1,), (1,)), ((), ())), preferred_element_type=F32)


def _dot_tn(a, b):
    return lax.dot_general(a, b, (((0,), (0,)), ((), ())), preferred_element_type=F32)


def _is_pad_row(row0, rows):
    r = row0 + lax.broadcasted_iota(jnp.int32, (rows, 1), 0)
    pad = None
    for b in range(BATCH):
        hit = jnp.logical_and(r >= b * LP, r < b * LP + PAD)
        pad = hit if pad is None else jnp.logical_or(pad, hit)
    return pad


def _ffn_body(sel_ref, x_ref, g_ref, wg_ref, wu_ref, wd_ref, o_ref):
    pre = sel_ref[1] * 4
    post = pre + 1
    pieces = [slice(lo, lo + FFN_SUB) for lo in range(0, FFN_TILE, FFN_SUB)]
    chunks = [slice(c * FF_CHUNK, (c + 1) * FF_CHUNK) for c in range(D_FF // FF_CHUNK)]
    xs = [x_ref[rows, :] for rows in pieces]
    hs = [_rms(x, g_ref[pl.ds(pre, 1), :]).astype(BF16) for x in xs]
    accs = []
    for h in hs:
        acc = jnp.zeros((FFN_SUB, D_MODEL), F32)
        for cols in chunks:
            act = (_silu(_dot(h, wg_ref[:, cols])) * _dot(h, wu_ref[:, cols])).astype(BF16)
            acc = acc + _dot(act, wd_ref[cols, :])
        accs.append(acc)
    for rows, x, acc in zip(pieces, xs, accs):
        o_ref[rows, :] = x + FFN_RESIDUAL * _rms(acc, g_ref[pl.ds(post, 1), :])


def _ffn(x, norm_gains, weights, layer, which):
    row = lambda i, sel: (i, 0)
    whole = lambda i, sel: (0, 0)
    resident = pl.Buffered(1)
    grid_spec = pltpu.PrefetchScalarGridSpec(
        num_scalar_prefetch=1,
        grid=(N_ROWS // FFN_TILE,),
        in_specs=[
            pl.BlockSpec((FFN_TILE, D_MODEL), row),
            pl.BlockSpec((None, 6, D_MODEL), lambda i, sel: (sel[0], 0, 0)),
            pl.BlockSpec((D_MODEL, D_FF), whole, pipeline_mode=resident),
            pl.BlockSpec((D_MODEL, D_FF), whole, pipeline_mode=resident),
            pl.BlockSpec((D_FF, D_MODEL), whole, pipeline_mode=resident),
        ],
        out_specs=pl.BlockSpec((FFN_TILE, D_MODEL), row),
    )
    return pl.pallas_call(
        _ffn_body,
        out_shape=jax.ShapeDtypeStruct((N_ROWS, D_MODEL), F32),
        grid_spec=grid_spec,
        input_output_aliases={1: 0},
        compiler_params=_params("parallel"),
        name="ffn",
    )(jnp.array([layer, which], jnp.int32), x, norm_gains, *weights)


def _cast_plan(targets, step_of):
    slab = lambda *ids: jnp.minimum(step_of(*ids), CAST_STEPS - 1)
    in_specs, out_specs, shapes = [], [], []
    for layer, which in targets:
        for block, shape, place in (((D_MODEL, CAST_SLAB), (D_MODEL, D_FF), lambda s: (0, s)),
                                    ((D_MODEL, CAST_SLAB), (D_MODEL, D_FF), lambda s: (0, s)),
                                    ((CAST_SLAB, D_MODEL), (D_FF, D_MODEL), lambda s: (s, 0))):
            in_specs.append(pl.BlockSpec((None, None) + block,
                                         lambda *ids, l=layer, w=which, p=place: (l, w) + p(slab(*ids))))
            out_specs.append(pl.BlockSpec(block, lambda *ids, p=place: p(slab(*ids))))
            shapes.append(jax.ShapeDtypeStruct(shape, BF16))
    return in_specs, out_specs, shapes


def _cast_slabs(step, srcs, dsts):
    @pl.when(step < CAST_STEPS)
    def _():
        for src, dst in zip(srcs, dsts):
            dst[...] = src[...].astype(BF16)


def _lower_bound(lbl_ref, j):
    rows = [lbl_ref[i:i + 1, :] for i in range(N_EVEN)]
    top = functools.reduce(jnp.maximum, rows)
    e = [jnp.exp(r - top) for r in rows]
    total = functools.reduce(lambda a, b: a + b, e)
    p = [v / total for v in e]
    cum = functools.reduce(lambda a, b: a + b, p[:j + 1])
    return jnp.maximum(cum - p[0], 0.0)


def _span_cumsum(g, row0):
    out = []
    for lo in range(0, g.shape[0], CUMSUM_ROWS):
        part = g[lo:lo + CUMSUM_ROWS, :]
        t = row0 + lo + lax.broadcasted_iota(jnp.int32, (CUMSUM_ROWS, CUMSUM_ROWS), 0)
        u = row0 + lo + lax.broadcasted_iota(jnp.int32, (CUMSUM_ROWS, CUMSUM_ROWS), 1)
        span = jnp.where(t >= P_ROWS, DEC_SEQ, HGRN_ROWS)
        tri = jnp.where(jnp.logical_and((t ^ u) < span, u <= t), 1.0, 0.0).astype(BF16)
        total = None
        for _ in range(3):
            term = part.astype(BF16)
            part = part - term.astype(F32)
            total = _dot(tri, term) if total is None else total + _dot(tri, term)
        out.append(total)
    return jnp.concatenate(out, axis=0)


def _even_in_body(x_ref, g_ref, w_ref, lbl_ref, q_ref, k_ref, v_ref, kk_ref, vv_ref,
                  hq_ref, hk_ref, hb_ref, hv_ref, hg_ref, *, j):
    h = _rms(x_ref[...], g_ref[2:3, :]).astype(BF16)
    widths = (A_Q_W, 2 * A_KV_W, B_QK_W, B_QK_W, B_V_W, B_V_W)
    starts = [sum(widths[:i]) for i in range(len(widths))]
    proj = lambda i: _dot(h, w_ref[:, starts[i]:starts[i] + widths[i]])
    fx = proj(3)
    q_ref[...] = (proj(0) * (A_HEAD_DIM ** -0.5)).astype(BF16)
    kv = proj(1)
    k_ref[...] = kv[:, :A_KV_W]
    v_ref[...] = kv[:, A_KV_W:]
    kk_ref[...] = jnp.concatenate(_half_lane_variants(kv[:, :A_KV_W]), axis=1)
    vv_ref[...] = jnp.concatenate(_half_lane_variants(kv[:, A_KV_W:]), axis=1)
    hq_ref[...] = _silu(proj(2))
    hv_ref[...] = proj(4).astype(BF16)
    hg_ref[...] = _silu(proj(5))
    lb = _lower_bound(lbl_ref, j)
    soft = jnp.log1p(jnp.exp(-jnp.abs(fx)))
    ls_pos = jnp.minimum(fx, 0.0) - soft
    ls_neg = jnp.minimum(-fx, 0.0) - soft
    other = jnp.log(jnp.maximum(lb, LB_FLOOR)) + ls_neg
    log_f = jnp.maximum(ls_pos, other) + jnp.log1p(jnp.exp(-jnp.abs(ls_pos - other)))
    hk_ref[...] = (1.0 - lb) * jax.nn.sigmoid(-fx)
    hb_ref[...] = _span_cumsum(log_f, pl.program_id(0) * ROW_TILE)


def _even_in(x, gains, w_in, lb_logits, j):
    row = lambda i: (i, 0)
    whole = lambda i: (0, 0)
    wide = lambda w, dt: jax.ShapeDtypeStruct((N_ROWS, w), dt)
    spec = lambda w: pl.BlockSpec((ROW_TILE, w), row)
    return pl.pallas_call(
        functools.partial(_even_in_body, j=j),
        out_shape=(wide(A_Q_W, BF16), wide(A_KV_W, F32), wide(A_KV_W, F32),
                   wide(4 * A_KV_W, BF16), wide(4 * A_KV_W, BF16),
                   wide(B_QK_W, F32), wide(B_QK_W, F32), wide(B_QK_W, F32),
                   wide(B_V_W, BF16), wide(B_V_W, F32)),
        grid=(N_ROWS // ROW_TILE,),
        in_specs=[
            pl.BlockSpec((ROW_TILE, D_MODEL), row),
            pl.BlockSpec((6, D_MODEL), whole),
            pl.BlockSpec((None, D_MODEL, EVEN_IN_W), lambda i: (j, 0, 0),
                         pipeline_mode=pl.Buffered(1)),
            pl.BlockSpec((N_EVEN, B_QK_W), whole),
        ],
        out_specs=(spec(A_Q_W), spec(A_KV_W), spec(A_KV_W), spec(4 * A_KV_W), spec(4 * A_KV_W),
                   spec(B_QK_W), spec(B_QK_W), spec(B_QK_W), spec(B_V_W), spec(B_V_W)),
        compiler_params=_params("parallel"),
        name=f"even_in_{j}",
    )(x, gains, w_in, lb_logits)


def _half_lane_variants(a):
    lane = lax.broadcasted_iota(jnp.int32, a.shape, 1)
    low = lane < A_HEAD_DIM
    swapped = pltpu.roll(a, A_HEAD_DIM, 1)
    zero = jnp.zeros_like(a)
    return (jnp.where(low, a, zero).astype(BF16), jnp.where(low, zero, swapped).astype(BF16),
            jnp.where(low, swapped, zero).astype(BF16), jnp.where(low, zero, a).astype(BF16))


def _sink_softmax(s, valid, sink):
    if valid is not None:
        s = jnp.where(valid, s, MASK_VALUE)
    m = jnp.maximum(jnp.max(s, axis=-1, keepdims=True), sink)
    p = jnp.exp(s - m)
    denom = jnp.sum(p, axis=-1, keepdims=True) + jnp.exp(sink - m)
    return (p / denom).astype(BF16)


def _group_sinks(sink_ref, kvh, rows):
    first = lax.broadcasted_iota(jnp.int32, (2 * rows, 1), 0) < rows
    base = kvh * (A_HEADS // A_KV_HEADS)
    return (jnp.where(first, sink_ref[base], sink_ref[base + 2]),
            jnp.where(first, sink_ref[base + 1], sink_ref[base + 3]))


def _attend(items, sinks):
    scores = []
    for q, keys, _, _ in items:
        for kvh in range(A_KV_HEADS):
            c0 = kvh * 256
            q2 = jnp.concatenate([q[:, c0:c0 + 128], q[:, c0 + 128:c0 + 256]], axis=0)
            scores += [_dot_nt(q2, keys[2 * kvh]), _dot_nt(q2, keys[2 * kvh + 1])]
    weights = []
    for n, (_, _, _, valid) in enumerate(items):
        for kvh in range(A_KV_HEADS):
            for half in range(2):
                weights.append(_sink_softmax(scores[4 * n + 2 * kvh + half], valid, sinks[kvh][half]))
    outs = []
    for n, (q, _, vals, _) in enumerate(items):
        rows = q.shape[0]
        groups = []
        for kvh in range(A_KV_HEADS):
            w = weights[4 * n + 2 * kvh:4 * n + 2 * kvh + 2]
            o = _dot(w[0], vals[2 * kvh]) + _dot(w[1], vals[2 * kvh + 1])
            groups.append(jnp.concatenate([o[:rows], o[rows:]], axis=1))
        outs.append(jnp.concatenate(groups, axis=1))
    return outs


def _attn_prompt_body(sink_ref, q_ref, kk_ref, vv_ref, o_ref):
    span = (WINDOW_CHUNKS + 1) * CHUNK
    variant = [slice(v * A_KV_W, (v + 1) * A_KV_W) for v in range(4)]
    sinks = [_group_sinks(sink_ref, kvh, CHUNK) for kvh in range(A_KV_HEADS)]

    def step(n, carry):
        starts, items = [], []
        for i in range(ATTN_CHUNKS):
            c = n * ATTN_CHUNKS + i
            r0 = pl.multiple_of(c * CHUNK, CHUNK)
            s0 = pl.multiple_of(jnp.maximum(c - WINDOW_CHUNKS, 0) * CHUNK, CHUNK)
            kpos = s0 + lax.broadcasted_iota(jnp.int32, (1, span), 1)
            valid = jnp.logical_and(kpos >= PAD, kpos < r0 + CHUNK)
            starts.append(r0)
            items.append((q_ref[pl.ds(r0, CHUNK), :],
                          [kk_ref[pl.ds(s0, span), v] for v in variant],
                          [vv_ref[pl.ds(s0, span), v] for v in variant], valid))
        for r0, out in zip(starts, _attend(items, sinks)):
            o_ref[pl.ds(r0, CHUNK), :] = out.astype(BF16)
        return carry

    lax.fori_loop(0, N_CHUNKS // ATTN_CHUNKS, step, 0)


def _attn_prompt(sink, q, kk, vv):
    seq = lambda b: (b, 0)
    return pl.pallas_call(
        _attn_prompt_body,
        out_shape=jax.ShapeDtypeStruct((N_ROWS, A_Q_W), BF16),
        grid=(BATCH,),
        in_specs=[
            pl.BlockSpec(memory_space=pltpu.SMEM),
            pl.BlockSpec((LP, A_Q_W), seq),
            pl.BlockSpec((LP, 4 * A_KV_W), seq),
            pl.BlockSpec((LP, 4 * A_KV_W), seq),
        ],
        out_specs=pl.BlockSpec((LP, A_Q_W), seq),
        compiler_params=_params("parallel"),
        name="attn_prompt",
    )(sink, q, kk, vv)


def _attn_sample_body(sink_ref, q_ref, k_ref, v_ref, ck_ref, cv_ref, o_in_ref, o_ref):
    del o_in_ref
    keys = _half_lane_variants(jnp.concatenate([ck_ref[...], k_ref[...]], axis=0))
    vals = _half_lane_variants(jnp.concatenate([cv_ref[...], v_ref[...]], axis=0))
    sinks = [_group_sinks(sink_ref, kvh, DEC_SEQ) for kvh in range(A_KV_HEADS)]
    o_ref[...] = _attend([(q_ref[...], keys, vals, None)], sinks)[0].astype(BF16)


def _attn_sample(sink, q, k, v, cache_k, cache_v, o_prompt):
    win = cache_k.shape[1]
    new = lambda b: (S_BLOCK0 + b, 0)
    old = lambda b: (b, 0, 0)
    return pl.pallas_call(
        _attn_sample_body,
        out_shape=jax.ShapeDtypeStruct((N_ROWS, A_Q_W), BF16),
        grid=(DEC_BATCH,),
        in_specs=[
            pl.BlockSpec(memory_space=pltpu.SMEM),
            pl.BlockSpec((DEC_SEQ, A_Q_W), new),
            pl.BlockSpec((DEC_SEQ, A_KV_W), new),
            pl.BlockSpec((DEC_SEQ, A_KV_W), new),
            pl.BlockSpec((None, win, A_KV_W), old),
            pl.BlockSpec((None, win, A_KV_W), old),
            pl.BlockSpec(memory_space=pl.ANY),
        ],
        out_specs=pl.BlockSpec((DEC_SEQ, A_Q_W), new),
        input_output_aliases={6: 0},
        compiler_params=_params("parallel"),
        name="attn_sample",
    )(sink, q, k, v, cache_k, cache_v, o_prompt)


def _group_row(a, group, row):
    spans = a.shape[0] // group
    picked = a.reshape(spans, group, a.shape[1])[:, row:row + 1, :]
    return jnp.broadcast_to(picked, (spans, group, a.shape[1])).reshape(a.shape)


def _hgrn_pairs(q, k, c2):
    parts = []
    for s in range(HGRN_SUB):
        decay = jnp.exp2(jnp.minimum(c2 - _group_row(c2, HGRN_SUB, s), 0.0))
        parts.append((q * _group_row(k, HGRN_SUB, s) * decay).astype(BF16))
    return jnp.concatenate(parts, axis=1)


def _hgrn_pick(rows):
    shape = (HGRN_SUB * B_KEY_DIM, rows)
    block = lax.broadcasted_iota(jnp.int32, shape, 0) // B_KEY_DIM
    col = lax.broadcasted_iota(jnp.int32, shape, 1) & (HGRN_SUB - 1)
    return jnp.where(block == col, 1.0, 0.0).astype(BF16)


def _hgrn_head(q, c, v, state_t, intra, levels, update):
    rows = q.shape[0]
    t_idx = lax.broadcasted_iota(jnp.int32, (rows, rows), 0)
    u_idx = lax.broadcasted_iota(jnp.int32, (rows, rows), 1)
    apart = t_idx ^ u_idx
    in_group = jnp.logical_and(apart < HGRN_SUB, u_idx <= t_idx)
    scores = jnp.where(in_group, intra, 0.0)
    for w, level in levels:
        scores = scores + (level if 2 * w == rows else jnp.where(apart < 2 * w, level, 0.0))
    o = _dot(scores.astype(BF16), v) + _dot_nt((q * jnp.exp2(c)).astype(BF16), state_t.astype(BF16))
    return o, state_t * jnp.exp2(c[rows - 1:rows, :]) + update


def _hgrn_levels(q, k, c):
    rows = q.shape[0]
    pos = lax.broadcasted_iota(jnp.int32, q.shape, 0)
    levels = []
    w = HGRN_SUB
    while w < rows:
        rho = _group_row(c, 2 * w, w - 1)
        upper = (pos & w) != 0
        q_up = jnp.where(upper, q * jnp.exp2(jnp.minimum(c - rho, 0.0)), 0.0).astype(BF16)
        k_low = jnp.where(upper, 0.0, k * jnp.exp2(jnp.minimum(rho - c, 0.0))).astype(BF16)
        levels.append((w, _dot_nt(q_up, k_low)))
        w *= 2
    return levels


def _hgrn_tiles(tiles, n, hq_ref, hk_ref, hc_ref, hv_ref, hg_ref, gain_ref, o_ref, st_ref, pick):
    loaded = [(hq_ref[r, :], hk_ref[r, :], hc_ref[r, :] * LOG2_E, hv_ref[r, :], hg_ref[r, :])
              for r in tiles]
    states = [st_ref[hd] for hd in range(B_HEADS)]
    heads = [slice(hd * B_KEY_DIM, (hd + 1) * B_KEY_DIM) for hd in range(B_HEADS)]
    intras = [_dot(jnp.concatenate([_hgrn_pairs(q[:, h], k[:, h], c[:, h]) for h in heads], axis=0), pick)
              for q, k, c, _, _ in loaded]
    levels = [[_hgrn_levels(q[:, h], k[:, h], c[:, h]) for h in heads] for q, k, c, _, _ in loaded]
    updates = [[_dot_tn(v[:, h], (k[:, h] * jnp.exp2(c[n - 1:n, h] - c[:, h])).astype(BF16))
                for h in heads] for _, k, c, v, _ in loaded]
    results = []
    for t, (q, k, c, v, g) in enumerate(loaded):
        outs = []
        for hd, h in enumerate(heads):
            o, states[hd] = _hgrn_head(q[:, h], c[:, h], v[:, h], states[hd],
                                       intras[t][hd * n:(hd + 1) * n, :], levels[t][hd], updates[t][hd])
            outs.append(_rms(o, gain_ref[hd:hd + 1, :]) * g[:, h])
        results.append(jnp.concatenate(outs, axis=1).astype(BF16))
    for r, res in zip(tiles, results):
        o_ref[r, :] = res
    for hd in range(B_HEADS):
        st_ref[hd] = states[hd]


def _hgrn_prompt_body(*refs, n_cast):
    hq_ref, hk_ref, hc_ref, hv_ref, hg_ref, gain_ref = refs[:6]
    o_ref, s_ref = refs[6 + n_cast:8 + n_cast]
    st_ref = refs[-1]
    _cast_slabs(pl.program_id(0) * pl.num_programs(1) + pl.program_id(1),
                refs[6:6 + n_cast], refs[8 + n_cast:8 + 2 * n_cast])

    @pl.when(pl.program_id(1) == 0)
    def _():
        st_ref[...] = jnp.zeros(st_ref.shape, F32)

    pick = _hgrn_pick(HGRN_ROWS)
    refs = (hq_ref, hk_ref, hc_ref, hv_ref, hg_ref, gain_ref, o_ref, st_ref, pick)
    n_tiles = HGRN_TILE // HGRN_ROWS

    def step(n, carry):
        first = n * HGRN_GROUP
        tiles = [pl.ds(pl.multiple_of((first + i) * HGRN_ROWS, HGRN_ROWS), HGRN_ROWS)
                 for i in range(HGRN_GROUP)]
        _hgrn_tiles(tiles, HGRN_ROWS, *refs)
        return carry

    lax.fori_loop(0, n_tiles // HGRN_GROUP, step, 0)
    rest = [pl.ds(i * HGRN_ROWS, HGRN_ROWS) for i in range(n_tiles - n_tiles % HGRN_GROUP, n_tiles)]
    if rest:
        _hgrn_tiles(rest, HGRN_ROWS, *refs)

    @pl.when(pl.program_id(1) == pl.num_programs(1) - 1)
    def _():
        for hd in range(B_HEADS):
            s_ref[hd] = st_ref[hd].T


def _hgrn_prompt(hq, hk, hb, hv, hg, gain, ffn_views, cast_targets):
    tiles = LP // HGRN_TILE
    row = lambda b, i: (b * tiles + i, 0)
    spec = pl.BlockSpec((HGRN_TILE, B_QK_W), row)
    cast_in, cast_out, cast_shapes = _cast_plan(cast_targets, lambda b, i: b * tiles + i)
    out = pl.pallas_call(
        functools.partial(_hgrn_prompt_body, n_cast=len(cast_in)),
        out_shape=(jax.ShapeDtypeStruct((N_ROWS, B_V_W), BF16),
                   jax.ShapeDtypeStruct((BATCH, B_HEADS, B_KEY_DIM, B_VAL_DIM), F32), *cast_shapes),
        grid=(BATCH, tiles),
        in_specs=[spec, spec, spec, spec, spec,
                  pl.BlockSpec((B_HEADS, B_VAL_DIM), lambda b, i: (0, 0)), *cast_in],
        out_specs=(spec, pl.BlockSpec((None, B_HEADS, B_KEY_DIM, B_VAL_DIM),
                                      lambda b, i: (b, 0, 0, 0)), *cast_out),
        scratch_shapes=[pltpu.VMEM((B_HEADS, B_VAL_DIM, B_KEY_DIM), F32)],
        compiler_params=_params("arbitrary", "arbitrary"),
        name="hgrn_prompt",
    )(hq, hk, hb, hv, hg, gain, *(ffn_views * len(cast_targets)))
    return out[0], out[1], out[2:]


def _hgrn_sample_body(hq_ref, hk_ref, hb_ref, hv_ref, hg_ref, gain_ref, s0_ref, o_in_ref,
                      o_ref, s_ref, st_ref):
    del o_in_ref
    for hd in range(B_HEADS):
        st_ref[hd] = s0_ref[hd].T
    _hgrn_tiles([pl.ds(0, DEC_SEQ)], DEC_SEQ, hq_ref, hk_ref, hb_ref, hv_ref, hg_ref, gain_ref, o_ref,
                st_ref, _hgrn_pick(DEC_SEQ))
    for hd in range(B_HEADS):
        s_ref[hd] = st_ref[hd].T


def _hgrn_sample(hq, hk, hb, hv, hg, gain, s0, o_prompt):
    row = lambda b: (S_BLOCK0 + b, 0)
    spec = pl.BlockSpec((DEC_SEQ, B_QK_W), row)
    state = pl.BlockSpec((None, B_HEADS, B_KEY_DIM, B_VAL_DIM), lambda b: (b, 0, 0, 0))
    return pl.pallas_call(
        _hgrn_sample_body,
        out_shape=(jax.ShapeDtypeStruct((N_ROWS, B_V_W), BF16),
                   jax.ShapeDtypeStruct((DEC_BATCH, B_HEADS, B_KEY_DIM, B_VAL_DIM), F32)),
        grid=(DEC_BATCH,),
        in_specs=[spec, spec, spec, spec, spec,
                  pl.BlockSpec((B_HEADS, B_VAL_DIM), lambda b: (0, 0)),
                  state, pl.BlockSpec(memory_space=pl.ANY)],
        out_specs=(spec, state),
        scratch_shapes=[pltpu.VMEM((B_HEADS, B_VAL_DIM, B_KEY_DIM), F32)],
        input_output_aliases={7: 0},
        compiler_params=_params("parallel"),
        name="hgrn_sample",
    )(hq, hk, hb, hv, hg, gain, s0, o_prompt)


def _even_out_body(x_ref, g_ref, oa_ref, ob_ref, w_ref, o_ref):
    m = _dot(oa_ref[...], w_ref[:A_Q_W, :]) + _dot(ob_ref[...], w_ref[A_Q_W:, :])
    y = x_ref[...] + _rms(m, g_ref[3:4, :])
    pad = _is_pad_row(pl.program_id(0) * ROW_TILE, ROW_TILE)
    o_ref[...] = jnp.where(pad, 0.0, y)


def _even_out(x, gains, oa, ob, w_out, j):
    row = lambda i: (i, 0)
    return pl.pallas_call(
        _even_out_body,
        out_shape=jax.ShapeDtypeStruct((N_ROWS, D_MODEL), F32),
        grid=(N_ROWS // ROW_TILE,),
        in_specs=[
            pl.BlockSpec((ROW_TILE, D_MODEL), row),
            pl.BlockSpec((6, D_MODEL), lambda i: (0, 0)),
            pl.BlockSpec((ROW_TILE, A_Q_W), row),
            pl.BlockSpec((ROW_TILE, B_V_W), row),
            pl.BlockSpec((None, A_Q_W + B_V_W, D_MODEL), lambda i: (j, 0, 0),
                         pipeline_mode=pl.Buffered(1)),
        ],
        out_specs=pl.BlockSpec((ROW_TILE, D_MODEL), row),
        input_output_aliases={0: 0},
        compiler_params=_params("parallel"),
        name=f"even_out_{j}",
    )(x, gains, oa, ob, w_out)


def _conv3(u, cw_ref, before1=None, before2=None):
    shift1, shift2 = pltpu.roll(u, 1, 0), pltpu.roll(u, 2, 0)
    if before1 is not None:
        pos = lax.broadcasted_iota(jnp.int32, (u.shape[0], 1), 0)
        shift1 = jnp.where(pos == 0, before1, shift1)
        shift2 = jnp.where(pos == 0, before2, jnp.where(pos == 1, before1, shift2))
    return shift2 * cw_ref[0:1, :] + shift1 * cw_ref[1:2, :] + u * cw_ref[2:3, :]


def _odd_prompt_body(*refs, n_cast):
    x_ref, halo_ref, g_ref, w_in_ref, cw_ref, w_out_ref = refs[:6]
    o_ref, tail_ref = refs[6 + n_cast:8 + n_cast]
    _cast_slabs(pl.program_id(0), refs[6:6 + n_cast], refs[8 + n_cast:])
    is_prompt = pl.program_id(0) < P_ROWS // ODD_TILE

    @pl.when(is_prompt)
    def _():
        x = x_ref[...]
        h = _rms(x, g_ref[2:3, :]).astype(BF16)
        h_ext = jnp.concatenate([_rms(halo_ref[...], g_ref[2:3, :]).astype(BF16), h], axis=0)
        bg = _dot(h, w_in_ref[:, :D_MODEL])
        u_ext = _dot(h_ext, w_in_ref[:, D_MODEL:2 * D_MODEL]) * _dot(h_ext, w_in_ref[:, 2 * D_MODEL:])
        tail_ref[...] = u_ext[ODD_HALO + ODD_TILE - TAIL_ROWS:, :]
        y = _conv3(u_ext, cw_ref)[ODD_HALO:, :]
        m = _dot((bg * y).astype(BF16), w_out_ref[...])
        out = x + _rms(m, g_ref[3:4, :])
        o_ref[...] = jnp.where(_is_pad_row(pl.program_id(0) * ODD_TILE, ODD_TILE), 0.0, out)

    @pl.when(jnp.logical_not(is_prompt))
    def _():
        o_ref[...] = x_ref[...]
        tail_ref[...] = jnp.zeros(tail_ref.shape, F32)


def _odd_prompt(x, gains, w_in, conv_w, w_out, j, ffn_views, cast_targets):
    tiles = pl.cdiv(N_ROWS, ODD_TILE)
    row = lambda i: (i, 0)
    per = ODD_TILE // ODD_HALO
    halo = lambda i: (jnp.maximum(i * per - 1, 0), 0)
    cast_in, cast_out, cast_shapes = _cast_plan(cast_targets, lambda i: i)
    out = pl.pallas_call(
        functools.partial(_odd_prompt_body, n_cast=len(cast_in)),
        out_shape=(jax.ShapeDtypeStruct((N_ROWS, D_MODEL), F32),
                   jax.ShapeDtypeStruct((tiles, TAIL_ROWS, D_MODEL), F32), *cast_shapes),
        grid=(tiles,),
        in_specs=[
            pl.BlockSpec((ODD_TILE, D_MODEL), row),
            pl.BlockSpec((ODD_HALO, D_MODEL), halo),
            pl.BlockSpec((6, D_MODEL), lambda i: (0, 0)),
            pl.BlockSpec((None, D_MODEL, 3 * D_MODEL), lambda i: (j, 0, 0), pipeline_mode=pl.Buffered(1)),
            pl.BlockSpec((None, CONV_WIDTH, D_MODEL), lambda i: (j, 0, 0)),
            pl.BlockSpec((None, D_MODEL, D_MODEL), lambda i: (j, 0, 0), pipeline_mode=pl.Buffered(1)),
            *cast_in,
        ],
        out_specs=(pl.BlockSpec((ODD_TILE, D_MODEL), row),
                   pl.BlockSpec((None, TAIL_ROWS, D_MODEL), lambda i: (i, 0, 0)), *cast_out),
        compiler_params=_params("arbitrary"),
        name=f"odd_prompt_{j}",
    )(x, x, gains, w_in, conv_w, w_out, *(ffn_views * len(cast_targets)))
    return out[0], out[1], out[2:]


def _odd_sample_body(x_ref, g_ref, w_in_ref, cw_ref, w_out_ref, cache_ref, o_ref, tail_ref):
    x = x_ref[...]
    h = _rms(x, g_ref[2:3, :]).astype(BF16)
    bg = _dot(h, w_in_ref[:, :D_MODEL])
    u = _dot(h, w_in_ref[:, D_MODEL:2 * D_MODEL]) * _dot(h, w_in_ref[:, 2 * D_MODEL:])
    ys = []
    for b in range(DEC_BATCH):
        ub = u[b * DEC_SEQ:(b + 1) * DEC_SEQ, :]
        ys.append(_conv3(ub, cw_ref, before1=cache_ref[b, 1:2, :], before2=cache_ref[b, 0:1, :]))
        tail_ref[b] = ub[DEC_SEQ - (CONV_WIDTH - 1):, :]
    m = _dot((bg * jnp.concatenate(ys, axis=0)).astype(BF16), w_out_ref[...])
    o_ref[...] = x + _rms(m, g_ref[3:4, :])


def _odd_sample(x, gains, w_in, conv_w, w_out, cache, j):
    rows = lambda i: (P_ROWS // S_ROWS, 0)
    whole3 = lambda i: (0, 0, 0)
    return pl.pallas_call(
        _odd_sample_body,
        out_shape=(jax.ShapeDtypeStruct((N_ROWS, D_MODEL), F32),
                   jax.ShapeDtypeStruct((DEC_BATCH, CONV_WIDTH - 1, D_MODEL), F32)),
        grid=(1,),
        in_specs=[
            pl.BlockSpec((S_ROWS, D_MODEL), rows),
            pl.BlockSpec((6, D_MODEL), lambda i: (0, 0)),
            pl.BlockSpec((None, D_MODEL, 3 * D_MODEL), lambda i: (j, 0, 0), pipeline_mode=pl.Buffered(1)),
            pl.BlockSpec((None, CONV_WIDTH, D_MODEL), lambda i: (j, 0, 0)),
            pl.BlockSpec((None, D_MODEL, D_MODEL), lambda i: (j, 0, 0), pipeline_mode=pl.Buffered(1)),
            pl.BlockSpec((DEC_BATCH, CONV_WIDTH - 1, D_MODEL), whole3),
        ],
        out_specs=(pl.BlockSpec((S_ROWS, D_MODEL), rows),
                   pl.BlockSpec((DEC_BATCH, CONV_WIDTH - 1, D_MODEL), whole3)),
        input_output_aliases={0: 0},
        compiler_params=_params("arbitrary"),
        name=f"odd_sample_{j}",
    )(x, gains, w_in, conv_w, w_out, cache)


def _stream_tails(a, row0, streams, length, n):
    ends = [row0 + (s + 1) * length for s in range(streams)]
    return jnp.stack([lax.slice_in_dim(a, e - n, e, axis=0) for e in ends])


def kernel(x_prompt, x_sample, cache_swa_k, cache_swa_v, state_hgrn, cache_conv, meta_tokens, norm_gains,
           w_ffn_gate, w_ffn_up, w_ffn_down, w_in_even, w_out_even, attn_sinks, hgrn_lb_logits,
           hgrn_norm_gain, w_in_odd, conv_w, w_out_odd):
    win = cache_swa_k.shape[2]
    lead = jnp.concatenate([jnp.zeros((PAD, D_MODEL), F32), meta_tokens.astype(F32)], axis=0)
    pieces = [p for b in range(BATCH) for p in (lead, x_prompt[b])]
    x = jnp.concatenate(pieces + [x_sample.reshape(S_ROWS, D_MODEL)], axis=0)

    ffn_views = (w_ffn_gate, w_ffn_up, w_ffn_down)
    ffn_w = {(0, 0): (w_ffn_gate[0, 0].astype(BF16), w_ffn_up[0, 0].astype(BF16),
                      w_ffn_down[0, 0].astype(BF16))}

    def keep_converted(targets, converted):
        for n, key in enumerate(targets):
            ffn_w[key] = tuple(converted[3 * n:3 * n + 3])

    w_in_e, w_out_e = w_in_even.astype(BF16), w_out_even.astype(BF16)
    w_in_o, w_out_o = w_in_odd.astype(BF16), w_out_odd.astype(BF16)
    cache_k = cache_swa_k.reshape(N_EVEN, DEC_BATCH, win, A_KV_W)
    cache_v = cache_swa_v.reshape(N_EVEN, DEC_BATCH, win, A_KV_W)

    k_p, v_p, s_p, c_p, k_s, v_s, s_s, c_s = ([] for _ in range(8))
    for layer in range(DEPTH):
        gains = norm_gains[layer]
        j = layer // 2
        x = _ffn(x, norm_gains, ffn_w[(layer, 0)], layer, 0)
        targets = [(layer, 1)] + ([(layer + 1, 0)] if layer + 1 < DEPTH else [])
        if layer % 2 == 0:
            q, k, v, kk, vv, hq, hk, hb, hv, hg = _even_in(x, gains, w_in_e, hgrn_lb_logits, j)
            oa = _attn_prompt(attn_sinks[j], q, kk, vv)
            oa = _attn_sample(attn_sinks[j], q, k, v, cache_k[j], cache_v[j], oa)
            ob, st_p, converted = _hgrn_prompt(hq, hk, hb, hv, hg, hgrn_norm_gain[j], ffn_views, targets)
            keep_converted(targets, converted)
            ob, st_s = _hgrn_sample(hq, hk, hb, hv, hg, hgrn_norm_gain[j], state_hgrn[j], ob)
            x = _even_out(x, gains, oa, ob, w_out_e, j)
            for new, tail_p, tail_s, cache in ((k, k_p, k_s, cache_k[j]), (v, v_p, v_s, cache_v[j])):
                tail_p.append(_stream_tails(new, 0, BATCH, LP, WINDOW).reshape(
                    BATCH, WINDOW, A_KV_HEADS, A_HEAD_DIM))
                both = jnp.concatenate([cache, new[P_ROWS:].reshape(DEC_BATCH, DEC_SEQ, A_KV_W)], axis=1)
                tail_s.append(both[:, both.shape[1] - win:].reshape(DEC_BATCH, win, A_KV_HEADS, A_HEAD_DIM))
            s_p.append(st_p)
            s_s.append(st_s)
        else:
            x, conv_s = _odd_sample(x, gains, w_in_o, conv_w, w_out_o, cache_conv[j], j)
            x, tails, converted = _odd_prompt(x, gains, w_in_o, conv_w, w_out_o, j, ffn_views, targets)
            keep_converted(targets, converted)
            per_stream = LP // ODD_TILE
            c_p.append(jnp.stack([tails[(b + 1) * per_stream - 1, TAIL_ROWS - (CONV_WIDTH - 1):]
                                  for b in range(BATCH)]))
            c_s.append(conv_s)
        x = _ffn(x, norm_gains, ffn_w[(layer, 1)], layer, 1)

    y_prompt = _stream_tails(x, 0, BATCH, LP, SEQ)
    y_sample = x[P_ROWS:].reshape(DEC_BATCH, DEC_SEQ, D_MODEL)
    return (y_prompt, y_sample, jnp.stack(k_p), jnp.stack(v_p), jnp.stack(s_p), jnp.stack(c_p),
            jnp.stack(k_s), jnp.stack(v_s), jnp.stack(s_s), jnp.stack(c_s))
```

```python
import functools

import jax
import jax.numpy as jnp
from jax import lax
from jax.experimental import pallas as pl
from jax.experimental.pallas import tpu as pltpu

F32 = jnp.float32
BF16 = jnp.bfloat16

D_MODEL = 1024
BATCH = 4
SEQ = 4096
DEPTH = 4
DEC_BATCH = 8
DEC_SEQ = 32
CHUNK = 64
N_META = 16
WINDOW = 128
WINDOW_CHUNKS = WINDOW // CHUNK
A_HEADS = 8
A_KV_HEADS = 2
A_HEAD_DIM = 64
A_Q_W = A_HEADS * A_HEAD_DIM
A_KV_W = A_KV_HEADS * A_HEAD_DIM
B_HEADS = 4
B_KEY_DIM = 128
B_VAL_DIM = 128
B_QK_W = B_HEADS * B_KEY_DIM
B_V_W = B_HEADS * B_VAL_DIM
B_BLOCK = 16
CONV_WIDTH = 3
D_FF = 2816
FFN_RESIDUAL = 0.5
N_EVEN = (DEPTH + 1) // 2
N_ODD = DEPTH // 2
EVEN_IN_W = A_Q_W + 2 * A_KV_W + 2 * B_QK_W + 2 * B_V_W
EPS = 1e-6
MASK_VALUE = -1e30
LB_FLOOR = 1e-30

PAD = (-N_META) % CHUNK
LP = PAD + N_META + SEQ
N_CHUNKS = LP // CHUNK
P_ROWS = BATCH * LP
S_ROWS = DEC_BATCH * DEC_SEQ
N_ROWS = P_ROWS + S_ROWS

ROW_TILE = 512
FF_CHUNK = 256
HGRN_TILE = 832
HGRN_ROWS = 64
HGRN_SUB = 8
HGRN_GROUP = 3
LOG2_E = 1.4426950408889634
CUMSUM_ROWS = 256
FFN_TILE = 1536
FFN_SUB = 512
ATTN_CHUNKS = 5
CAST_SLAB = 256
CAST_STEPS = D_FF // CAST_SLAB
ODD_TILE = 832
ODD_HALO = 16
TAIL_ROWS = 8
VMEM_LIMIT = 56 * 1024 * 1024

assert N_ROWS % ROW_TILE == 0 and D_FF % FF_CHUNK == 0
assert LP % HGRN_TILE == 0 and HGRN_TILE % HGRN_ROWS == 0 and ROW_TILE % HGRN_ROWS == 0
assert HGRN_ROWS % DEC_SEQ == 0 and DEC_SEQ % HGRN_SUB == 0
assert ROW_TILE % CUMSUM_ROWS == 0 and CUMSUM_ROWS % HGRN_ROWS == 0
assert N_ROWS % FFN_TILE == 0 and FFN_TILE % FFN_SUB == 0 and N_CHUNKS % ATTN_CHUNKS == 0
assert D_FF % CAST_SLAB == 0 and CAST_STEPS <= min(P_ROWS // ODD_TILE, P_ROWS // HGRN_TILE)
assert LP % ODD_TILE == 0 and ODD_TILE % ODD_HALO == 0 and P_ROWS % DEC_SEQ == 0
assert P_ROWS % S_ROWS == 0 and S_ROWS <= ODD_TILE and TAIL_ROWS >= CONV_WIDTH - 1


def _params(*sem):
    return pltpu.CompilerParams(dimension_semantics=sem, vmem_limit_bytes=VMEM_LIMIT)


def _rms(x, gain):
    return x * lax.rsqrt(jnp.mean(x * x, axis=-1, keepdims=True) + EPS) * gain


def _silu(x):
    return x * jax.nn.sigmoid(x)


def _dot(a, b):
    return jnp.dot(a, b, preferred_element_type=F32)


def _dot_nt(a, b):
    return lax.dot_general(a, b, (((1,), (1,)), ((), ())), preferred_element_type=F32)


def _dot_tn(a, b):
    return lax.dot_general(a, b, (((0,), (0,)), ((), ())), preferred_element_type=F32)


def _is_pad_row(row0, rows):
    r = row0 + lax.broadcasted_iota(jnp.int32, (rows, 1), 0)
    pad = None
    for b in range(BATCH):
        hit = jnp.logical_and(r >= b * LP, r < b * LP + PAD)
        pad = hit if pad is None else jnp.logical_or(pad, hit)
    return pad


def _ffn_body(sel_ref, x_ref, g_ref, wg_ref, wu_ref, wd_ref, o_ref):
    pre = sel_ref[1] * 4
    post = pre + 1
    pieces = [slice(lo, lo + FFN_SUB) for lo in range(0, FFN_TILE, FFN_SUB)]
    chunks = [slice(c * FF_CHUNK, (c + 1) * FF_CHUNK) for c in range(D_FF // FF_CHUNK)]
    xs = [x_ref[rows, :] for rows in pieces]
    hs = [_rms(x, g_ref[pl.ds(pre, 1), :]).astype(BF16) for x in xs]
    accs = []
    for h in hs:
        acc = jnp.zeros((FFN_SUB, D_MODEL), F32)
        for cols in chunks:
            act = (_silu(_dot(h, wg_ref[:, cols])) * _dot(h, wu_ref[:, cols])).astype(BF16)
            acc = acc + _dot(act, wd_ref[cols, :])
        accs.append(acc)
    for rows, x, acc in zip(pieces, xs, accs):
        o_ref[rows, :] = x + FFN_RESIDUAL * _rms(acc, g_ref[pl.ds(post, 1), :])


def _ffn(x, norm_gains, weights, layer, which):
    row = lambda i, sel: (i, 0)
    whole = lambda i, sel: (0, 0)
    resident = pl.Buffered(1)
    grid_spec = pltpu.PrefetchScalarGridSpec(
        num_scalar_prefetch=1,
        grid=(N_ROWS // FFN_TILE,),
        in_specs=[
            pl.BlockSpec((FFN_TILE, D_MODEL), row),
            pl.BlockSpec((None, 6, D_MODEL), lambda i, sel: (sel[0], 0, 0)),
            pl.BlockSpec((D_MODEL, D_FF), whole, pipeline_mode=resident),
            pl.BlockSpec((D_MODEL, D_FF), whole, pipeline_mode=resident),
            pl.BlockSpec((D_FF, D_MODEL), whole, pipeline_mode=resident),
        ],
        out_specs=pl.BlockSpec((FFN_TILE, D_MODEL), row),
    )
    return pl.pallas_call(
        _ffn_body,
        out_shape=jax.ShapeDtypeStruct((N_ROWS, D_MODEL), F32),
        grid_spec=grid_spec,
        input_output_aliases={1: 0},
        compiler_params=_params("parallel"),
        name="ffn",
    )(jnp.array([layer, which], jnp.int32), x, norm_gains, *weights)


def _cast_plan(targets, step_of):
    slab = lambda *ids: jnp.minimum(step_of(*ids), CAST_STEPS - 1)
    in_specs, out_specs, shapes = [], [], []
    for layer, which in targets:
        for block, shape, place in (((D_MODEL, CAST_SLAB), (D_MODEL, D_FF), lambda s: (0, s)),
                                    ((D_MODEL, CAST_SLAB), (D_MODEL, D_FF), lambda s: (0, s)),
                                    ((CAST_SLAB, D_MODEL), (D_FF, D_MODEL), lambda s: (s, 0))):
            in_specs.append(pl.BlockSpec((None, None) + block,
                                         lambda *ids, l=layer, w=which, p=place: (l, w) + p(slab(*ids))))
            out_specs.append(pl.BlockSpec(block, lambda *ids, p=place: p(slab(*ids))))
            shapes.append(jax.ShapeDtypeStruct(shape, BF16))
    return in_specs, out_specs, shapes


def _cast_slabs(step, srcs, dsts):
    @pl.when(step < CAST_STEPS)
    def _():
        for src, dst in zip(srcs, dsts):
            dst[...] = src[...].astype(BF16)


def _lower_bound(lbl_ref, j):
    rows = [lbl_ref[i:i + 1, :] for i in range(N_EVEN)]
    top = functools.reduce(jnp.maximum, rows)
    e = [jnp.exp(r - top) for r in rows]
    total = functools.reduce(lambda a, b: a + b, e)
    p = [v / total for v in e]
    cum = functools.reduce(lambda a, b: a + b, p[:j + 1])
    return jnp.maximum(cum - p[0], 0.0)


def _span_cumsum(g, row0):
    out = []
    for lo in range(0, g.shape[0], CUMSUM_ROWS):
        part = g[lo:lo + CUMSUM_ROWS, :]
        t = row0 + lo + lax.broadcasted_iota(jnp.int32, (CUMSUM_ROWS, CUMSUM_ROWS), 0)
        u = row0 + lo + lax.broadcasted_iota(jnp.int32, (CUMSUM_ROWS, CUMSUM_ROWS), 1)
        span = jnp.where(t >= P_ROWS, DEC_SEQ, HGRN_ROWS)
        tri = jnp.where(jnp.logical_and((t ^ u) < span, u <= t), 1.0, 0.0).astype(BF16)
        total = None
        for _ in range(3):
            term = part.astype(BF16)
            part = part - term.astype(F32)
            total = _dot(tri, term) if total is None else total + _dot(tri, term)
        out.append(total)
    return jnp.concatenate(out, axis=0)


def _even_in_body(x_ref, g_ref, w_ref, lbl_ref, q_ref, k_ref, v_ref, kk_ref, vv_ref,
                  hq_ref, hk_ref, hb_ref, hv_ref, hg_ref, *, j):
    h = _rms(x_ref[...], g_ref[2:3, :]).astype(BF16)
    widths = (A_Q_W, 2 * A_KV_W, B_QK_W, B_QK_W, B_V_W, B_V_W)
    starts = [sum(widths[:i]) for i in range(len(widths))]
    proj = lambda i: _dot(h, w_ref[:, starts[i]:starts[i] + widths[i]])
    fx = proj(3)
    q_ref[...] = (proj(0) * (A_HEAD_DIM ** -0.5)).astype(BF16)
    kv = proj(1)
    k_ref[...] = kv[:, :A_KV_W]
    v_ref[...] = kv[:, A_KV_W:]
    kk_ref[...] = jnp.concatenate(_half_lane_variants(kv[:, :A_KV_W]), axis=1)
    vv_ref[...] = jnp.concatenate(_half_lane_variants(kv[:, A_KV_W:]), axis=1)
    hq_ref[...] = _silu(proj(2))
    hv_ref[...] = proj(4).astype(BF16)
    hg_ref[...] = _silu(proj(5))
    lb = _lower_bound(lbl_ref, j)
    soft = jnp.log1p(jnp.exp(-jnp.abs(fx)))
    ls_pos = jnp.minimum(fx, 0.0) - soft
    ls_neg = jnp.minimum(-fx, 0.0) - soft
    other = jnp.log(jnp.maximum(lb, LB_FLOOR)) + ls_neg
    log_f = jnp.maximum(ls_pos, other) + jnp.log1p(jnp.exp(-jnp.abs(ls_pos - other)))
    hk_ref[...] = (1.0 - lb) * jax.nn.sigmoid(-fx)
    hb_ref[...] = _span_cumsum(log_f, pl.program_id(0) * ROW_TILE)


def _even_in(x, gains, w_in, lb_logits, j):
    row = lambda i: (i, 0)
    whole = lambda i: (0, 0)
    wide = lambda w, dt: jax.ShapeDtypeStruct((N_ROWS, w), dt)
    spec = lambda w: pl.BlockSpec((ROW_TILE, w), row)
    return pl.pallas_call(
        functools.partial(_even_in_body, j=j),
        out_shape=(wide(A_Q_W, BF16), wide(A_KV_W, F32), wide(A_KV_W, F32),
                   wide(4 * A_KV_W, BF16), wide(4 * A_KV_W, BF16),
                   wide(B_QK_W, F32), wide(B_QK_W, F32), wide(B_QK_W, F32),
                   wide(B_V_W, BF16), wide(B_V_W, F32)),
        grid=(N_ROWS // ROW_TILE,),
        in_specs=[
            pl.BlockSpec((ROW_TILE, D_MODEL), row),
            pl.BlockSpec((6, D_MODEL), whole),
            pl.BlockSpec((None, D_MODEL, EVEN_IN_W), lambda i: (j, 0, 0),
                         pipeline_mode=pl.Buffered(1)),
            pl.BlockSpec((N_EVEN, B_QK_W), whole),
        ],
        out_specs=(spec(A_Q_W), spec(A_KV_W), spec(A_KV_W), spec(4 * A_KV_W), spec(4 * A_KV_W),
                   spec(B_QK_W), spec(B_QK_W), spec(B_QK_W), spec(B_V_W), spec(B_V_W)),
        compiler_params=_params("parallel"),
        name=f"even_in_{j}",
    )(x, gains, w_in, lb_logits)


def _half_lane_variants(a):
    lane = lax.broadcasted_iota(jnp.int32, a.shape, 1)
    low = lane < A_HEAD_DIM
    swapped = pltpu.roll(a, A_HEAD_DIM, 1)
    zero = jnp.zeros_like(a)
    return (jnp.where(low, a, zero).astype(BF16), jnp.where(low, zero, swapped).astype(BF16),
            jnp.where(low, swapped, zero).astype(BF16), jnp.where(low, zero, a).astype(BF16))


def _sink_softmax(s, valid, sink):
    if valid is not None:
        s = jnp.where(valid, s, MASK_VALUE)
    m = jnp.maximum(jnp.max(s, axis=-1, keepdims=True), sink)
    p = jnp.exp(s - m)
    denom = jnp.sum(p, axis=-1, keepdims=True) + jnp.exp(sink - m)
    return (p / denom).astype(BF16)


def _group_sinks(sink_ref, kvh, rows):
    first = lax.broadcasted_iota(jnp.int32, (2 * rows, 1), 0) < rows
    base = kvh * (A_HEADS // A_KV_HEADS)
    return (jnp.where(first, sink_ref[base], sink_ref[base + 2]),
            jnp.where(first, sink_ref[base + 1], sink_ref[base + 3]))


def _attend(items, sinks):
    scores = []
    for q, keys, _, _ in items:
        for kvh in range(A_KV_HEADS):
            c0 = kvh * 256
            q2 = jnp.concatenate([q[:, c0:c0 + 128], q[:, c0 + 128:c0 + 256]], axis=0)
            scores += [_dot_nt(q2, keys[2 * kvh]), _dot_nt(q2, keys[2 * kvh + 1])]
    weights = []
    for n, (_, _, _, valid) in enumerate(items):
        for kvh in range(A_KV_HEADS):
            for half in range(2):
                weights.append(_sink_softmax(scores[4 * n + 2 * kvh + half], valid, sinks[kvh][half]))
    outs = []
    for n, (q, _, vals, _) in enumerate(items):
        rows = q.shape[0]
        groups = []
        for kvh in range(A_KV_HEADS):
            w = weights[4 * n + 2 * kvh:4 * n + 2 * kvh + 2]
            o = _dot(w[0], vals[2 * kvh]) + _dot(w[1], vals[2 * kvh + 1])
            groups.append(jnp.concatenate([o[:rows], o[rows:]], axis=1))
        outs.append(jnp.concatenate(groups, axis=1))
    return outs


def _attn_body(sink_ref, q_ref, kk_ref, vv_ref, ck_ref, cv_ref, o_ref):
    variant = [slice(v * A_KV_W, (v + 1) * A_KV_W) for v in range(4)]

    @pl.when(pl.program_id(0) < BATCH)
    def _():
        _attn_prompt_stream(sink_ref, q_ref, kk_ref, vv_ref, o_ref, variant)

    @pl.when(pl.program_id(0) == BATCH)
    def _():
        sinks = [_group_sinks(sink_ref, kvh, DEC_SEQ) for kvh in range(A_KV_HEADS)]
        items = []
        for b in range(DEC_BATCH):
            rows = slice(b * DEC_SEQ, (b + 1) * DEC_SEQ)
            old_k, old_v = _half_lane_variants(ck_ref[b]), _half_lane_variants(cv_ref[b])
            keys = [jnp.concatenate([old_k[v], kk_ref[rows, variant[v]]], axis=0) for v in range(4)]
            vals = [jnp.concatenate([old_v[v], vv_ref[rows, variant[v]]], axis=0) for v in range(4)]
            items.append((q_ref[rows, :], keys, vals, None))
        for b, out in enumerate(_attend(items, sinks)):
            o_ref[b * DEC_SEQ:(b + 1) * DEC_SEQ, :] = out.astype(BF16)


def _attn_prompt_stream(sink_ref, q_ref, kk_ref, vv_ref, o_ref, variant):
    span = (WINDOW_CHUNKS + 1) * CHUNK
    sinks = [_group_sinks(sink_ref, kvh, CHUNK) for kvh in range(A_KV_HEADS)]

    def step(n, carry):
        starts, items = [], []
        for i in range(ATTN_CHUNKS):
            c = n * ATTN_CHUNKS + i
            r0 = pl.multiple_of(c * CHUNK, CHUNK)
            s0 = pl.multiple_of(jnp.maximum(c - WINDOW_CHUNKS, 0) * CHUNK, CHUNK)
            kpos = s0 + lax.broadcasted_iota(jnp.int32, (1, span), 1)
            valid = jnp.logical_and(kpos >= PAD, kpos < r0 + CHUNK)
            starts.append(r0)
            items.append((q_ref[pl.ds(r0, CHUNK), :],
                          [kk_ref[pl.ds(s0, span), v] for v in variant],
                          [vv_ref[pl.ds(s0, span), v] for v in variant], valid))
        for r0, out in zip(starts, _attend(items, sinks)):
            o_ref[pl.ds(r0, CHUNK), :] = out.astype(BF16)
        return carry

    lax.fori_loop(0, N_CHUNKS // ATTN_CHUNKS, step, 0)


def _attn(sink, q, kk, vv, cache_k, cache_v):
    seq = lambda b: (b, 0)
    old = pl.BlockSpec(cache_k.shape, lambda b: (0, 0, 0))
    return pl.pallas_call(
        _attn_body,
        out_shape=jax.ShapeDtypeStruct((N_ROWS, A_Q_W), BF16),
        grid=(BATCH + 1,),
        in_specs=[
            pl.BlockSpec(memory_space=pltpu.SMEM),
            pl.BlockSpec((LP, A_Q_W), seq),
            pl.BlockSpec((LP, 4 * A_KV_W), seq),
            pl.BlockSpec((LP, 4 * A_KV_W), seq),
            old, old,
        ],
        out_specs=pl.BlockSpec((LP, A_Q_W), seq),
        compiler_params=_params("parallel"),
        name="attn",
    )(sink, q, kk, vv, cache_k, cache_v)


def _group_row(a, group, row):
    spans = a.shape[0] // group
    picked = a.reshape(spans, group, a.shape[1])[:, row:row + 1, :]
    return jnp.broadcast_to(picked, (spans, group, a.shape[1])).reshape(a.shape)


def _hgrn_pairs(q, k, c2):
    parts = []
    for s in range(HGRN_SUB):
        decay = jnp.exp2(jnp.minimum(c2 - _group_row(c2, HGRN_SUB, s), 0.0))
        parts.append((q * _group_row(k, HGRN_SUB, s) * decay).astype(BF16))
    return jnp.concatenate(parts, axis=1)


def _hgrn_pick(rows):
    shape = (HGRN_SUB * B_KEY_DIM, rows)
    block = lax.broadcasted_iota(jnp.int32, shape, 0) // B_KEY_DIM
    col = lax.broadcasted_iota(jnp.int32, shape, 1) & (HGRN_SUB - 1)
    return jnp.where(block == col, 1.0, 0.0).astype(BF16)


def _hgrn_head(q, c, v, state_t, intra, levels, update):
    rows = q.shape[0]
    t_idx = lax.broadcasted_iota(jnp.int32, (rows, rows), 0)
    u_idx = lax.broadcasted_iota(jnp.int32, (rows, rows), 1)
    apart = t_idx ^ u_idx
    in_group = jnp.logical_and(apart < HGRN_SUB, u_idx <= t_idx)
    scores = jnp.where(in_group, intra, 0.0)
    for w, level in levels:
        scores = scores + (level if 2 * w == rows else jnp.where(apart < 2 * w, level, 0.0))
    o = _dot(scores.astype(BF16), v) + _dot_nt((q * jnp.exp2(c)).astype(BF16), state_t.astype(BF16))
    return o, state_t * jnp.exp2(c[rows - 1:rows, :]) + update


def _hgrn_levels(q, k, c):
    rows = q.shape[0]
    pos = lax.broadcasted_iota(jnp.int32, q.shape, 0)
    levels = []
    w = HGRN_SUB
    while w < rows:
        rho = _group_row(c, 2 * w, w - 1)
        upper = (pos & w) != 0
        q_up = jnp.where(upper, q * jnp.exp2(jnp.minimum(c - rho, 0.0)), 0.0).astype(BF16)
        k_low = jnp.where(upper, 0.0, k * jnp.exp2(jnp.minimum(rho - c, 0.0))).astype(BF16)
        levels.append((w, _dot_nt(q_up, k_low)))
        w *= 2
    return levels


def _hgrn_tiles(tiles, n, hq_ref, hk_ref, hc_ref, hv_ref, hg_ref, gain_ref, o_ref, st_ref, pick):
    loaded = [(hq_ref[r, :], hk_ref[r, :], hc_ref[r, :] * LOG2_E, hv_ref[r, :], hg_ref[r, :])
              for r in tiles]
    states = [st_ref[hd] for hd in range(B_HEADS)]
    heads = [slice(hd * B_KEY_DIM, (hd + 1) * B_KEY_DIM) for hd in range(B_HEADS)]
    intras = [_dot(jnp.concatenate([_hgrn_pairs(q[:, h], k[:, h], c[:, h]) for h in heads], axis=0), pick)
              for q, k, c, _, _ in loaded]
    levels = [[_hgrn_levels(q[:, h], k[:, h], c[:, h]) for h in heads] for q, k, c, _, _ in loaded]
    updates = [[_dot_tn(v[:, h], (k[:, h] * jnp.exp2(c[n - 1:n, h] - c[:, h])).astype(BF16))
                for h in heads] for _, k, c, v, _ in loaded]
    results = []
    for t, (q, k, c, v, g) in enumerate(loaded):
        outs = []
        for hd, h in enumerate(heads):
            o, states[hd] = _hgrn_head(q[:, h], c[:, h], v[:, h], states[hd],
                                       intras[t][hd * n:(hd + 1) * n, :], levels[t][hd], updates[t][hd])
            outs.append(_rms(o, gain_ref[hd:hd + 1, :]) * g[:, h])
        results.append(jnp.concatenate(outs, axis=1).astype(BF16))
    for r, res in zip(tiles, results):
        o_ref[r, :] = res
    for hd in range(B_HEADS):
        st_ref[hd] = states[hd]


def _hgrn_body(*refs, n_cast):
    hq_ref, hk_ref, hc_ref, hv_ref, hg_ref, gain_ref, s0_ref = refs[:7]
    o_ref, sp_ref, ss_ref = refs[7 + n_cast:10 + n_cast]
    st_ref = refs[-1]
    step = pl.program_id(0)
    tiles = LP // HGRN_TILE
    tile = lax.rem(step, tiles)
    is_prompt = step < BATCH * tiles
    _cast_slabs(step, refs[7:7 + n_cast], refs[10 + n_cast:10 + 2 * n_cast])
    mixer = (hq_ref, hk_ref, hc_ref, hv_ref, hg_ref, gain_ref, o_ref, st_ref)

    @pl.when(jnp.logical_and(is_prompt, tile == 0))
    def _():
        st_ref[...] = jnp.zeros(st_ref.shape, F32)

    @pl.when(is_prompt)
    def _():
        pick = _hgrn_pick(HGRN_ROWS)
        n_tiles = HGRN_TILE // HGRN_ROWS

        def group(n, carry):
            first = n * HGRN_GROUP
            spans = [pl.ds(pl.multiple_of((first + i) * HGRN_ROWS, HGRN_ROWS), HGRN_ROWS)
                     for i in range(HGRN_GROUP)]
            _hgrn_tiles(spans, HGRN_ROWS, *mixer, pick)
            return carry

        lax.fori_loop(0, n_tiles // HGRN_GROUP, group, 0)
        rest = [pl.ds(i * HGRN_ROWS, HGRN_ROWS) for i in range(n_tiles - n_tiles % HGRN_GROUP, n_tiles)]
        if rest:
            _hgrn_tiles(rest, HGRN_ROWS, *mixer, pick)

    @pl.when(jnp.logical_and(is_prompt, tile == tiles - 1))
    def _():
        for hd in range(B_HEADS):
            sp_ref[hd] = st_ref[hd].T

    @pl.when(jnp.logical_not(is_prompt))
    def _():
        pick = _hgrn_pick(DEC_SEQ)
        for b in range(DEC_BATCH):
            for hd in range(B_HEADS):
                st_ref[hd] = s0_ref[b, hd].T
            _hgrn_tiles([pl.ds(b * DEC_SEQ, DEC_SEQ)], DEC_SEQ, *mixer, pick)
            for hd in range(B_HEADS):
                ss_ref[b, hd] = st_ref[hd].T


def _hgrn(hq, hk, hb, hv, hg, gain, s0, ffn_f32, cast_targets):
    tiles = LP // HGRN_TILE
    row = lambda s: (s, 0)
    spec = pl.BlockSpec((HGRN_TILE, B_QK_W), row)
    state = (B_HEADS, B_KEY_DIM, B_VAL_DIM)
    cast_in, cast_out, cast_shapes = _cast_plan(cast_targets, lambda s: s)
    out = pl.pallas_call(
        functools.partial(_hgrn_body, n_cast=len(cast_in)),
        out_shape=(jax.ShapeDtypeStruct((N_ROWS, B_V_W), BF16),
                   jax.ShapeDtypeStruct((BATCH,) + state, F32),
                   jax.ShapeDtypeStruct((DEC_BATCH,) + state, F32), *cast_shapes),
        grid=(BATCH * tiles + 1,),
        in_specs=[spec, spec, spec, spec, spec,
                  pl.BlockSpec((B_HEADS, B_VAL_DIM), lambda s: (0, 0)),
                  pl.BlockSpec((DEC_BATCH,) + state, lambda s: (0, 0, 0, 0)), *cast_in],
        out_specs=(spec,
                   pl.BlockSpec((None,) + state, lambda s: (jnp.minimum(s // tiles, BATCH - 1), 0, 0, 0)),
                   pl.BlockSpec((DEC_BATCH,) + state, lambda s: (0, 0, 0, 0)), *cast_out),
        scratch_shapes=[pltpu.VMEM((B_HEADS, B_VAL_DIM, B_KEY_DIM), F32)],
        compiler_params=_params("arbitrary"),
        name="hgrn",
    )(hq, hk, hb, hv, hg, gain, s0, *(ffn_f32 * len(cast_targets)))
    return out[0], out[1], out[2], out[3:]


def _even_out_body(x_ref, g_ref, oa_ref, ob_ref, w_ref, o_ref):
    for lo in range(0, FFN_TILE, FFN_SUB):
        rows = slice(lo, lo + FFN_SUB)
        m = _dot(oa_ref[rows, :], w_ref[:A_Q_W, :]) + _dot(ob_ref[rows, :], w_ref[A_Q_W:, :])
        y = x_ref[rows, :] + _rms(m, g_ref[3:4, :])
        pad = _is_pad_row(pl.program_id(0) * FFN_TILE + lo, FFN_SUB)
        o_ref[rows, :] = jnp.where(pad, 0.0, y)


def _even_out(x, gains, oa, ob, w_out, j):
    row = lambda i: (i, 0)
    return pl.pallas_call(
        _even_out_body,
        out_shape=jax.ShapeDtypeStruct((N_ROWS, D_MODEL), F32),
        grid=(N_ROWS // FFN_TILE,),
        in_specs=[
            pl.BlockSpec((FFN_TILE, D_MODEL), row),
            pl.BlockSpec((6, D_MODEL), lambda i: (0, 0)),
            pl.BlockSpec((FFN_TILE, A_Q_W), row),
            pl.BlockSpec((FFN_TILE, B_V_W), row),
            pl.BlockSpec((None, A_Q_W + B_V_W, D_MODEL), lambda i: (j, 0, 0),
                         pipeline_mode=pl.Buffered(1)),
        ],
        out_specs=pl.BlockSpec((FFN_TILE, D_MODEL), row),
        input_output_aliases={0: 0},
        compiler_params=_params("parallel"),
        name=f"even_out_{j}",
    )(x, gains, oa, ob, w_out)


def _conv3(u, cw_ref, before1=None, before2=None):
    shift1, shift2 = pltpu.roll(u, 1, 0), pltpu.roll(u, 2, 0)
    if before1 is not None:
        pos = lax.broadcasted_iota(jnp.int32, (u.shape[0], 1), 0)
        shift1 = jnp.where(pos == 0, before1, shift1)
        shift2 = jnp.where(pos == 0, before2, jnp.where(pos == 1, before1, shift2))
    return shift2 * cw_ref[0:1, :] + shift1 * cw_ref[1:2, :] + u * cw_ref[2:3, :]


def _odd_prompt_body(*refs, n_cast):
    x_ref, halo_ref, g_ref, w_in_ref, cw_ref, w_out_ref = refs[:6]
    o_ref, tail_ref = refs[6 + n_cast:8 + n_cast]
    _cast_slabs(pl.program_id(0), refs[6:6 + n_cast], refs[8 + n_cast:])
    is_prompt = pl.program_id(0) < P_ROWS // ODD_TILE

    @pl.when(is_prompt)
    def _():
        x = x_ref[...]
        h = _rms(x, g_ref[2:3, :]).astype(BF16)
        h_ext = jnp.concatenate([_rms(halo_ref[...], g_ref[2:3, :]).astype(BF16), h], axis=0)
        bg = _dot(h, w_in_ref[:, :D_MODEL])
        u_ext = _dot(h_ext, w_in_ref[:, D_MODEL:2 * D_MODEL]) * _dot(h_ext, w_in_ref[:, 2 * D_MODEL:])
        tail_ref[...] = u_ext[ODD_HALO + ODD_TILE - TAIL_ROWS:, :]
        y = _conv3(u_ext, cw_ref)[ODD_HALO:, :]
        m = _dot((bg * y).astype(BF16), w_out_ref[...])
        out = x + _rms(m, g_ref[3:4, :])
        o_ref[...] = jnp.where(_is_pad_row(pl.program_id(0) * ODD_TILE, ODD_TILE), 0.0, out)

    @pl.when(jnp.logical_not(is_prompt))
    def _():
        o_ref[...] = x_ref[...]
        tail_ref[...] = jnp.zeros(tail_ref.shape, F32)


def _odd_prompt(x, gains, w_in, conv_w, w_out, j, ffn_f32, cast_targets):
    tiles = pl.cdiv(N_ROWS, ODD_TILE)
    row = lambda i: (i, 0)
    per = ODD_TILE // ODD_HALO
    halo = lambda i: (jnp.maximum(i * per - 1, 0), 0)
    cast_in, cast_out, cast_shapes = _cast_plan(cast_targets, lambda i: i)
    out = pl.pallas_call(
        functools.partial(_odd_prompt_body, n_cast=len(cast_in)),
        out_shape=(jax.ShapeDtypeStruct((N_ROWS, D_MODEL), F32),
                   jax.ShapeDtypeStruct((tiles, TAIL_ROWS, D_MODEL), F32), *cast_shapes),
        grid=(tiles,),
        in_specs=[
            pl.BlockSpec((ODD_TILE, D_MODEL), row),
            pl.BlockSpec((ODD_HALO, D_MODEL), halo),
            pl.BlockSpec((6, D_MODEL), lambda i: (0, 0)),
            pl.BlockSpec((None, D_MODEL, 3 * D_MODEL), lambda i: (j, 0, 0), pipeline_mode=pl.Buffered(1)),
            pl.BlockSpec((None, CONV_WIDTH, D_MODEL), lambda i: (j, 0, 0)),
            pl.BlockSpec((None, D_MODEL, D_MODEL), lambda i: (j, 0, 0), pipeline_mode=pl.Buffered(1)),
            *cast_in,
        ],
        out_specs=(pl.BlockSpec((ODD_TILE, D_MODEL), row),
                   pl.BlockSpec((None, TAIL_ROWS, D_MODEL), lambda i: (i, 0, 0)), *cast_out),
        compiler_params=_params("arbitrary"),
        name=f"odd_prompt_{j}",
    )(x, x, gains, w_in, conv_w, w_out, *(ffn_f32 * len(cast_targets)))
    return out[0], out[1], out[2:]


def _odd_sample_body(x_ref, g_ref, w_in_ref, cw_ref, w_out_ref, cache_ref, o_ref, tail_ref):
    x = x_ref[...]
    h = _rms(x, g_ref[2:3, :]).astype(BF16)
    bg = _dot(h, w_in_ref[:, :D_MODEL])
    u = _dot(h, w_in_ref[:, D_MODEL:2 * D_MODEL]) * _dot(h, w_in_ref[:, 2 * D_MODEL:])
    ys = []
    for b in range(DEC_BATCH):
        ub = u[b * DEC_SEQ:(b + 1) * DEC_SEQ, :]
        ys.append(_conv3(ub, cw_ref, before1=cache_ref[b, 1:2, :], before2=cache_ref[b, 0:1, :]))
        tail_ref[b] = ub[DEC_SEQ - (CONV_WIDTH - 1):, :]
    m = _dot((bg * jnp.concatenate(ys, axis=0)).astype(BF16), w_out_ref[...])
    o_ref[...] = x + _rms(m, g_ref[3:4, :])


def _odd_sample(x, gains, w_in, conv_w, w_out, cache, j):
    rows = lambda i: (P_ROWS // S_ROWS, 0)
    whole3 = lambda i: (0, 0, 0)
    return pl.pallas_call(
        _odd_sample_body,
        out_shape=(jax.ShapeDtypeStruct((N_ROWS, D_MODEL), F32),
                   jax.ShapeDtypeStruct((DEC_BATCH, CONV_WIDTH - 1, D_MODEL), F32)),
        grid=(1,),
        in_specs=[
            pl.BlockSpec((S_ROWS, D_MODEL), rows),
            pl.BlockSpec((6, D_MODEL), lambda i: (0, 0)),
            pl.BlockSpec((None, D_MODEL, 3 * D_MODEL), lambda i: (j, 0, 0), pipeline_mode=pl.Buffered(1)),
            pl.BlockSpec((None, CONV_WIDTH, D_MODEL), lambda i: (j, 0, 0)),
            pl.BlockSpec((None, D_MODEL, D_MODEL), lambda i: (j, 0, 0), pipeline_mode=pl.Buffered(1)),
            pl.BlockSpec((DEC_BATCH, CONV_WIDTH - 1, D_MODEL), whole3),
        ],
        out_specs=(pl.BlockSpec((S_ROWS, D_MODEL), rows),
                   pl.BlockSpec((DEC_BATCH, CONV_WIDTH - 1, D_MODEL), whole3)),
        input_output_aliases={0: 0},
        compiler_params=_params("arbitrary"),
        name=f"odd_sample_{j}",
    )(x, gains, w_in, conv_w, w_out, cache)


def _stream_tails(a, row0, streams, length, n):
    ends = [row0 + (s + 1) * length for s in range(streams)]
    return jnp.stack([lax.slice_in_dim(a, e - n, e, axis=0) for e in ends])


def kernel(x_prompt, x_sample, cache_swa_k, cache_swa_v, state_hgrn, cache_conv, meta_tokens, norm_gains,
           w_ffn_gate, w_ffn_up, w_ffn_down, w_in_even, w_out_even, attn_sinks, hgrn_lb_logits,
           hgrn_norm_gain, w_in_odd, conv_w, w_out_odd):
    win = cache_swa_k.shape[2]
    lead = jnp.concatenate([jnp.zeros((PAD, D_MODEL), F32), meta_tokens.astype(F32)], axis=0)
    pieces = [p for b in range(BATCH) for p in (lead, x_prompt[b])]
    x = jnp.concatenate(pieces + [x_sample.reshape(S_ROWS, D_MODEL)], axis=0)

    ffn_f32 = (w_ffn_gate, w_ffn_up, w_ffn_down)
    ffn_w = {(0, 0): (w_ffn_gate[0, 0].astype(BF16), w_ffn_up[0, 0].astype(BF16),
                      w_ffn_down[0, 0].astype(BF16))}

    def keep_converted(targets, converted):
        for n, key in enumerate(targets):
            ffn_w[key] = tuple(converted[3 * n:3 * n + 3])

    w_in_e, w_out_e = w_in_even.astype(BF16), w_out_even.astype(BF16)
    w_in_o, w_out_o = w_in_odd.astype(BF16), w_out_odd.astype(BF16)
    cache_k = cache_swa_k.reshape(N_EVEN, DEC_BATCH, win, A_KV_W)
    cache_v = cache_swa_v.reshape(N_EVEN, DEC_BATCH, win, A_KV_W)

    k_p, v_p, s_p, c_p, k_s, v_s, s_s, c_s = ([] for _ in range(8))
    for layer in range(DEPTH):
        gains = norm_gains[layer]
        j = layer // 2
        x = _ffn(x, norm_gains, ffn_w[(layer, 0)], layer, 0)
        targets = [(layer, 1)] + ([(layer + 1, 0)] if layer + 1 < DEPTH else [])
        if layer % 2 == 0:
            q, k, v, kk, vv, hq, hk, hb, hv, hg = _even_in(x, gains, w_in_e, hgrn_lb_logits, j)
            oa = _attn(attn_sinks[j], q, kk, vv, cache_k[j], cache_v[j])
            ob, st_p, st_s, converted = _hgrn(hq, hk, hb, hv, hg, hgrn_norm_gain[j], state_hgrn[j],
                                              ffn_f32, targets)
            keep_converted(targets, converted)
            x = _even_out(x, gains, oa, ob, w_out_e, j)
            for new, tail_p, tail_s, cache in ((k, k_p, k_s, cache_k[j]), (v, v_p, v_s, cache_v[j])):
                tail_p.append(_stream_tails(new, 0, BATCH, LP, WINDOW).reshape(
                    BATCH, WINDOW, A_KV_HEADS, A_HEAD_DIM))
                both = jnp.concatenate([cache, new[P_ROWS:].reshape(DEC_BATCH, DEC_SEQ, A_KV_W)], axis=1)
                tail_s.append(both[:, both.shape[1] - win:].reshape(DEC_BATCH, win, A_KV_HEADS, A_HEAD_DIM))
            s_p.append(st_p)
            s_s.append(st_s)
        else:
            x, conv_s = _odd_sample(x, gains, w_in_o, conv_w, w_out_o, cache_conv[j], j)
            x, tails, converted = _odd_prompt(x, gains, w_in_o, conv_w, w_out_o, j, ffn_f32, targets)
            keep_converted(targets, converted)
            per_stream = LP // ODD_TILE
            c_p.append(jnp.stack([tails[(b + 1) * per_stream - 1, TAIL_ROWS - (CONV_WIDTH - 1):]
                                  for b in range(BATCH)]))
            c_s.append(conv_s)
        x = _ffn(x, norm_gains, ffn_w[(layer, 1)], layer, 1)

    y_prompt = _stream_tails(x, 0, BATCH, LP, SEQ)
    y_sample = x[P_ROWS:].reshape(DEC_BATCH, DEC_SEQ, D_MODEL)
    return (y_prompt, y_sample, jnp.stack(k_p), jnp.stack(v_p), jnp.stack(s_p), jnp.stack(c_p),
            jnp.stack(k_s), jnp.stack(v_s), jnp.stack(s_s), jnp.stack(c_s))
```

```python
import functools

import jax
import jax.numpy as jnp
from jax import lax
from jax.experimental import pallas as pl
from jax.experimental.pallas import tpu as pltpu

F32 = jnp.float32
BF16 = jnp.bfloat16

D_MODEL = 1024
BATCH = 4
SEQ = 4096
DEPTH = 4
DEC_BATCH = 8
DEC_SEQ = 32
CHUNK = 64
N_META = 16
WINDOW = 128
WINDOW_CHUNKS = WINDOW // CHUNK
A_HEADS = 8
A_KV_HEADS = 2
A_HEAD_DIM = 64
A_Q_W = A_HEADS * A_HEAD_DIM
A_KV_W = A_KV_HEADS * A_HEAD_DIM
B_HEADS = 4
B_KEY_DIM = 128
B_VAL_DIM = 128
B_QK_W = B_HEADS * B_KEY_DIM
B_V_W = B_HEADS * B_VAL_DIM
B_BLOCK = 16
CONV_WIDTH = 3
D_FF = 2816
FFN_RESIDUAL = 0.5
N_EVEN = (DEPTH + 1) // 2
N_ODD = DEPTH // 2
EVEN_IN_W = A_Q_W + 2 * A_KV_W + 2 * B_QK_W + 2 * B_V_W
EPS = 1e-6
MASK_VALUE = -1e30
LB_FLOOR = 1e-30

PAD = (-N_META) % CHUNK
LP = PAD + N_META + SEQ
N_CHUNKS = LP // CHUNK
P_ROWS = BATCH * LP
S_ROWS = DEC_BATCH * DEC_SEQ
N_ROWS = P_ROWS + S_ROWS

ROW_TILE = 512
FF_CHUNK = 256
HGRN_TILE = 832
HGRN_ROWS = 64
HGRN_SUB = 8
HGRN_GROUP = 3
LOG2_E = 1.4426950408889634
CUMSUM_ROWS = 256
FFN_TILE = 1536
FFN_SUB = 512
ATTN_CHUNKS = 5
CAST_SLAB = 256
CAST_STEPS = D_FF // CAST_SLAB
ODD_TILE = 832
ODD_HALO = 16
TAIL_ROWS = 8
VMEM_LIMIT = 56 * 1024 * 1024

assert N_ROWS % ROW_TILE == 0 and D_FF % FF_CHUNK == 0
assert LP % HGRN_TILE == 0 and HGRN_TILE % HGRN_ROWS == 0 and ROW_TILE % HGRN_ROWS == 0
assert HGRN_ROWS % DEC_SEQ == 0 and DEC_SEQ % HGRN_SUB == 0
assert ROW_TILE % CUMSUM_ROWS == 0 and CUMSUM_ROWS % HGRN_ROWS == 0
assert N_ROWS % FFN_TILE == 0 and FFN_TILE % FFN_SUB == 0 and N_CHUNKS % ATTN_CHUNKS == 0
assert D_FF % CAST_SLAB == 0 and CAST_STEPS <= min(P_ROWS // ODD_TILE, P_ROWS // HGRN_TILE)
assert LP % ODD_TILE == 0 and ODD_TILE % ODD_HALO == 0 and P_ROWS % DEC_SEQ == 0
assert P_ROWS % S_ROWS == 0 and S_ROWS <= ODD_TILE and TAIL_ROWS >= CONV_WIDTH - 1


def _params(*sem):
    return pltpu.CompilerParams(dimension_semantics=sem, vmem_limit_bytes=VMEM_LIMIT)


def _rms(x, gain):
    return x * lax.rsqrt(jnp.mean(x * x, axis=-1, keepdims=True) + EPS) * gain


def _silu(x):
    return x * jax.nn.sigmoid(x)


def _dot(a, b):
    return jnp.dot(a, b, preferred_element_type=F32)


def _dot_nt(a, b):
    return lax.dot_general(a, b, (((1,), (1,)), ((), ())), preferred_element_type=F32)


def _dot_tn(a, b):
    return lax.dot_general(a, b, (((0,), (0,)), ((), ())), preferred_element_type=F32)


def _is_pad_row(row0, rows):
    r = row0 + lax.broadcasted_iota(jnp.int32, (rows, 1), 0)
    pad = None
    for b in range(BATCH):
        hit = jnp.logical_and(r >= b * LP, r < b * LP + PAD)
        pad = hit if pad is None else jnp.logical_or(pad, hit)
    return pad


def _ffn_body(sel_ref, x_ref, g_ref, wg_ref, wu_ref, wd_ref, o_ref):
    pre = sel_ref[1] * 4
    post = pre + 1
    pieces = [slice(lo, lo + FFN_SUB) for lo in range(0, FFN_TILE, FFN_SUB)]
    chunks = [slice(c * FF_CHUNK, (c + 1) * FF_CHUNK) for c in range(D_FF // FF_CHUNK)]
    xs = [x_ref[rows, :] for rows in pieces]
    hs = [_rms(x, g_ref[pl.ds(pre, 1), :]).astype(BF16) for x in xs]
    accs = []
    for h in hs:
        acc = jnp.zeros((FFN_SUB, D_MODEL), F32)
        for cols in chunks:
            act = (_silu(_dot(h, wg_ref[:, cols])) * _dot(h, wu_ref[:, cols])).astype(BF16)
            acc = acc + _dot(act, wd_ref[cols, :])
        accs.append(acc)
    for rows, x, acc in zip(pieces, xs, accs):
        o_ref[rows, :] = x + FFN_RESIDUAL * _rms(acc, g_ref[pl.ds(post, 1), :])


def _ffn(x, norm_gains, weights, layer, which):
    row = lambda i, sel: (i, 0)
    whole = lambda i, sel: (0, 0)
    resident = pl.Buffered(1)
    grid_spec = pltpu.PrefetchScalarGridSpec(
        num_scalar_prefetch=1,
        grid=(N_ROWS // FFN_TILE,),
        in_specs=[
            pl.BlockSpec((FFN_TILE, D_MODEL), row),
            pl.BlockSpec((None, 6, D_MODEL), lambda i, sel: (sel[0], 0, 0)),
            pl.BlockSpec((D_MODEL, D_FF), whole, pipeline_mode=resident),
            pl.BlockSpec((D_MODEL, D_FF), whole, pipeline_mode=resident),
            pl.BlockSpec((D_FF, D_MODEL), whole, pipeline_mode=resident),
        ],
        out_specs=pl.BlockSpec((FFN_TILE, D_MODEL), row),
    )
    return pl.pallas_call(
        _ffn_body,
        out_shape=jax.ShapeDtypeStruct((N_ROWS, D_MODEL), F32),
        grid_spec=grid_spec,
        input_output_aliases={1: 0},
        compiler_params=_params("parallel"),
        name="ffn",
    )(jnp.array([layer, which], jnp.int32), x, norm_gains, *weights)


def _cast_plan(targets, step_of):
    slab = lambda *ids: jnp.minimum(step_of(*ids), CAST_STEPS - 1)
    in_specs, out_specs, shapes = [], [], []
    for layer, which in targets:
        for block, shape, place in (((D_MODEL, CAST_SLAB), (D_MODEL, D_FF), lambda s: (0, s)),
                                    ((D_MODEL, CAST_SLAB), (D_MODEL, D_FF), lambda s: (0, s)),
                                    ((CAST_SLAB, D_MODEL), (D_FF, D_MODEL), lambda s: (s, 0))):
            in_specs.append(pl.BlockSpec((None, None) + block,
                                         lambda *ids, l=layer, w=which, p=place: (l, w) + p(slab(*ids))))
            out_specs.append(pl.BlockSpec(block, lambda *ids, p=place: p(slab(*ids))))
            shapes.append(jax.ShapeDtypeStruct(shape, BF16))
    return in_specs, out_specs, shapes


def _cast_slabs(step, srcs, dsts):
    @pl.when(step < CAST_STEPS)
    def _():
        for src, dst in zip(srcs, dsts):
            dst[...] = src[...].astype(BF16)


def _lower_bound(lbl_ref, j):
    rows = [lbl_ref[i:i + 1, :] for i in range(N_EVEN)]
    top = functools.reduce(jnp.maximum, rows)
    e = [jnp.exp(r - top) for r in rows]
    total = functools.reduce(lambda a, b: a + b, e)
    p = [v / total for v in e]
    cum = functools.reduce(lambda a, b: a + b, p[:j + 1])
    return jnp.maximum(cum - p[0], 0.0)


def _span_cumsum(g, row0):
    out = []
    for lo in range(0, g.shape[0], CUMSUM_ROWS):
        part = g[lo:lo + CUMSUM_ROWS, :]
        t = row0 + lo + lax.broadcasted_iota(jnp.int32, (CUMSUM_ROWS, CUMSUM_ROWS), 0)
        u = row0 + lo + lax.broadcasted_iota(jnp.int32, (CUMSUM_ROWS, CUMSUM_ROWS), 1)
        span = jnp.where(t >= P_ROWS, DEC_SEQ, HGRN_ROWS)
        tri = jnp.where(jnp.logical_and((t ^ u) < span, u <= t), 1.0, 0.0).astype(BF16)
        total = None
        for _ in range(3):
            term = part.astype(BF16)
            part = part - term.astype(F32)
            total = _dot(tri, term) if total is None else total + _dot(tri, term)
        out.append(total)
    return jnp.concatenate(out, axis=0)


def _even_in_body(x_ref, g_ref, w_ref, lbl_ref, q_ref, k_ref, v_ref, kk_ref, vv_ref,
                  hq_ref, hk_ref, hb_ref, hv_ref, hg_ref, *, j):
    h = _rms(x_ref[...], g_ref[2:3, :]).astype(BF16)
    widths = (A_Q_W, 2 * A_KV_W, B_QK_W, B_QK_W, B_V_W, B_V_W)
    starts = [sum(widths[:i]) for i in range(len(widths))]
    proj = lambda i: _dot(h, w_ref[:, starts[i]:starts[i] + widths[i]])
    fx = proj(3)
    q_ref[...] = (proj(0) * (A_HEAD_DIM ** -0.5)).astype(BF16)
    kv = proj(1)
    k_ref[...] = kv[:, :A_KV_W]
    v_ref[...] = kv[:, A_KV_W:]
    kk_ref[...] = jnp.concatenate(_half_lane_variants(kv[:, :A_KV_W]), axis=1)
    vv_ref[...] = jnp.concatenate(_half_lane_variants(kv[:, A_KV_W:]), axis=1)
    hq_ref[...] = _silu(proj(2))
    hv_ref[...] = proj(4).astype(BF16)
    hg_ref[...] = _silu(proj(5))
    lb = _lower_bound(lbl_ref, j)
    soft = jnp.log1p(jnp.exp(-jnp.abs(fx)))
    ls_pos = jnp.minimum(fx, 0.0) - soft
    ls_neg = jnp.minimum(-fx, 0.0) - soft
    other = jnp.log(jnp.maximum(lb, LB_FLOOR)) + ls_neg
    log_f = jnp.maximum(ls_pos, other) + jnp.log1p(jnp.exp(-jnp.abs(ls_pos - other)))
    hk_ref[...] = (1.0 - lb) * jax.nn.sigmoid(-fx)
    hb_ref[...] = _span_cumsum(log_f, pl.program_id(0) * ROW_TILE)


def _even_in(x, gains, w_in, lb_logits, j):
    row = lambda i: (i, 0)
    whole = lambda i: (0, 0)
    wide = lambda w, dt: jax.ShapeDtypeStruct((N_ROWS, w), dt)
    spec = lambda w: pl.BlockSpec((ROW_TILE, w), row)
    return pl.pallas_call(
        functools.partial(_even_in_body, j=j),
        out_shape=(wide(A_Q_W, BF16), wide(A_KV_W, F32), wide(A_KV_W, F32),
                   wide(4 * A_KV_W, BF16), wide(4 * A_KV_W, BF16),
                   wide(B_QK_W, F32), wide(B_QK_W, F32), wide(B_QK_W, F32),
                   wide(B_V_W, BF16), wide(B_V_W, F32)),
        grid=(N_ROWS // ROW_TILE,),
        in_specs=[
            pl.BlockSpec((ROW_TILE, D_MODEL), row),
            pl.BlockSpec((6, D_MODEL), whole),
            pl.BlockSpec((None, D_MODEL, EVEN_IN_W), lambda i: (j, 0, 0),
                         pipeline_mode=pl.Buffered(1)),
            pl.BlockSpec((N_EVEN, B_QK_W), whole),
        ],
        out_specs=(spec(A_Q_W), spec(A_KV_W), spec(A_KV_W), spec(4 * A_KV_W), spec(4 * A_KV_W),
                   spec(B_QK_W), spec(B_QK_W), spec(B_QK_W), spec(B_V_W), spec(B_V_W)),
        compiler_params=_params("parallel"),
        name=f"even_in_{j}",
    )(x, gains, w_in, lb_logits)


def _half_lane_variants(a):
    lane = lax.broadcasted_iota(jnp.int32, a.shape, 1)
    low = lane < A_HEAD_DIM
    swapped = pltpu.roll(a, A_HEAD_DIM, 1)
    zero = jnp.zeros_like(a)
    return (jnp.where(low, a, zero).astype(BF16), jnp.where(low, zero, swapped).astype(BF16),
            jnp.where(low, swapped, zero).astype(BF16), jnp.where(low, zero, a).astype(BF16))


def _sink_softmax(s, valid, sink):
    if valid is not None:
        s = jnp.where(valid, s, MASK_VALUE)
    m = jnp.maximum(jnp.max(s, axis=-1, keepdims=True), sink)
    p = jnp.exp(s - m)
    denom = jnp.sum(p, axis=-1, keepdims=True) + jnp.exp(sink - m)
    return (p / denom).astype(BF16)


def _group_sinks(sink_ref, kvh, rows):
    first = lax.broadcasted_iota(jnp.int32, (2 * rows, 1), 0) < rows
    base = kvh * (A_HEADS // A_KV_HEADS)
    return (jnp.where(first, sink_ref[base], sink_ref[base + 2]),
            jnp.where(first, sink_ref[base + 1], sink_ref[base + 3]))


def _attend(items, sinks):
    scores = []
    for q, keys, _, _ in items:
        for kvh in range(A_KV_HEADS):
            c0 = kvh * 256
            q2 = jnp.concatenate([q[:, c0:c0 + 128], q[:, c0 + 128:c0 + 256]], axis=0)
            scores += [_dot_nt(q2, keys[2 * kvh]), _dot_nt(q2, keys[2 * kvh + 1])]
    weights = []
    for n, (_, _, _, valid) in enumerate(items):
        for kvh in range(A_KV_HEADS):
            for half in range(2):
                weights.append(_sink_softmax(scores[4 * n + 2 * kvh + half], valid, sinks[kvh][half]))
    outs = []
    for n, (q, _, vals, _) in enumerate(items):
        rows = q.shape[0]
        groups = []
        for kvh in range(A_KV_HEADS):
            w = weights[4 * n + 2 * kvh:4 * n + 2 * kvh + 2]
            o = _dot(w[0], vals[2 * kvh]) + _dot(w[1], vals[2 * kvh + 1])
            groups.append(jnp.concatenate([o[:rows], o[rows:]], axis=1))
        outs.append(jnp.concatenate(groups, axis=1))
    return outs


def _attn_body(sink_ref, q_ref, kk_ref, vv_ref, ck_ref, cv_ref, o_ref):
    variant = [slice(v * A_KV_W, (v + 1) * A_KV_W) for v in range(4)]

    @pl.when(pl.program_id(0) < BATCH)
    def _():
        _attn_prompt_stream(sink_ref, q_ref, kk_ref, vv_ref, o_ref, variant)

    @pl.when(pl.program_id(0) == BATCH)
    def _():
        sinks = [_group_sinks(sink_ref, kvh, DEC_SEQ) for kvh in range(A_KV_HEADS)]
        items = []
        for b in range(DEC_BATCH):
            rows = slice(b * DEC_SEQ, (b + 1) * DEC_SEQ)
            old_k, old_v = _half_lane_variants(ck_ref[b]), _half_lane_variants(cv_ref[b])
            keys = [jnp.concatenate([old_k[v], kk_ref[rows, variant[v]]], axis=0) for v in range(4)]
            vals = [jnp.concatenate([old_v[v], vv_ref[rows, variant[v]]], axis=0) for v in range(4)]
            items.append((q_ref[rows, :], keys, vals, None))
        for b, out in enumerate(_attend(items, sinks)):
            o_ref[b * DEC_SEQ:(b + 1) * DEC_SEQ, :] = out.astype(BF16)


def _attn_prompt_stream(sink_ref, q_ref, kk_ref, vv_ref, o_ref, variant):
    span = (WINDOW_CHUNKS + 1) * CHUNK
    sinks = [_group_sinks(sink_ref, kvh, CHUNK) for kvh in range(A_KV_HEADS)]

    def step(n, carry):
        starts, items = [], []
        for i in range(ATTN_CHUNKS):
            c = n * ATTN_CHUNKS + i
            r0 = pl.multiple_of(c * CHUNK, CHUNK)
            s0 = pl.multiple_of(jnp.maximum(c - WINDOW_CHUNKS, 0) * CHUNK, CHUNK)
            kpos = s0 + lax.broadcasted_iota(jnp.int32, (1, span), 1)
            valid = jnp.logical_and(kpos >= PAD, kpos < r0 + CHUNK)
            starts.append(r0)
            items.append((q_ref[pl.ds(r0, CHUNK), :],
                          [kk_ref[pl.ds(s0, span), v] for v in variant],
                          [vv_ref[pl.ds(s0, span), v] for v in variant], valid))
        for r0, out in zip(starts, _attend(items, sinks)):
            o_ref[pl.ds(r0, CHUNK), :] = out.astype(BF16)
        return carry

    lax.fori_loop(0, N_CHUNKS // ATTN_CHUNKS, step, 0)


def _attn(sink, q, kk, vv, cache_k, cache_v):
    seq = lambda b: (b, 0)
    old = pl.BlockSpec(cache_k.shape, lambda b: (0, 0, 0))
    return pl.pallas_call(
        _attn_body,
        out_shape=jax.ShapeDtypeStruct((N_ROWS, A_Q_W), BF16),
        grid=(BATCH + 1,),
        in_specs=[
            pl.BlockSpec(memory_space=pltpu.SMEM),
            pl.BlockSpec((LP, A_Q_W), seq),
            pl.BlockSpec((LP, 4 * A_KV_W), seq),
            pl.BlockSpec((LP, 4 * A_KV_W), seq),
            old, old,
        ],
        out_specs=pl.BlockSpec((LP, A_Q_W), seq),
        compiler_params=_params("parallel"),
        name="attn",
    )(sink, q, kk, vv, cache_k, cache_v)


def _group_row(a, group, row):
    spans = a.shape[0] // group
    picked = a.reshape(spans, group, a.shape[1])[:, row:row + 1, :]
    return jnp.broadcast_to(picked, (spans, group, a.shape[1])).reshape(a.shape)


def _hgrn_pairs(q, k, c2):
    parts = []
    for s in range(HGRN_SUB):
        decay = jnp.exp2(jnp.minimum(c2 - _group_row(c2, HGRN_SUB, s), 0.0))
        parts.append((q * _group_row(k, HGRN_SUB, s) * decay).astype(BF16))
    return jnp.concatenate(parts, axis=1)


def _hgrn_pick(rows):
    shape = (HGRN_SUB * B_KEY_DIM, rows)
    block = lax.broadcasted_iota(jnp.int32, shape, 0) // B_KEY_DIM
    col = lax.broadcasted_iota(jnp.int32, shape, 1) & (HGRN_SUB - 1)
    return jnp.where(block == col, 1.0, 0.0).astype(BF16)


def _hgrn_head(q, c, v, state_t, intra, levels, update):
    rows = q.shape[0]
    t_idx = lax.broadcasted_iota(jnp.int32, (rows, rows), 0)
    u_idx = lax.broadcasted_iota(jnp.int32, (rows, rows), 1)
    apart = t_idx ^ u_idx
    in_group = jnp.logical_and(apart < HGRN_SUB, u_idx <= t_idx)
    scores = jnp.where(in_group, intra, 0.0)
    for w, level in levels:
        scores = scores + (level if 2 * w == rows else jnp.where(apart < 2 * w, level, 0.0))
    o = _dot(scores.astype(BF16), v) + _dot_nt((q * jnp.exp2(c)).astype(BF16), state_t.astype(BF16))
    return o, state_t * jnp.exp2(c[rows - 1:rows, :]) + update


def _hgrn_levels(q, k, c):
    rows = q.shape[0]
    pos = lax.broadcasted_iota(jnp.int32, q.shape, 0)
    levels = []
    w = HGRN_SUB
    while w < rows:
        rho = _group_row(c, 2 * w, w - 1)
        upper = (pos & w) != 0
        q_up = jnp.where(upper, q * jnp.exp2(jnp.minimum(c - rho, 0.0)), 0.0).astype(BF16)
        k_low = jnp.where(upper, 0.0, k * jnp.exp2(jnp.minimum(rho - c, 0.0))).astype(BF16)
        levels.append((w, _dot_nt(q_up, k_low)))
        w *= 2
    return levels


def _hgrn_tiles(tiles, n, hq_ref, hk_ref, hc_ref, hv_ref, hg_ref, gain_ref, o_ref, st_ref, pick):
    loaded = [(hq_ref[r, :], hk_ref[r, :], hc_ref[r, :] * LOG2_E, hv_ref[r, :], hg_ref[r, :])
              for r in tiles]
    states = [st_ref[hd] for hd in range(B_HEADS)]
    heads = [slice(hd * B_KEY_DIM, (hd + 1) * B_KEY_DIM) for hd in range(B_HEADS)]
    intras = [_dot(jnp.concatenate([_hgrn_pairs(q[:, h], k[:, h], c[:, h]) for h in heads], axis=0), pick)
              for q, k, c, _, _ in loaded]
    levels = [[_hgrn_levels(q[:, h], k[:, h], c[:, h]) for h in heads] for q, k, c, _, _ in loaded]
    updates = [[_dot_tn(v[:, h], (k[:, h] * jnp.exp2(c[n - 1:n, h] - c[:, h])).astype(BF16))
                for h in heads] for _, k, c, v, _ in loaded]
    results = []
    for t, (q, k, c, v, g) in enumerate(loaded):
        outs = []
        for hd, h in enumerate(heads):
            o, states[hd] = _hgrn_head(q[:, h], c[:, h], v[:, h], states[hd],
                                       intras[t][hd * n:(hd + 1) * n, :], levels[t][hd], updates[t][hd])
            outs.append(_rms(o, gain_ref[hd:hd + 1, :]) * g[:, h])
        results.append(jnp.concatenate(outs, axis=1).astype(BF16))
    for r, res in zip(tiles, results):
        o_ref[r, :] = res
    for hd in range(B_HEADS):
        st_ref[hd] = states[hd]


def _hgrn_body(*refs, n_cast):
    hq_ref, hk_ref, hc_ref, hv_ref, hg_ref, gain_ref, s0_ref = refs[:7]
    o_ref, sp_ref, ss_ref = refs[7 + n_cast:10 + n_cast]
    st_ref = refs[-1]
    step = pl.program_id(0)
    tiles = LP // HGRN_TILE
    tile = lax.rem(step, tiles)
    is_prompt = step < BATCH * tiles
    _cast_slabs(step, refs[7:7 + n_cast], refs[10 + n_cast:10 + 2 * n_cast])
    mixer = (hq_ref, hk_ref, hc_ref, hv_ref, hg_ref, gain_ref, o_ref, st_ref)

    @pl.when(jnp.logical_and(is_prompt, tile == 0))
    def _():
        st_ref[...] = jnp.zeros(st_ref.shape, F32)

    @pl.when(is_prompt)
    def _():
        pick = _hgrn_pick(HGRN_ROWS)
        n_tiles = HGRN_TILE // HGRN_ROWS

        def group(n, carry):
            first = n * HGRN_GROUP
            spans = [pl.ds(pl.multiple_of((first + i) * HGRN_ROWS, HGRN_ROWS), HGRN_ROWS)
                     for i in range(HGRN_GROUP)]
            _hgrn_tiles(spans, HGRN_ROWS, *mixer, pick)
            return carry

        lax.fori_loop(0, n_tiles // HGRN_GROUP, group, 0)
        rest = [pl.ds(i * HGRN_ROWS, HGRN_ROWS) for i in range(n_tiles - n_tiles % HGRN_GROUP, n_tiles)]
        if rest:
            _hgrn_tiles(rest, HGRN_ROWS, *mixer, pick)

    @pl.when(jnp.logical_and(is_prompt, tile == tiles - 1))
    def _():
        for hd in range(B_HEADS):
            sp_ref[hd] = st_ref[hd].T

    @pl.when(jnp.logical_not(is_prompt))
    def _():
        pick = _hgrn_pick(DEC_SEQ)
        for b in range(DEC_BATCH):
            for hd in range(B_HEADS):
                st_ref[hd] = s0_ref[b, hd].T
            _hgrn_tiles([pl.ds(b * DEC_SEQ, DEC_SEQ)], DEC_SEQ, *mixer, pick)
            for hd in range(B_HEADS):
                ss_ref[b, hd] = st_ref[hd].T


def _hgrn(hq, hk, hb, hv, hg, gain, s0, ffn_f32, cast_targets):
    tiles = LP // HGRN_TILE
    row = lambda s: (s, 0)
    spec = pl.BlockSpec((HGRN_TILE, B_QK_W), row)
    state = (B_HEADS, B_KEY_DIM, B_VAL_DIM)
    cast_in, cast_out, cast_shapes = _cast_plan(cast_targets, lambda s: s)
    out = pl.pallas_call(
        functools.partial(_hgrn_body, n_cast=len(cast_in)),
        out_shape=(jax.ShapeDtypeStruct((N_ROWS, B_V_W), BF16),
                   jax.ShapeDtypeStruct((BATCH,) + state, F32),
                   jax.ShapeDtypeStruct((DEC_BATCH,) + state, F32), *cast_shapes),
        grid=(BATCH * tiles + 1,),
        in_specs=[spec, spec, spec, spec, spec,
                  pl.BlockSpec((B_HEADS, B_VAL_DIM), lambda s: (0, 0)),
                  pl.BlockSpec((DEC_BATCH,) + state, lambda s: (0, 0, 0, 0)), *cast_in],
        out_specs=(spec,
                   pl.BlockSpec((None,) + state, lambda s: (jnp.minimum(s // tiles, BATCH - 1), 0, 0, 0)),
                   pl.BlockSpec((DEC_BATCH,) + state, lambda s: (0, 0, 0, 0)), *cast_out),
        scratch_shapes=[pltpu.VMEM((B_HEADS, B_VAL_DIM, B_KEY_DIM), F32)],
        compiler_params=_params("arbitrary"),
        name="hgrn",
    )(hq, hk, hb, hv, hg, gain, s0, *(ffn_f32 * len(cast_targets)))
    return out[0], out[1], out[2], out[3:]


def _even_out_body(x_ref, g_ref, oa_ref, ob_ref, w_ref, o_ref):
    for lo in range(0, FFN_TILE, FFN_SUB):
        rows = slice(lo, lo + FFN_SUB)
        m = _dot(oa_ref[rows, :], w_ref[:A_Q_W, :]) + _dot(ob_ref[rows, :], w_ref[A_Q_W:, :])
        y = x_ref[rows, :] + _rms(m, g_ref[3:4, :])
        pad = _is_pad_row(pl.program_id(0) * FFN_TILE + lo, FFN_SUB)
        o_ref[rows, :] = jnp.where(pad, 0.0, y)


def _even_out(x, gains, oa, ob, w_out, j):
    row = lambda i: (i, 0)
    return pl.pallas_call(
        _even_out_body,
        out_shape=jax.ShapeDtypeStruct((N_ROWS, D_MODEL), F32),
        grid=(N_ROWS // FFN_TILE,),
        in_specs=[
            pl.BlockSpec((FFN_TILE, D_MODEL), row),
            pl.BlockSpec((6, D_MODEL), lambda i: (0, 0)),
            pl.BlockSpec((FFN_TILE, A_Q_W), row),
            pl.BlockSpec((FFN_TILE, B_V_W), row),
            pl.BlockSpec((None, A_Q_W + B_V_W, D_MODEL), lambda i: (j, 0, 0),
                         pipeline_mode=pl.Buffered(1)),
        ],
        out_specs=pl.BlockSpec((FFN_TILE, D_MODEL), row),
        input_output_aliases={0: 0},
        compiler_params=_params("parallel"),
        name=f"even_out_{j}",
    )(x, gains, oa, ob, w_out)


def _conv3(u, cw_ref, before1=None, before2=None):
    shift1, shift2 = pltpu.roll(u, 1, 0), pltpu.roll(u, 2, 0)
    if before1 is not None:
        pos = lax.broadcasted_iota(jnp.int32, (u.shape[0], 1), 0)
        shift1 = jnp.where(pos == 0, before1, shift1)
        shift2 = jnp.where(pos == 0, before2, jnp.where(pos == 1, before1, shift2))
    return shift2 * cw_ref[0:1, :] + shift1 * cw_ref[1:2, :] + u * cw_ref[2:3, :]


def _odd_body(*refs, n_cast):
    x_ref, halo_ref, g_ref, w_in_ref, cw_ref, w_out_ref, cache_ref = refs[:7]
    o_ref, tail_ref, new_cache_ref = refs[7 + n_cast:10 + n_cast]
    _cast_slabs(pl.program_id(0), refs[7:7 + n_cast], refs[10 + n_cast:])
    is_prompt = pl.program_id(0) < P_ROWS // ODD_TILE

    @pl.when(is_prompt)
    def _():
        x = x_ref[...]
        h = _rms(x, g_ref[2:3, :]).astype(BF16)
        h_ext = jnp.concatenate([_rms(halo_ref[...], g_ref[2:3, :]).astype(BF16), h], axis=0)
        bg = _dot(h, w_in_ref[:, :D_MODEL])
        u_ext = _dot(h_ext, w_in_ref[:, D_MODEL:2 * D_MODEL]) * _dot(h_ext, w_in_ref[:, 2 * D_MODEL:])
        tail_ref[...] = u_ext[ODD_HALO + ODD_TILE - TAIL_ROWS:, :]
        y = _conv3(u_ext, cw_ref)[ODD_HALO:, :]
        m = _dot((bg * y).astype(BF16), w_out_ref[...])
        out = x + _rms(m, g_ref[3:4, :])
        o_ref[...] = jnp.where(_is_pad_row(pl.program_id(0) * ODD_TILE, ODD_TILE), 0.0, out)

    @pl.when(jnp.logical_not(is_prompt))
    def _():
        x = x_ref[:S_ROWS, :]
        h = _rms(x, g_ref[2:3, :]).astype(BF16)
        bg = _dot(h, w_in_ref[:, :D_MODEL])
        u = _dot(h, w_in_ref[:, D_MODEL:2 * D_MODEL]) * _dot(h, w_in_ref[:, 2 * D_MODEL:])
        ys = []
        for b in range(DEC_BATCH):
            ub = u[b * DEC_SEQ:(b + 1) * DEC_SEQ, :]
            ys.append(_conv3(ub, cw_ref, before1=cache_ref[b, 1:2, :], before2=cache_ref[b, 0:1, :]))
            new_cache_ref[b] = ub[DEC_SEQ - (CONV_WIDTH - 1):, :]
        m = _dot((bg * jnp.concatenate(ys, axis=0)).astype(BF16), w_out_ref[...])
        o_ref[:S_ROWS, :] = x + _rms(m, g_ref[3:4, :])
        tail_ref[...] = jnp.zeros(tail_ref.shape, F32)


def _odd(x, gains, w_in, conv_w, w_out, cache, j, ffn_f32, cast_targets):
    tiles = pl.cdiv(N_ROWS, ODD_TILE)
    row = lambda i: (i, 0)
    per = ODD_TILE // ODD_HALO
    whole3 = lambda i: (0, 0, 0)
    halo = lambda i: (jnp.maximum(i * per - 1, 0), 0)
    cast_in, cast_out, cast_shapes = _cast_plan(cast_targets, lambda i: i)
    out = pl.pallas_call(
        functools.partial(_odd_body, n_cast=len(cast_in)),
        out_shape=(jax.ShapeDtypeStruct((N_ROWS, D_MODEL), F32),
                   jax.ShapeDtypeStruct((tiles, TAIL_ROWS, D_MODEL), F32),
                   jax.ShapeDtypeStruct(cache.shape, F32), *cast_shapes),
        grid=(tiles,),
        in_specs=[
            pl.BlockSpec((ODD_TILE, D_MODEL), row),
            pl.BlockSpec((ODD_HALO, D_MODEL), halo),
            pl.BlockSpec((6, D_MODEL), lambda i: (0, 0)),
            pl.BlockSpec((None, D_MODEL, 3 * D_MODEL), lambda i: (j, 0, 0), pipeline_mode=pl.Buffered(1)),
            pl.BlockSpec((None, CONV_WIDTH, D_MODEL), lambda i: (j, 0, 0)),
            pl.BlockSpec((None, D_MODEL, D_MODEL), lambda i: (j, 0, 0), pipeline_mode=pl.Buffered(1)),
            pl.BlockSpec(cache.shape, whole3),
            *cast_in,
        ],
        out_specs=(pl.BlockSpec((ODD_TILE, D_MODEL), row),
                   pl.BlockSpec((None, TAIL_ROWS, D_MODEL), lambda i: (i, 0, 0)),
                   pl.BlockSpec(cache.shape, whole3), *cast_out),
        compiler_params=_params("arbitrary"),
        name=f"odd_{j}",
    )(x, x, gains, w_in, conv_w, w_out, cache, *(ffn_f32 * len(cast_targets)))
    return out[0], out[1], out[2], out[3:]


def _stream_tails(a, row0, streams, length, n):
    ends = [row0 + (s + 1) * length for s in range(streams)]
    return jnp.stack([lax.slice_in_dim(a, e - n, e, axis=0) for e in ends])


def kernel(x_prompt, x_sample, cache_swa_k, cache_swa_v, state_hgrn, cache_conv, meta_tokens, norm_gains,
           w_ffn_gate, w_ffn_up, w_ffn_down, w_in_even, w_out_even, attn_sinks, hgrn_lb_logits,
           hgrn_norm_gain, w_in_odd, conv_w, w_out_odd):
    win = cache_swa_k.shape[2]
    x = jnp.pad(x_prompt, ((0, 0), (PAD + N_META, 0), (0, 0))).reshape(P_ROWS, D_MODEL)
    x = jnp.pad(x, ((0, S_ROWS), (0, 0)))
    x = lax.dynamic_update_slice(x, x_sample.reshape(S_ROWS, D_MODEL), (P_ROWS, 0))
    for b in range(BATCH):
        x = lax.dynamic_update_slice(x, meta_tokens.astype(F32), (b * LP + PAD, 0))

    ffn_f32 = (w_ffn_gate, w_ffn_up, w_ffn_down)
    ffn_w = {(0, 0): (w_ffn_gate[0, 0].astype(BF16), w_ffn_up[0, 0].astype(BF16),
                      w_ffn_down[0, 0].astype(BF16))}

    def keep_converted(targets, converted):
        for n, key in enumerate(targets):
            ffn_w[key] = tuple(converted[3 * n:3 * n + 3])

    w_in_e, w_out_e = w_in_even.astype(BF16), w_out_even.astype(BF16)
    w_in_o, w_out_o = w_in_odd.astype(BF16), w_out_odd.astype(BF16)
    cache_k = cache_swa_k.reshape(N_EVEN, DEC_BATCH, win, A_KV_W)
    cache_v = cache_swa_v.reshape(N_EVEN, DEC_BATCH, win, A_KV_W)

    k_p, v_p, s_p, c_p, k_s, v_s, s_s, c_s = ([] for _ in range(8))
    for layer in range(DEPTH):
        gains = norm_gains[layer]
        j = layer // 2
        x = _ffn(x, norm_gains, ffn_w[(layer, 0)], layer, 0)
        targets = [(layer, 1)] + ([(layer + 1, 0)] if layer + 1 < DEPTH else [])
        if layer % 2 == 0:
            q, k, v, kk, vv, hq, hk, hb, hv, hg = _even_in(x, gains, w_in_e, hgrn_lb_logits, j)
            oa = _attn(attn_sinks[j], q, kk, vv, cache_k[j], cache_v[j])
            ob, st_p, st_s, converted = _hgrn(hq, hk, hb, hv, hg, hgrn_norm_gain[j], state_hgrn[j],
                                              ffn_f32, targets)
            keep_converted(targets, converted)
            x = _even_out(x, gains, oa, ob, w_out_e, j)
            for new, tail_p, tail_s, cache in ((k, k_p, k_s, cache_k[j]), (v, v_p, v_s, cache_v[j])):
                tail_p.append(_stream_tails(new, 0, BATCH, LP, WINDOW).reshape(
                    BATCH, WINDOW, A_KV_HEADS, A_HEAD_DIM))
                both = jnp.concatenate([cache, new[P_ROWS:].reshape(DEC_BATCH, DEC_SEQ, A_KV_W)], axis=1)
                tail_s.append(both[:, both.shape[1] - win:].reshape(DEC_BATCH, win, A_KV_HEADS, A_HEAD_DIM))
            s_p.append(st_p)
            s_s.append(st_s)
        else:
            x, tails, conv_s, converted = _odd(x, gains, w_in_o, conv_w, w_out_o, cache_conv[j], j,
                                               ffn_f32, targets)
            keep_converted(targets, converted)
            per_stream = LP // ODD_TILE
            c_p.append(jnp.stack([tails[(b + 1) * per_stream - 1, TAIL_ROWS - (CONV_WIDTH - 1):]
                                  for b in range(BATCH)]))
            c_s.append(conv_s)
        x = _ffn(x, norm_gains, ffn_w[(layer, 1)], layer, 1)

    y_prompt = _stream_tails(x, 0, BATCH, LP, SEQ)
    y_sample = x[P_ROWS:].reshape(DEC_BATCH, DEC_SEQ, D_MODEL)
    return (y_prompt, y_sample, jnp.stack(k_p), jnp.stack(v_p), jnp.stack(s_p), jnp.stack(c_p),
            jnp.stack(k_s), jnp.stack(v_s), jnp.stack(s_s), jnp.stack(c_s))
```

```python
import functools

import jax
import jax.numpy as jnp
from jax import lax
from jax.experimental import pallas as pl
from jax.experimental.pallas import tpu as pltpu

F32 = jnp.float32
BF16 = jnp.bfloat16

D_MODEL = 1024
BATCH = 4
SEQ = 4096
DEPTH = 4
DEC_BATCH = 8
DEC_SEQ = 32
CHUNK = 64
N_META = 16
WINDOW = 128
WINDOW_CHUNKS = WINDOW // CHUNK
A_HEADS = 8
A_KV_HEADS = 2
A_HEAD_DIM = 64
A_Q_W = A_HEADS * A_HEAD_DIM
A_KV_W = A_KV_HEADS * A_HEAD_DIM
B_HEADS = 4
B_KEY_DIM = 128
B_VAL_DIM = 128
B_QK_W = B_HEADS * B_KEY_DIM
B_V_W = B_HEADS * B_VAL_DIM
B_BLOCK = 16
CONV_WIDTH = 3
D_FF = 2816
FFN_RESIDUAL = 0.5
N_EVEN = (DEPTH + 1) // 2
N_ODD = DEPTH // 2
EVEN_IN_W = A_Q_W + 2 * A_KV_W + 2 * B_QK_W + 2 * B_V_W
EPS = 1e-6
MASK_VALUE = -1e30
LB_FLOOR = 1e-30

PAD = (-N_META) % CHUNK
LP = PAD + N_META + SEQ
N_CHUNKS = LP // CHUNK
P_ROWS = BATCH * LP
S_ROWS = DEC_BATCH * DEC_SEQ
N_ROWS = P_ROWS + S_ROWS

ROW_TILE = 512
FF_CHUNK = 256
HGRN_TILE = 832
HGRN_ROWS = 64
HGRN_SUB = 8
HGRN_GROUP = 3
LOG2_E = 1.4426950408889634
CUMSUM_ROWS = 256
FFN_TILE = 1536
FFN_SUB = 512
ATTN_CHUNKS = 5
PLACE_ROWS = 1024
CAST_SLAB = 256
CAST_STEPS = D_FF // CAST_SLAB
ODD_TILE = 832
ODD_HALO = 16
TAIL_ROWS = 8
VMEM_LIMIT = 56 * 1024 * 1024

assert N_ROWS % ROW_TILE == 0 and D_FF % FF_CHUNK == 0
assert LP % HGRN_TILE == 0 and HGRN_TILE % HGRN_ROWS == 0 and ROW_TILE % HGRN_ROWS == 0
assert HGRN_ROWS % DEC_SEQ == 0 and DEC_SEQ % HGRN_SUB == 0
assert ROW_TILE % CUMSUM_ROWS == 0 and CUMSUM_ROWS % HGRN_ROWS == 0
assert N_ROWS % FFN_TILE == 0 and FFN_TILE % FFN_SUB == 0 and N_CHUNKS % ATTN_CHUNKS == 0
assert D_FF % CAST_SLAB == 0 and CAST_STEPS <= min(P_ROWS // ODD_TILE, P_ROWS // HGRN_TILE)
assert LP % ODD_TILE == 0 and ODD_TILE % ODD_HALO == 0 and P_ROWS % DEC_SEQ == 0
assert P_ROWS % S_ROWS == 0 and S_ROWS <= ODD_TILE and TAIL_ROWS >= CONV_WIDTH - 1


def _params(*sem):
    return pltpu.CompilerParams(dimension_semantics=sem, vmem_limit_bytes=VMEM_LIMIT)


def _rms(x, gain):
    return x * lax.rsqrt(jnp.mean(x * x, axis=-1, keepdims=True) + EPS) * gain


def _silu(x):
    return x * jax.nn.sigmoid(x)


def _dot(a, b):
    return jnp.dot(a, b, preferred_element_type=F32)


def _dot_nt(a, b):
    return lax.dot_general(a, b, (((1,), (1,)), ((), ())), preferred_element_type=F32)


def _dot_tn(a, b):
    return lax.dot_general(a, b, (((0,), (0,)), ((), ())), preferred_element_type=F32)


def _is_pad_row(row0, rows):
    r = row0 + lax.broadcasted_iota(jnp.int32, (rows, 1), 0)
    pad = None
    for b in range(BATCH):
        hit = jnp.logical_and(r >= b * LP, r < b * LP + PAD)
        pad = hit if pad is None else jnp.logical_or(pad, hit)
    return pad


def _ffn_body(sel_ref, x_ref, g_ref, wg_ref, wu_ref, wd_ref, o_ref):
    pre = sel_ref[1] * 4
    post = pre + 1
    pieces = [slice(lo, lo + FFN_SUB) for lo in range(0, FFN_TILE, FFN_SUB)]
    chunks = [slice(c * FF_CHUNK, (c + 1) * FF_CHUNK) for c in range(D_FF // FF_CHUNK)]
    xs = [x_ref[rows, :] for rows in pieces]
    hs = [_rms(x, g_ref[pl.ds(pre, 1), :]).astype(BF16) for x in xs]
    accs = []
    for h in hs:
        acc = jnp.zeros((FFN_SUB, D_MODEL), F32)
        for cols in chunks:
            act = (_silu(_dot(h, wg_ref[:, cols])) * _dot(h, wu_ref[:, cols])).astype(BF16)
            acc = acc + _dot(act, wd_ref[cols, :])
        accs.append(acc)
    for rows, x, acc in zip(pieces, xs, accs):
        o_ref[rows, :] = x + FFN_RESIDUAL * _rms(acc, g_ref[pl.ds(post, 1), :])


def _ffn(x, norm_gains, weights, layer, which):
    row = lambda i, sel: (i, 0)
    whole = lambda i, sel: (0, 0)
    resident = pl.Buffered(1)
    grid_spec = pltpu.PrefetchScalarGridSpec(
        num_scalar_prefetch=1,
        grid=(N_ROWS // FFN_TILE,),
        in_specs=[
            pl.BlockSpec((FFN_TILE, D_MODEL), row),
            pl.BlockSpec((None, 6, D_MODEL), lambda i, sel: (sel[0], 0, 0)),
            pl.BlockSpec((D_MODEL, D_FF), whole, pipeline_mode=resident),
            pl.BlockSpec((D_MODEL, D_FF), whole, pipeline_mode=resident),
            pl.BlockSpec((D_FF, D_MODEL), whole, pipeline_mode=resident),
        ],
        out_specs=pl.BlockSpec((FFN_TILE, D_MODEL), row),
    )
    return pl.pallas_call(
        _ffn_body,
        out_shape=jax.ShapeDtypeStruct((N_ROWS, D_MODEL), F32),
        grid_spec=grid_spec,
        input_output_aliases={1: 0},
        compiler_params=_params("parallel"),
        name="ffn",
    )(jnp.array([layer, which], jnp.int32), x, norm_gains, *weights)


def _cast_plan(targets, step_of):
    slab = lambda *ids: jnp.minimum(step_of(*ids), CAST_STEPS - 1)
    in_specs, out_specs, shapes = [], [], []
    for layer, which in targets:
        for block, shape, place in (((D_MODEL, CAST_SLAB), (D_MODEL, D_FF), lambda s: (0, s)),
                                    ((D_MODEL, CAST_SLAB), (D_MODEL, D_FF), lambda s: (0, s)),
                                    ((CAST_SLAB, D_MODEL), (D_FF, D_MODEL), lambda s: (s, 0))):
            in_specs.append(pl.BlockSpec((None, None) + block,
                                         lambda *ids, l=layer, w=which, p=place: (l, w) + p(slab(*ids))))
            out_specs.append(pl.BlockSpec(block, lambda *ids, p=place: p(slab(*ids))))
            shapes.append(jax.ShapeDtypeStruct(shape, BF16))
    return in_specs, out_specs, shapes


def _cast_slabs(step, srcs, dsts):
    @pl.when(step < CAST_STEPS)
    def _():
        for src, dst in zip(srcs, dsts):
            dst[...] = src[...].astype(BF16)


def _lower_bound(lbl_ref, j):
    rows = [lbl_ref[i:i + 1, :] for i in range(N_EVEN)]
    top = functools.reduce(jnp.maximum, rows)
    e = [jnp.exp(r - top) for r in rows]
    total = functools.reduce(lambda a, b: a + b, e)
    p = [v / total for v in e]
    cum = functools.reduce(lambda a, b: a + b, p[:j + 1])
    return jnp.maximum(cum - p[0], 0.0)


def _span_cumsum(g, row0):
    out = []
    for lo in range(0, g.shape[0], CUMSUM_ROWS):
        part = g[lo:lo + CUMSUM_ROWS, :]
        t = row0 + lo + lax.broadcasted_iota(jnp.int32, (CUMSUM_ROWS, CUMSUM_ROWS), 0)
        u = row0 + lo + lax.broadcasted_iota(jnp.int32, (CUMSUM_ROWS, CUMSUM_ROWS), 1)
        span = jnp.where(t >= P_ROWS, DEC_SEQ, HGRN_ROWS)
        tri = jnp.where(jnp.logical_and((t ^ u) < span, u <= t), 1.0, 0.0).astype(BF16)
        total = None
        for _ in range(3):
            term = part.astype(BF16)
            part = part - term.astype(F32)
            total = _dot(tri, term) if total is None else total + _dot(tri, term)
        out.append(total)
    return jnp.concatenate(out, axis=0)


def _even_in_body(x_ref, g_ref, w_ref, lbl_ref, q_ref, k_ref, v_ref, kk_ref, vv_ref,
                  hq_ref, hk_ref, hb_ref, hv_ref, hg_ref, *, j):
    h = _rms(x_ref[...], g_ref[2:3, :]).astype(BF16)
    widths = (A_Q_W, 2 * A_KV_W, B_QK_W, B_QK_W, B_V_W, B_V_W)
    starts = [sum(widths[:i]) for i in range(len(widths))]
    proj = lambda i: _dot(h, w_ref[:, starts[i]:starts[i] + widths[i]])
    fx = proj(3)
    q_ref[...] = (proj(0) * (A_HEAD_DIM ** -0.5)).astype(BF16)
    kv = proj(1)
    k_ref[...] = kv[:, :A_KV_W]
    v_ref[...] = kv[:, A_KV_W:]
    kk_ref[...] = jnp.concatenate(_half_lane_variants(kv[:, :A_KV_W]), axis=1)
    vv_ref[...] = jnp.concatenate(_half_lane_variants(kv[:, A_KV_W:]), axis=1)
    hq_ref[...] = _silu(proj(2))
    hv_ref[...] = proj(4).astype(BF16)
    hg_ref[...] = _silu(proj(5))
    lb = _lower_bound(lbl_ref, j)
    soft = jnp.log1p(jnp.exp(-jnp.abs(fx)))
    ls_pos = jnp.minimum(fx, 0.0) - soft
    ls_neg = jnp.minimum(-fx, 0.0) - soft
    other = jnp.log(jnp.maximum(lb, LB_FLOOR)) + ls_neg
    log_f = jnp.maximum(ls_pos, other) + jnp.log1p(jnp.exp(-jnp.abs(ls_pos - other)))
    hk_ref[...] = (1.0 - lb) * jax.nn.sigmoid(-fx)
    hb_ref[...] = _span_cumsum(log_f, pl.program_id(0) * ROW_TILE)


def _even_in(x, gains, w_in, lb_logits, j):
    row = lambda i: (i, 0)
    whole = lambda i: (0, 0)
    wide = lambda w, dt: jax.ShapeDtypeStruct((N_ROWS, w), dt)
    spec = lambda w: pl.BlockSpec((ROW_TILE, w), row)
    return pl.pallas_call(
        functools.partial(_even_in_body, j=j),
        out_shape=(wide(A_Q_W, BF16), wide(A_KV_W, F32), wide(A_KV_W, F32),
                   wide(4 * A_KV_W, BF16), wide(4 * A_KV_W, BF16),
                   wide(B_QK_W, F32), wide(B_QK_W, F32), wide(B_QK_W, F32),
                   wide(B_V_W, BF16), wide(B_V_W, F32)),
        grid=(N_ROWS // ROW_TILE,),
        in_specs=[
            pl.BlockSpec((ROW_TILE, D_MODEL), row),
            pl.BlockSpec((6, D_MODEL), whole),
            pl.BlockSpec((None, D_MODEL, EVEN_IN_W), lambda i: (j, 0, 0),
                         pipeline_mode=pl.Buffered(1)),
            pl.BlockSpec((N_EVEN, B_QK_W), whole),
        ],
        out_specs=(spec(A_Q_W), spec(A_KV_W), spec(A_KV_W), spec(4 * A_KV_W), spec(4 * A_KV_W),
                   spec(B_QK_W), spec(B_QK_W), spec(B_QK_W), spec(B_V_W), spec(B_V_W)),
        compiler_params=_params("parallel"),
        name=f"even_in_{j}",
    )(x, gains, w_in, lb_logits)


def _half_lane_variants(a):
    lane = lax.broadcasted_iota(jnp.int32, a.shape, 1)
    low = lane < A_HEAD_DIM
    swapped = pltpu.roll(a, A_HEAD_DIM, 1)
    zero = jnp.zeros_like(a)
    return (jnp.where(low, a, zero).astype(BF16), jnp.where(low, zero, swapped).astype(BF16),
            jnp.where(low, swapped, zero).astype(BF16), jnp.where(low, zero, a).astype(BF16))


def _sink_softmax(s, valid, sink):
    if valid is not None:
        s = jnp.where(valid, s, MASK_VALUE)
    m = jnp.maximum(jnp.max(s, axis=-1, keepdims=True), sink)
    p = jnp.exp(s - m)
    denom = jnp.sum(p, axis=-1, keepdims=True) + jnp.exp(sink - m)
    return (p / denom).astype(BF16)


def _group_sinks(sink_ref, kvh, rows):
    first = lax.broadcasted_iota(jnp.int32, (2 * rows, 1), 0) < rows
    base = kvh * (A_HEADS // A_KV_HEADS)
    return (jnp.where(first, sink_ref[base], sink_ref[base + 2]),
            jnp.where(first, sink_ref[base + 1], sink_ref[base + 3]))


def _attend(items, sinks):
    scores = []
    for q, keys, _, _ in items:
        for kvh in range(A_KV_HEADS):
            c0 = kvh * 256
            q2 = jnp.concatenate([q[:, c0:c0 + 128], q[:, c0 + 128:c0 + 256]], axis=0)
            scores += [_dot_nt(q2, keys[2 * kvh]), _dot_nt(q2, keys[2 * kvh + 1])]
    weights = []
    for n, (_, _, _, valid) in enumerate(items):
        for kvh in range(A_KV_HEADS):
            for half in range(2):
                weights.append(_sink_softmax(scores[4 * n + 2 * kvh + half], valid, sinks[kvh][half]))
    outs = []
    for n, (q, _, vals, _) in enumerate(items):
        rows = q.shape[0]
        groups = []
        for kvh in range(A_KV_HEADS):
            w = weights[4 * n + 2 * kvh:4 * n + 2 * kvh + 2]
            o = _dot(w[0], vals[2 * kvh]) + _dot(w[1], vals[2 * kvh + 1])
            groups.append(jnp.concatenate([o[:rows], o[rows:]], axis=1))
        outs.append(jnp.concatenate(groups, axis=1))
    return outs


def _attn_body(sink_ref, q_ref, kk_ref, vv_ref, ck_ref, cv_ref, o_ref):
    variant = [slice(v * A_KV_W, (v + 1) * A_KV_W) for v in range(4)]

    @pl.when(pl.program_id(0) < BATCH)
    def _():
        _attn_prompt_stream(sink_ref, q_ref, kk_ref, vv_ref, o_ref, variant)

    @pl.when(pl.program_id(0) == BATCH)
    def _():
        sinks = [_group_sinks(sink_ref, kvh, DEC_SEQ) for kvh in range(A_KV_HEADS)]
        items = []
        for b in range(DEC_BATCH):
            rows = slice(b * DEC_SEQ, (b + 1) * DEC_SEQ)
            old_k, old_v = _half_lane_variants(ck_ref[b]), _half_lane_variants(cv_ref[b])
            keys = [jnp.concatenate([old_k[v], kk_ref[rows, variant[v]]], axis=0) for v in range(4)]
            vals = [jnp.concatenate([old_v[v], vv_ref[rows, variant[v]]], axis=0) for v in range(4)]
            items.append((q_ref[rows, :], keys, vals, None))
        for b, out in enumerate(_attend(items, sinks)):
            o_ref[b * DEC_SEQ:(b + 1) * DEC_SEQ, :] = out.astype(BF16)


def _attn_prompt_stream(sink_ref, q_ref, kk_ref, vv_ref, o_ref, variant):
    span = (WINDOW_CHUNKS + 1) * CHUNK
    sinks = [_group_sinks(sink_ref, kvh, CHUNK) for kvh in range(A_KV_HEADS)]

    def step(n, carry):
        starts, items = [], []
        for i in range(ATTN_CHUNKS):
            c = n * ATTN_CHUNKS + i
            r0 = pl.multiple_of(c * CHUNK, CHUNK)
            s0 = pl.multiple_of(jnp.maximum(c - WINDOW_CHUNKS, 0) * CHUNK, CHUNK)
            kpos = s0 + lax.broadcasted_iota(jnp.int32, (1, span), 1)
            valid = jnp.logical_and(kpos >= PAD, kpos < r0 + CHUNK)
            starts.append(r0)
            items.append((q_ref[pl.ds(r0, CHUNK), :],
                          [kk_ref[pl.ds(s0, span), v] for v in variant],
                          [vv_ref[pl.ds(s0, span), v] for v in variant], valid))
        for r0, out in zip(starts, _attend(items, sinks)):
            o_ref[pl.ds(r0, CHUNK), :] = out.astype(BF16)
        return carry

    lax.fori_loop(0, N_CHUNKS // ATTN_CHUNKS, step, 0)


def _attn(sink, q, kk, vv, cache_k, cache_v):
    seq = lambda b: (b, 0)
    old = pl.BlockSpec(cache_k.shape, lambda b: (0, 0, 0))
    return pl.pallas_call(
        _attn_body,
        out_shape=jax.ShapeDtypeStruct((N_ROWS, A_Q_W), BF16),
        grid=(BATCH + 1,),
        in_specs=[
            pl.BlockSpec(memory_space=pltpu.SMEM),
            pl.BlockSpec((LP, A_Q_W), seq),
            pl.BlockSpec((LP, 4 * A_KV_W), seq),
            pl.BlockSpec((LP, 4 * A_KV_W), seq),
            old, old,
        ],
        out_specs=pl.BlockSpec((LP, A_Q_W), seq),
        compiler_params=_params("parallel"),
        name="attn",
    )(sink, q, kk, vv, cache_k, cache_v)


def _group_row(a, group, row):
    spans = a.shape[0] // group
    picked = a.reshape(spans, group, a.shape[1])[:, row:row + 1, :]
    return jnp.broadcast_to(picked, (spans, group, a.shape[1])).reshape(a.shape)


def _hgrn_pairs(q, k, c2):
    parts = []
    for s in range(HGRN_SUB):
        decay = jnp.exp2(jnp.minimum(c2 - _group_row(c2, HGRN_SUB, s), 0.0))
        parts.append((q * _group_row(k, HGRN_SUB, s) * decay).astype(BF16))
    return jnp.concatenate(parts, axis=1)


def _hgrn_pick(rows):
    shape = (HGRN_SUB * B_KEY_DIM, rows)
    block = lax.broadcasted_iota(jnp.int32, shape, 0) // B_KEY_DIM
    col = lax.broadcasted_iota(jnp.int32, shape, 1) & (HGRN_SUB - 1)
    return jnp.where(block == col, 1.0, 0.0).astype(BF16)


def _hgrn_head(q, c, v, state_t, intra, levels, update):
    rows = q.shape[0]
    t_idx = lax.broadcasted_iota(jnp.int32, (rows, rows), 0)
    u_idx = lax.broadcasted_iota(jnp.int32, (rows, rows), 1)
    apart = t_idx ^ u_idx
    in_group = jnp.logical_and(apart < HGRN_SUB, u_idx <= t_idx)
    scores = jnp.where(in_group, intra, 0.0)
    for w, level in levels:
        scores = scores + (level if 2 * w == rows else jnp.where(apart < 2 * w, level, 0.0))
    o = _dot(scores.astype(BF16), v) + _dot_nt((q * jnp.exp2(c)).astype(BF16), state_t.astype(BF16))
    return o, state_t * jnp.exp2(c[rows - 1:rows, :]) + update


def _hgrn_levels(q, k, c):
    rows = q.shape[0]
    pos = lax.broadcasted_iota(jnp.int32, q.shape, 0)
    levels = []
    w = HGRN_SUB
    while w < rows:
        rho = _group_row(c, 2 * w, w - 1)
        upper = (pos & w) != 0
        q_up = jnp.where(upper, q * jnp.exp2(jnp.minimum(c - rho, 0.0)), 0.0).astype(BF16)
        k_low = jnp.where(upper, 0.0, k * jnp.exp2(jnp.minimum(rho - c, 0.0))).astype(BF16)
        levels.append((w, _dot_nt(q_up, k_low)))
        w *= 2
    return levels


def _hgrn_tiles(tiles, n, hq_ref, hk_ref, hc_ref, hv_ref, hg_ref, gain_ref, o_ref, st_ref, pick):
    loaded = [(hq_ref[r, :], hk_ref[r, :], hc_ref[r, :] * LOG2_E, hv_ref[r, :], hg_ref[r, :])
              for r in tiles]
    states = [st_ref[hd] for hd in range(B_HEADS)]
    heads = [slice(hd * B_KEY_DIM, (hd + 1) * B_KEY_DIM) for hd in range(B_HEADS)]
    intras = [_dot(jnp.concatenate([_hgrn_pairs(q[:, h], k[:, h], c[:, h]) for h in heads], axis=0), pick)
              for q, k, c, _, _ in loaded]
    levels = [[_hgrn_levels(q[:, h], k[:, h], c[:, h]) for h in heads] for q, k, c, _, _ in loaded]
    updates = [[_dot_tn(v[:, h], (k[:, h] * jnp.exp2(c[n - 1:n, h] - c[:, h])).astype(BF16))
                for h in heads] for _, k, c, v, _ in loaded]
    results = []
    for t, (q, k, c, v, g) in enumerate(loaded):
        outs = []
        for hd, h in enumerate(heads):
            o, states[hd] = _hgrn_head(q[:, h], c[:, h], v[:, h], states[hd],
                                       intras[t][hd * n:(hd + 1) * n, :], levels[t][hd], updates[t][hd])
            outs.append(_rms(o, gain_ref[hd:hd + 1, :]) * g[:, h])
        results.append(jnp.concatenate(outs, axis=1).astype(BF16))
    for r, res in zip(tiles, results):
        o_ref[r, :] = res
    for hd in range(B_HEADS):
        st_ref[hd] = states[hd]


def _hgrn_body(*refs, n_cast):
    hq_ref, hk_ref, hc_ref, hv_ref, hg_ref, gain_ref, s0_ref = refs[:7]
    o_ref, sp_ref, ss_ref = refs[7 + n_cast:10 + n_cast]
    st_ref = refs[-1]
    step = pl.program_id(0)
    tiles = LP // HGRN_TILE
    tile = lax.rem(step, tiles)
    is_prompt = step < BATCH * tiles
    _cast_slabs(step, refs[7:7 + n_cast], refs[10 + n_cast:10 + 2 * n_cast])
    mixer = (hq_ref, hk_ref, hc_ref, hv_ref, hg_ref, gain_ref, o_ref, st_ref)

    @pl.when(jnp.logical_and(is_prompt, tile == 0))
    def _():
        st_ref[...] = jnp.zeros(st_ref.shape, F32)

    @pl.when(is_prompt)
    def _():
        pick = _hgrn_pick(HGRN_ROWS)
        n_tiles = HGRN_TILE // HGRN_ROWS

        def group(n, carry):
            first = n * HGRN_GROUP
            spans = [pl.ds(pl.multiple_of((first + i) * HGRN_ROWS, HGRN_ROWS), HGRN_ROWS)
                     for i in range(HGRN_GROUP)]
            _hgrn_tiles(spans, HGRN_ROWS, *mixer, pick)
            return carry

        lax.fori_loop(0, n_tiles // HGRN_GROUP, group, 0)
        rest = [pl.ds(i * HGRN_ROWS, HGRN_ROWS) for i in range(n_tiles - n_tiles % HGRN_GROUP, n_tiles)]
        if rest:
            _hgrn_tiles(rest, HGRN_ROWS, *mixer, pick)

    @pl.when(jnp.logical_and(is_prompt, tile == tiles - 1))
    def _():
        for hd in range(B_HEADS):
            sp_ref[hd] = st_ref[hd].T

    @pl.when(jnp.logical_not(is_prompt))
    def _():
        pick = _hgrn_pick(DEC_SEQ)
        for b in range(DEC_BATCH):
            for hd in range(B_HEADS):
                st_ref[hd] = s0_ref[b, hd].T
            _hgrn_tiles([pl.ds(b * DEC_SEQ, DEC_SEQ)], DEC_SEQ, *mixer, pick)
            for hd in range(B_HEADS):
                ss_ref[b, hd] = st_ref[hd].T


def _hgrn(hq, hk, hb, hv, hg, gain, s0, ffn_f32, cast_targets):
    tiles = LP // HGRN_TILE
    row = lambda s: (s, 0)
    spec = pl.BlockSpec((HGRN_TILE, B_QK_W), row)
    state = (B_HEADS, B_KEY_DIM, B_VAL_DIM)
    cast_in, cast_out, cast_shapes = _cast_plan(cast_targets, lambda s: s)
    out = pl.pallas_call(
        functools.partial(_hgrn_body, n_cast=len(cast_in)),
        out_shape=(jax.ShapeDtypeStruct((N_ROWS, B_V_W), BF16),
                   jax.ShapeDtypeStruct((BATCH,) + state, F32),
                   jax.ShapeDtypeStruct((DEC_BATCH,) + state, F32), *cast_shapes),
        grid=(BATCH * tiles + 1,),
        in_specs=[spec, spec, spec, spec, spec,
                  pl.BlockSpec((B_HEADS, B_VAL_DIM), lambda s: (0, 0)),
                  pl.BlockSpec((DEC_BATCH,) + state, lambda s: (0, 0, 0, 0)), *cast_in],
        out_specs=(spec,
                   pl.BlockSpec((None,) + state, lambda s: (jnp.minimum(s // tiles, BATCH - 1), 0, 0, 0)),
                   pl.BlockSpec((DEC_BATCH,) + state, lambda s: (0, 0, 0, 0)), *cast_out),
        scratch_shapes=[pltpu.VMEM((B_HEADS, B_VAL_DIM, B_KEY_DIM), F32)],
        compiler_params=_params("arbitrary"),
        name="hgrn",
    )(hq, hk, hb, hv, hg, gain, s0, *(ffn_f32 * len(cast_targets)))
    return out[0], out[1], out[2], out[3:]


def _even_out_body(x_ref, g_ref, oa_ref, ob_ref, w_ref, o_ref):
    for lo in range(0, FFN_TILE, FFN_SUB):
        rows = slice(lo, lo + FFN_SUB)
        m = _dot(oa_ref[rows, :], w_ref[:A_Q_W, :]) + _dot(ob_ref[rows, :], w_ref[A_Q_W:, :])
        y = x_ref[rows, :] + _rms(m, g_ref[3:4, :])
        pad = _is_pad_row(pl.program_id(0) * FFN_TILE + lo, FFN_SUB)
        o_ref[rows, :] = jnp.where(pad, 0.0, y)


def _even_out(x, gains, oa, ob, w_out, j):
    row = lambda i: (i, 0)
    return pl.pallas_call(
        _even_out_body,
        out_shape=jax.ShapeDtypeStruct((N_ROWS, D_MODEL), F32),
        grid=(N_ROWS // FFN_TILE,),
        in_specs=[
            pl.BlockSpec((FFN_TILE, D_MODEL), row),
            pl.BlockSpec((6, D_MODEL), lambda i: (0, 0)),
            pl.BlockSpec((FFN_TILE, A_Q_W), row),
            pl.BlockSpec((FFN_TILE, B_V_W), row),
            pl.BlockSpec((None, A_Q_W + B_V_W, D_MODEL), lambda i: (j, 0, 0),
                         pipeline_mode=pl.Buffered(1)),
        ],
        out_specs=pl.BlockSpec((FFN_TILE, D_MODEL), row),
        input_output_aliases={0: 0},
        compiler_params=_params("parallel"),
        name=f"even_out_{j}",
    )(x, gains, oa, ob, w_out)


def _conv3(u, cw_ref, before1=None, before2=None):
    shift1, shift2 = pltpu.roll(u, 1, 0), pltpu.roll(u, 2, 0)
    if before1 is not None:
        pos = lax.broadcasted_iota(jnp.int32, (u.shape[0], 1), 0)
        shift1 = jnp.where(pos == 0, before1, shift1)
        shift2 = jnp.where(pos == 0, before2, jnp.where(pos == 1, before1, shift2))
    return shift2 * cw_ref[0:1, :] + shift1 * cw_ref[1:2, :] + u * cw_ref[2:3, :]


def _odd_body(*refs, n_cast):
    x_ref, halo_ref, g_ref, w_in_ref, cw_ref, w_out_ref, cache_ref = refs[:7]
    o_ref, tail_ref, new_cache_ref = refs[7 + n_cast:10 + n_cast]
    _cast_slabs(pl.program_id(0), refs[7:7 + n_cast], refs[10 + n_cast:])
    is_prompt = pl.program_id(0) < P_ROWS // ODD_TILE

    @pl.when(is_prompt)
    def _():
        x = x_ref[...]
        h = _rms(x, g_ref[2:3, :]).astype(BF16)
        h_ext = jnp.concatenate([_rms(halo_ref[...], g_ref[2:3, :]).astype(BF16), h], axis=0)
        bg = _dot(h, w_in_ref[:, :D_MODEL])
        u_ext = _dot(h_ext, w_in_ref[:, D_MODEL:2 * D_MODEL]) * _dot(h_ext, w_in_ref[:, 2 * D_MODEL:])
        tail_ref[...] = u_ext[ODD_HALO + ODD_TILE - TAIL_ROWS:, :]
        y = _conv3(u_ext, cw_ref)[ODD_HALO:, :]
        m = _dot((bg * y).astype(BF16), w_out_ref[...])
        out = x + _rms(m, g_ref[3:4, :])
        o_ref[...] = jnp.where(_is_pad_row(pl.program_id(0) * ODD_TILE, ODD_TILE), 0.0, out)

    @pl.when(jnp.logical_not(is_prompt))
    def _():
        x = x_ref[:S_ROWS, :]
        h = _rms(x, g_ref[2:3, :]).astype(BF16)
        bg = _dot(h, w_in_ref[:, :D_MODEL])
        u = _dot(h, w_in_ref[:, D_MODEL:2 * D_MODEL]) * _dot(h, w_in_ref[:, 2 * D_MODEL:])
        ys = []
        for b in range(DEC_BATCH):
            ub = u[b * DEC_SEQ:(b + 1) * DEC_SEQ, :]
            ys.append(_conv3(ub, cw_ref, before1=cache_ref[b, 1:2, :], before2=cache_ref[b, 0:1, :]))
            new_cache_ref[b] = ub[DEC_SEQ - (CONV_WIDTH - 1):, :]
        m = _dot((bg * jnp.concatenate(ys, axis=0)).astype(BF16), w_out_ref[...])
        o_ref[:S_ROWS, :] = x + _rms(m, g_ref[3:4, :])
        tail_ref[...] = jnp.zeros(tail_ref.shape, F32)


def _odd(x, gains, w_in, conv_w, w_out, cache, j, ffn_f32, cast_targets):
    tiles = pl.cdiv(N_ROWS, ODD_TILE)
    row = lambda i: (i, 0)
    per = ODD_TILE // ODD_HALO
    whole3 = lambda i: (0, 0, 0)
    halo = lambda i: (jnp.maximum(i * per - 1, 0), 0)
    cast_in, cast_out, cast_shapes = _cast_plan(cast_targets, lambda i: i)
    out = pl.pallas_call(
        functools.partial(_odd_body, n_cast=len(cast_in)),
        out_shape=(jax.ShapeDtypeStruct((N_ROWS, D_MODEL), F32),
                   jax.ShapeDtypeStruct((tiles, TAIL_ROWS, D_MODEL), F32),
                   jax.ShapeDtypeStruct(cache.shape, F32), *cast_shapes),
        grid=(tiles,),
        in_specs=[
            pl.BlockSpec((ODD_TILE, D_MODEL), row),
            pl.BlockSpec((ODD_HALO, D_MODEL), halo),
            pl.BlockSpec((6, D_MODEL), lambda i: (0, 0)),
            pl.BlockSpec((None, D_MODEL, 3 * D_MODEL), lambda i: (j, 0, 0), pipeline_mode=pl.Buffered(1)),
            pl.BlockSpec((None, CONV_WIDTH, D_MODEL), lambda i: (j, 0, 0)),
            pl.BlockSpec((None, D_MODEL, D_MODEL), lambda i: (j, 0, 0), pipeline_mode=pl.Buffered(1)),
            pl.BlockSpec(cache.shape, whole3),
            *cast_in,
        ],
        out_specs=(pl.BlockSpec((ODD_TILE, D_MODEL), row),
                   pl.BlockSpec((None, TAIL_ROWS, D_MODEL), lambda i: (i, 0, 0)),
                   pl.BlockSpec(cache.shape, whole3), *cast_out),
        compiler_params=_params("arbitrary"),
        name=f"odd_{j}",
    )(x, x, gains, w_in, conv_w, w_out, cache, *(ffn_f32 * len(cast_targets)))
    return out[0], out[1], out[2], out[3:]


def _place_frames_body(x_ref, o_ref):
    o_ref[...] = x_ref[...]


def _place_frames(x_prompt):
    steps = SEQ // PLACE_ROWS
    return pl.pallas_call(
        _place_frames_body,
        out_shape=jax.ShapeDtypeStruct((N_ROWS, D_MODEL), F32),
        grid=(BATCH, steps),
        in_specs=[pl.BlockSpec((None, PLACE_ROWS, D_MODEL), lambda b, t: (b, t, 0))],
        out_specs=pl.BlockSpec((pl.Element(PLACE_ROWS), pl.Element(D_MODEL)),
                               lambda b, t: (pl.multiple_of(b * LP + PAD + N_META + t * PLACE_ROWS, CHUNK), 0)),
        compiler_params=_params("parallel", "parallel"),
        name="place_frames",
    )(x_prompt)


def _stream_tails(a, row0, streams, length, n):
    ends = [row0 + (s + 1) * length for s in range(streams)]
    return jnp.stack([lax.slice_in_dim(a, e - n, e, axis=0) for e in ends])


def kernel(x_prompt, x_sample, cache_swa_k, cache_swa_v, state_hgrn, cache_conv, meta_tokens, norm_gains,
           w_ffn_gate, w_ffn_up, w_ffn_down, w_in_even, w_out_even, attn_sinks, hgrn_lb_logits,
           hgrn_norm_gain, w_in_odd, conv_w, w_out_odd):
    win = cache_swa_k.shape[2]
    lead = jnp.concatenate([jnp.zeros((PAD, D_MODEL), F32), meta_tokens.astype(F32)], axis=0)
    x = _place_frames(x_prompt)
    x = lax.dynamic_update_slice(x, x_sample.reshape(S_ROWS, D_MODEL), (P_ROWS, 0))
    for b in range(BATCH):
        x = lax.dynamic_update_slice(x, lead, (b * LP, 0))

    ffn_f32 = (w_ffn_gate, w_ffn_up, w_ffn_down)
    ffn_w = {(0, 0): (w_ffn_gate[0, 0].astype(BF16), w_ffn_up[0, 0].astype(BF16),
                      w_ffn_down[0, 0].astype(BF16))}

    def keep_converted(targets, converted):
        for n, key in enumerate(targets):
            ffn_w[key] = tuple(converted[3 * n:3 * n + 3])

    w_in_e, w_out_e = w_in_even.astype(BF16), w_out_even.astype(BF16)
    w_in_o, w_out_o = w_in_odd.astype(BF16), w_out_odd.astype(BF16)
    cache_k = cache_swa_k.reshape(N_EVEN, DEC_BATCH, win, A_KV_W)
    cache_v = cache_swa_v.reshape(N_EVEN, DEC_BATCH, win, A_KV_W)

    k_p, v_p, s_p, c_p, k_s, v_s, s_s, c_s = ([] for _ in range(8))
    for layer in range(DEPTH):
        gains = norm_gains[layer]
        j = layer // 2
        x = _ffn(x, norm_gains, ffn_w[(layer, 0)], layer, 0)
        targets = [(layer, 1)] + ([(layer + 1, 0)] if layer + 1 < DEPTH else [])
        if layer % 2 == 0:
            q, k, v, kk, vv, hq, hk, hb, hv, hg = _even_in(x, gains, w_in_e, hgrn_lb_logits, j)
            oa = _attn(attn_sinks[j], q, kk, vv, cache_k[j], cache_v[j])
            ob, st_p, st_s, converted = _hgrn(hq, hk, hb, hv, hg, hgrn_norm_gain[j], state_hgrn[j],
                                              ffn_f32, targets)
            keep_converted(targets, converted)
            x = _even_out(x, gains, oa, ob, w_out_e, j)
            for new, tail_p, tail_s, cache in ((k, k_p, k_s, cache_k[j]), (v, v_p, v_s, cache_v[j])):
                tail_p.append(_stream_tails(new, 0, BATCH, LP, WINDOW).reshape(
                    BATCH, WINDOW, A_KV_HEADS, A_HEAD_DIM))
                both = jnp.concatenate([cache, new[P_ROWS:].reshape(DEC_BATCH, DEC_SEQ, A_KV_W)], axis=1)
                tail_s.append(both[:, both.shape[1] - win:].reshape(DEC_BATCH, win, A_KV_HEADS, A_HEAD_DIM))
            s_p.append(st_p)
            s_s.append(st_s)
        else:
            x, tails, conv_s, converted = _odd(x, gains, w_in_o, conv_w, w_out_o, cache_conv[j], j,
                                               ffn_f32, targets)
            keep_converted(targets, converted)
            per_stream = LP // ODD_TILE
            c_p.append(jnp.stack([tails[(b + 1) * per_stream - 1, TAIL_ROWS - (CONV_WIDTH - 1):]
                                  for b in range(BATCH)]))
            c_s.append(conv_s)
        x = _ffn(x, norm_gains, ffn_w[(layer, 1)], layer, 1)

    y_prompt = _stream_tails(x, 0, BATCH, LP, SEQ)
    y_sample = x[P_ROWS:].reshape(DEC_BATCH, DEC_SEQ, D_MODEL)
    return (y_prompt, y_sample, jnp.stack(k_p), jnp.stack(v_p), jnp.stack(s_p), jnp.stack(c_p),
            jnp.stack(k_s), jnp.stack(v_s), jnp.stack(s_s), jnp.stack(c_s))
```

```python
import functools

import jax
import jax.numpy as jnp
from jax import lax
from jax.experimental import pallas as pl
from jax.experimental.pallas import tpu as pltpu

F32 = jnp.float32
BF16 = jnp.bfloat16

D_MODEL = 1024
BATCH = 4
SEQ = 4096
DEPTH = 4
DEC_BATCH = 8
DEC_SEQ = 32
CHUNK = 64
N_META = 16
WINDOW = 128
WINDOW_CHUNKS = WINDOW // CHUNK
A_HEADS = 8
A_KV_HEADS = 2
A_HEAD_DIM = 64
A_Q_W = A_HEADS * A_HEAD_DIM
A_KV_W = A_KV_HEADS * A_HEAD_DIM
B_HEADS = 4
B_KEY_DIM = 128
B_VAL_DIM = 128
B_QK_W = B_HEADS * B_KEY_DIM
B_V_W = B_HEADS * B_VAL_DIM
B_BLOCK = 16
CONV_WIDTH = 3
D_FF = 2816
FFN_RESIDUAL = 0.5
N_EVEN = (DEPTH + 1) // 2
N_ODD = DEPTH // 2
EVEN_IN_W = A_Q_W + 2 * A_KV_W + 2 * B_QK_W + 2 * B_V_W
EPS = 1e-6
MASK_VALUE = -1e30
LB_FLOOR = 1e-30

PAD = (-N_META) % CHUNK
LP = PAD + N_META + SEQ
N_CHUNKS = LP // CHUNK
P_ROWS = BATCH * LP
S_ROWS = DEC_BATCH * DEC_SEQ
N_ROWS = P_ROWS + S_ROWS

ROW_TILE = 512
FF_CHUNK = 256
HGRN_TILE = 832
HGRN_ROWS = 64
HGRN_SUB = 8
HGRN_GROUP = 3
LOG2_E = 1.4426950408889634
CUMSUM_ROWS = 256
FFN_TILE = 1536
FFN_SUB = 512
ATTN_CHUNKS = 5
PLACE_ROWS = 1024
CAST_SLAB = 256
CAST_STEPS = D_FF // CAST_SLAB
ODD_TILE = 832
ODD_HALO = 16
TAIL_ROWS = 8
VMEM_LIMIT = 56 * 1024 * 1024

assert N_ROWS % ROW_TILE == 0 and D_FF % FF_CHUNK == 0
assert LP % HGRN_TILE == 0 and HGRN_TILE % HGRN_ROWS == 0 and ROW_TILE % HGRN_ROWS == 0
assert HGRN_ROWS % DEC_SEQ == 0 and DEC_SEQ % HGRN_SUB == 0
assert ROW_TILE % CUMSUM_ROWS == 0 and CUMSUM_ROWS % HGRN_ROWS == 0
assert N_ROWS % FFN_TILE == 0 and FFN_TILE % FFN_SUB == 0 and N_CHUNKS % ATTN_CHUNKS == 0
assert D_FF % CAST_SLAB == 0 and CAST_STEPS <= min(P_ROWS // ODD_TILE, P_ROWS // HGRN_TILE)
assert LP % ODD_TILE == 0 and ODD_TILE % ODD_HALO == 0 and P_ROWS % DEC_SEQ == 0
assert P_ROWS % S_ROWS == 0 and S_ROWS <= ODD_TILE and TAIL_ROWS >= CONV_WIDTH - 1


def _params(*sem):
    return pltpu.CompilerParams(dimension_semantics=sem, vmem_limit_bytes=VMEM_LIMIT)


def _rms(x, gain):
    return x * lax.rsqrt(jnp.mean(x * x, axis=-1, keepdims=True) + EPS) * gain


def _silu(x):
    return x * jax.nn.sigmoid(x)


def _dot(a, b):
    return jnp.dot(a, b, preferred_element_type=F32)


def _dot_nt(a, b):
    return lax.dot_general(a, b, (((1,), (1,)), ((), ())), preferred_element_type=F32)


def _dot_tn(a, b):
    return lax.dot_general(a, b, (((0,), (0,)), ((), ())), preferred_element_type=F32)


def _is_pad_row(row0, rows):
    r = row0 + lax.broadcasted_iota(jnp.int32, (rows, 1), 0)
    pad = None
    for b in range(BATCH):
        hit = jnp.logical_and(r >= b * LP, r < b * LP + PAD)
        pad = hit if pad is None else jnp.logical_or(pad, hit)
    return pad


def _ffn_body(sel_ref, x_ref, g_ref, wg_ref, wu_ref, wd_ref, o_ref):
    pre = sel_ref[1] * 4
    post = pre + 1
    pieces = [slice(lo, lo + FFN_SUB) for lo in range(0, FFN_TILE, FFN_SUB)]
    chunks = [slice(c * FF_CHUNK, (c + 1) * FF_CHUNK) for c in range(D_FF // FF_CHUNK)]
    xs = [x_ref[rows, :] for rows in pieces]
    hs = [_rms(x, g_ref[pl.ds(pre, 1), :]).astype(BF16) for x in xs]
    accs = []
    for h in hs:
        acc = jnp.zeros((FFN_SUB, D_MODEL), F32)
        for cols in chunks:
            act = (_silu(_dot(h, wg_ref[:, cols])) * _dot(h, wu_ref[:, cols])).astype(BF16)
            acc = acc + _dot(act, wd_ref[cols, :])
        accs.append(acc)
    for rows, x, acc in zip(pieces, xs, accs):
        o_ref[rows, :] = x + FFN_RESIDUAL * _rms(acc, g_ref[pl.ds(post, 1), :])


def _ffn(x, norm_gains, weights, layer, which):
    row = lambda i, sel: (i, 0)
    whole = lambda i, sel: (0, 0)
    resident = pl.Buffered(1)
    grid_spec = pltpu.PrefetchScalarGridSpec(
        num_scalar_prefetch=1,
        grid=(N_ROWS // FFN_TILE,),
        in_specs=[
            pl.BlockSpec((FFN_TILE, D_MODEL), row),
            pl.BlockSpec((None, 6, D_MODEL), lambda i, sel: (sel[0], 0, 0)),
            pl.BlockSpec((D_MODEL, D_FF), whole, pipeline_mode=resident),
            pl.BlockSpec((D_MODEL, D_FF), whole, pipeline_mode=resident),
            pl.BlockSpec((D_FF, D_MODEL), whole, pipeline_mode=resident),
        ],
        out_specs=pl.BlockSpec((FFN_TILE, D_MODEL), row),
    )
    return pl.pallas_call(
        _ffn_body,
        out_shape=jax.ShapeDtypeStruct((N_ROWS, D_MODEL), F32),
        grid_spec=grid_spec,
        input_output_aliases={1: 0},
        compiler_params=_params("parallel"),
        name="ffn",
    )(jnp.array([layer, which], jnp.int32), x, norm_gains, *weights)


def _cast_plan(targets, step_of):
    slab = lambda *ids: jnp.minimum(step_of(*ids), CAST_STEPS - 1)
    in_specs, out_specs, shapes = [], [], []
    for layer, which in targets:
        for block, shape, place in (((D_MODEL, CAST_SLAB), (D_MODEL, D_FF), lambda s: (0, s)),
                                    ((D_MODEL, CAST_SLAB), (D_MODEL, D_FF), lambda s: (0, s)),
                                    ((CAST_SLAB, D_MODEL), (D_FF, D_MODEL), lambda s: (s, 0))):
            in_specs.append(pl.BlockSpec((None, None) + block,
                                         lambda *ids, l=layer, w=which, p=place: (l, w) + p(slab(*ids))))
            out_specs.append(pl.BlockSpec(block, lambda *ids, p=place: p(slab(*ids))))
            shapes.append(jax.ShapeDtypeStruct(shape, BF16))
    return in_specs, out_specs, shapes


def _cast_slabs(step, srcs, dsts):
    @pl.when(step < CAST_STEPS)
    def _():
        for src, dst in zip(srcs, dsts):
            dst[...] = src[...].astype(BF16)


def _lower_bound(lbl_ref, j):
    rows = [lbl_ref[i:i + 1, :] for i in range(N_EVEN)]
    top = functools.reduce(jnp.maximum, rows)
    e = [jnp.exp(r - top) for r in rows]
    total = functools.reduce(lambda a, b: a + b, e)
    p = [v / total for v in e]
    cum = functools.reduce(lambda a, b: a + b, p[:j + 1])
    return jnp.maximum(cum - p[0], 0.0)


def _span_cumsum(g, row0):
    out = []
    for lo in range(0, g.shape[0], CUMSUM_ROWS):
        part = g[lo:lo + CUMSUM_ROWS, :]
        t = row0 + lo + lax.broadcasted_iota(jnp.int32, (CUMSUM_ROWS, CUMSUM_ROWS), 0)
        u = row0 + lo + lax.broadcasted_iota(jnp.int32, (CUMSUM_ROWS, CUMSUM_ROWS), 1)
        span = jnp.where(t >= P_ROWS, DEC_SEQ, HGRN_ROWS)
        tri = jnp.where(jnp.logical_and((t ^ u) < span, u <= t), 1.0, 0.0).astype(BF16)
        total = None
        for _ in range(3):
            term = part.astype(BF16)
            part = part - term.astype(F32)
            total = _dot(tri, term) if total is None else total + _dot(tri, term)
        out.append(total)
    return jnp.concatenate(out, axis=0)


def _even_in_body(x_ref, g_ref, w_ref, lbl_ref, q_ref, k_ref, v_ref, kk_ref, vv_ref,
                  hq_ref, hk_ref, hb_ref, hv_ref, hg_ref, *, j):
    h = _rms(x_ref[...], g_ref[2:3, :]).astype(BF16)
    widths = (A_Q_W, 2 * A_KV_W, B_QK_W, B_QK_W, B_V_W, B_V_W)
    starts = [sum(widths[:i]) for i in range(len(widths))]
    proj = lambda i: _dot(h, w_ref[:, starts[i]:starts[i] + widths[i]])
    fx = proj(3)
    q_ref[...] = (proj(0) * (A_HEAD_DIM ** -0.5)).astype(BF16)
    kv = proj(1)
    k_ref[...] = kv[:, :A_KV_W]
    v_ref[...] = kv[:, A_KV_W:]
    kk_ref[...] = jnp.concatenate(_half_lane_variants(kv[:, :A_KV_W]), axis=1)
    vv_ref[...] = jnp.concatenate(_half_lane_variants(kv[:, A_KV_W:]), axis=1)
    hq_ref[...] = _silu(proj(2))
    hv_ref[...] = proj(4).astype(BF16)
    hg_ref[...] = _silu(proj(5))
    lb = _lower_bound(lbl_ref, j)
    soft = jnp.log1p(jnp.exp(-jnp.abs(fx)))
    ls_pos = jnp.minimum(fx, 0.0) - soft
    ls_neg = jnp.minimum(-fx, 0.0) - soft
    other = jnp.log(jnp.maximum(lb, LB_FLOOR)) + ls_neg
    log_f = jnp.maximum(ls_pos, other) + jnp.log1p(jnp.exp(-jnp.abs(ls_pos - other)))
    hk_ref[...] = (1.0 - lb) * jax.nn.sigmoid(-fx)
    hb_ref[...] = _span_cumsum(log_f, pl.program_id(0) * ROW_TILE)


def _even_in(x, gains, w_in, lb_logits, j):
    row = lambda i: (i, 0)
    whole = lambda i: (0, 0)
    wide = lambda w, dt: jax.ShapeDtypeStruct((N_ROWS, w), dt)
    spec = lambda w: pl.BlockSpec((ROW_TILE, w), row)
    return pl.pallas_call(
        functools.partial(_even_in_body, j=j),
        out_shape=(wide(A_Q_W, BF16), wide(A_KV_W, F32), wide(A_KV_W, F32),
                   wide(4 * A_KV_W, BF16), wide(4 * A_KV_W, BF16),
                   wide(B_QK_W, F32), wide(B_QK_W, F32), wide(B_QK_W, F32),
                   wide(B_V_W, BF16), wide(B_V_W, F32)),
        grid=(N_ROWS // ROW_TILE,),
        in_specs=[
            pl.BlockSpec((ROW_TILE, D_MODEL), row),
            pl.BlockSpec((6, D_MODEL), whole),
            pl.BlockSpec((None, D_MODEL, EVEN_IN_W), lambda i: (j, 0, 0),
                         pipeline_mode=pl.Buffered(1)),
            pl.BlockSpec((N_EVEN, B_QK_W), whole),
        ],
        out_specs=(spec(A_Q_W), spec(A_KV_W), spec(A_KV_W), spec(4 * A_KV_W), spec(4 * A_KV_W),
                   spec(B_QK_W), spec(B_QK_W), spec(B_QK_W), spec(B_V_W), spec(B_V_W)),
        compiler_params=_params("parallel"),
        name=f"even_in_{j}",
    )(x, gains, w_in, lb_logits)


def _half_lane_variants(a):
    lane = lax.broadcasted_iota(jnp.int32, a.shape, 1)
    low = lane < A_HEAD_DIM
    swapped = pltpu.roll(a, A_HEAD_DIM, 1)
    zero = jnp.zeros_like(a)
    return (jnp.where(low, a, zero).astype(BF16), jnp.where(low, zero, swapped).astype(BF16),
            jnp.where(low, swapped, zero).astype(BF16), jnp.where(low, zero, a).astype(BF16))


def _sink_softmax(s, valid, sink):
    if valid is not None:
        s = jnp.where(valid, s, MASK_VALUE)
    m = jnp.maximum(jnp.max(s, axis=-1, keepdims=True), sink)
    p = jnp.exp(s - m)
    denom = jnp.sum(p, axis=-1, keepdims=True) + jnp.exp(sink - m)
    return (p / denom).astype(BF16)


def _group_sinks(sink_ref, kvh, rows):
    first = lax.broadcasted_iota(jnp.int32, (2 * rows, 1), 0) < rows
    base = kvh * (A_HEADS // A_KV_HEADS)
    return (jnp.where(first, sink_ref[base], sink_ref[base + 2]),
            jnp.where(first, sink_ref[base + 1], sink_ref[base + 3]))


def _attend(items, sinks):
    scores = []
    for q, keys, _, _ in items:
        for kvh in range(A_KV_HEADS):
            c0 = kvh * 256
            q2 = jnp.concatenate([q[:, c0:c0 + 128], q[:, c0 + 128:c0 + 256]], axis=0)
            scores += [_dot_nt(q2, keys[2 * kvh]), _dot_nt(q2, keys[2 * kvh + 1])]
    weights = []
    for n, (_, _, _, valid) in enumerate(items):
        for kvh in range(A_KV_HEADS):
            for half in range(2):
                weights.append(_sink_softmax(scores[4 * n + 2 * kvh + half], valid, sinks[kvh][half]))
    outs = []
    for n, (q, _, vals, _) in enumerate(items):
        rows = q.shape[0]
        groups = []
        for kvh in range(A_KV_HEADS):
            w = weights[4 * n + 2 * kvh:4 * n + 2 * kvh + 2]
            o = _dot(w[0], vals[2 * kvh]) + _dot(w[1], vals[2 * kvh + 1])
            groups.append(jnp.concatenate([o[:rows], o[rows:]], axis=1))
        outs.append(jnp.concatenate(groups, axis=1))
    return outs


def _attn_body(sink_ref, q_ref, kk_ref, vv_ref, ck_ref, cv_ref, o_ref):
    variant = [slice(v * A_KV_W, (v + 1) * A_KV_W) for v in range(4)]

    @pl.when(pl.program_id(0) < BATCH)
    def _():
        _attn_prompt_stream(sink_ref, q_ref, kk_ref, vv_ref, o_ref, variant)

    @pl.when(pl.program_id(0) == BATCH)
    def _():
        sinks = [_group_sinks(sink_ref, kvh, DEC_SEQ) for kvh in range(A_KV_HEADS)]
        items = []
        for b in range(DEC_BATCH):
            rows = slice(b * DEC_SEQ, (b + 1) * DEC_SEQ)
            old_k, old_v = _half_lane_variants(ck_ref[b]), _half_lane_variants(cv_ref[b])
            keys = [jnp.concatenate([old_k[v], kk_ref[rows, variant[v]]], axis=0) for v in range(4)]
            vals = [jnp.concatenate([old_v[v], vv_ref[rows, variant[v]]], axis=0) for v in range(4)]
            items.append((q_ref[rows, :], keys, vals, None))
        for b, out in enumerate(_attend(items, sinks)):
            o_ref[b * DEC_SEQ:(b + 1) * DEC_SEQ, :] = out.astype(BF16)


def _attn_prompt_stream(sink_ref, q_ref, kk_ref, vv_ref, o_ref, variant):
    span = (WINDOW_CHUNKS + 1) * CHUNK
    sinks = [_group_sinks(sink_ref, kvh, CHUNK) for kvh in range(A_KV_HEADS)]

    def step(n, carry):
        starts, items = [], []
        for i in range(ATTN_CHUNKS):
            c = n * ATTN_CHUNKS + i
            r0 = pl.multiple_of(c * CHUNK, CHUNK)
            s0 = pl.multiple_of(jnp.maximum(c - WINDOW_CHUNKS, 0) * CHUNK, CHUNK)
            kpos = s0 + lax.broadcasted_iota(jnp.int32, (1, span), 1)
            valid = jnp.logical_and(kpos >= PAD, kpos < r0 + CHUNK)
            starts.append(r0)
            items.append((q_ref[pl.ds(r0, CHUNK), :],
                          [kk_ref[pl.ds(s0, span), v] for v in variant],
                          [vv_ref[pl.ds(s0, span), v] for v in variant], valid))
        for r0, out in zip(starts, _attend(items, sinks)):
            o_ref[pl.ds(r0, CHUNK), :] = out.astype(BF16)
        return carry

    lax.fori_loop(0, N_CHUNKS // ATTN_CHUNKS, step, 0)


def _attn(sink, q, kk, vv, cache_k, cache_v):
    seq = lambda b: (b, 0)
    old = pl.BlockSpec(cache_k.shape, lambda b: (0, 0, 0))
    return pl.pallas_call(
        _attn_body,
        out_shape=jax.ShapeDtypeStruct((N_ROWS, A_Q_W), BF16),
        grid=(BATCH + 1,),
        in_specs=[
            pl.BlockSpec(memory_space=pltpu.SMEM),
            pl.BlockSpec((LP, A_Q_W), seq),
            pl.BlockSpec((LP, 4 * A_KV_W), seq),
            pl.BlockSpec((LP, 4 * A_KV_W), seq),
            old, old,
        ],
        out_specs=pl.BlockSpec((LP, A_Q_W), seq),
        compiler_params=_params("parallel"),
        name="attn",
    )(sink, q, kk, vv, cache_k, cache_v)


def _group_row(a, group, row):
    spans = a.shape[0] // group
    picked = a.reshape(spans, group, a.shape[1])[:, row:row + 1, :]
    return jnp.broadcast_to(picked, (spans, group, a.shape[1])).reshape(a.shape)


def _hgrn_pairs(q, k, c2):
    parts = []
    for s in range(HGRN_SUB):
        decay = jnp.exp2(jnp.minimum(c2 - _group_row(c2, HGRN_SUB, s), 0.0))
        parts.append((q * _group_row(k, HGRN_SUB, s) * decay).astype(BF16))
    return jnp.concatenate(parts, axis=1)


def _hgrn_pick(rows):
    shape = (HGRN_SUB * B_KEY_DIM, rows)
    block = lax.broadcasted_iota(jnp.int32, shape, 0) // B_KEY_DIM
    col = lax.broadcasted_iota(jnp.int32, shape, 1) & (HGRN_SUB - 1)
    return jnp.where(block == col, 1.0, 0.0).astype(BF16)


def _hgrn_head(q, c, v, state_t, intra, levels, update):
    rows = q.shape[0]
    t_idx = lax.broadcasted_iota(jnp.int32, (rows, rows), 0)
    u_idx = lax.broadcasted_iota(jnp.int32, (rows, rows), 1)
    apart = t_idx ^ u_idx
    in_group = jnp.logical_and(apart < HGRN_SUB, u_idx <= t_idx)
    scores = jnp.where(in_group, intra, 0.0)
    for w, level in levels:
        scores = scores + (level if 2 * w == rows else jnp.where(apart < 2 * w, level, 0.0))
    o = _dot(scores.astype(BF16), v) + _dot_nt((q * jnp.exp2(c)).astype(BF16), state_t.astype(BF16))
    return o, state_t * jnp.exp2(c[rows - 1:rows, :]) + update


def _hgrn_levels(q, k, c):
    rows = q.shape[0]
    pos = lax.broadcasted_iota(jnp.int32, q.shape, 0)
    levels = []
    w = HGRN_SUB
    while w < rows:
        rho = _group_row(c, 2 * w, w - 1)
        upper = (pos & w) != 0
        q_up = jnp.where(upper, q * jnp.exp2(jnp.minimum(c - rho, 0.0)), 0.0).astype(BF16)
        k_low = jnp.where(upper, 0.0, k * jnp.exp2(jnp.minimum(rho - c, 0.0))).astype(BF16)
        levels.append((w, _dot_nt(q_up, k_low)))
        w *= 2
    return levels


def _hgrn_tiles(tiles, n, hq_ref, hk_ref, hc_ref, hv_ref, hg_ref, gain_ref, o_ref, st_ref, pick):
    loaded = [(hq_ref[r, :], hk_ref[r, :], hc_ref[r, :] * LOG2_E, hv_ref[r, :], hg_ref[r, :])
              for r in tiles]
    states = [st_ref[hd] for hd in range(B_HEADS)]
    heads = [slice(hd * B_KEY_DIM, (hd + 1) * B_KEY_DIM) for hd in range(B_HEADS)]
    intras = [_dot(jnp.concatenate([_hgrn_pairs(q[:, h], k[:, h], c[:, h]) for h in heads], axis=0), pick)
              for q, k, c, _, _ in loaded]
    levels = [[_hgrn_levels(q[:, h], k[:, h], c[:, h]) for h in heads] for q, k, c, _, _ in loaded]
    updates = [[_dot_tn(v[:, h], (k[:, h] * jnp.exp2(c[n - 1:n, h] - c[:, h])).astype(BF16))
                for h in heads] for _, k, c, v, _ in loaded]
    results = []
    for t, (q, k, c, v, g) in enumerate(loaded):
        outs = []
        for hd, h in enumerate(heads):
            o, states[hd] = _hgrn_head(q[:, h], c[:, h], v[:, h], states[hd],
                                       intras[t][hd * n:(hd + 1) * n, :], levels[t][hd], updates[t][hd])
            outs.append(_rms(o, gain_ref[hd:hd + 1, :]) * g[:, h])
        results.append(jnp.concatenate(outs, axis=1).astype(BF16))
    for r, res in zip(tiles, results):
        o_ref[r, :] = res
    for hd in range(B_HEADS):
        st_ref[hd] = states[hd]


def _hgrn_body(*refs, n_cast):
    hq_ref, hk_ref, hc_ref, hv_ref, hg_ref, gain_ref, s0_ref = refs[:7]
    o_ref, sp_ref, ss_ref = refs[7 + n_cast:10 + n_cast]
    st_ref = refs[-1]
    step = pl.program_id(0)
    tiles = LP // HGRN_TILE
    tile = lax.rem(step, tiles)
    is_prompt = step < BATCH * tiles
    _cast_slabs(step, refs[7:7 + n_cast], refs[10 + n_cast:10 + 2 * n_cast])
    mixer = (hq_ref, hk_ref, hc_ref, hv_ref, hg_ref, gain_ref, o_ref, st_ref)

    @pl.when(jnp.logical_and(is_prompt, tile == 0))
    def _():
        st_ref[...] = jnp.zeros(st_ref.shape, F32)

    @pl.when(is_prompt)
    def _():
        pick = _hgrn_pick(HGRN_ROWS)
        n_tiles = HGRN_TILE // HGRN_ROWS

        def group(n, carry):
            first = n * HGRN_GROUP
            spans = [pl.ds(pl.multiple_of((first + i) * HGRN_ROWS, HGRN_ROWS), HGRN_ROWS)
                     for i in range(HGRN_GROUP)]
            _hgrn_tiles(spans, HGRN_ROWS, *mixer, pick)
            return carry

        lax.fori_loop(0, n_tiles // HGRN_GROUP, group, 0)
        rest = [pl.ds(i * HGRN_ROWS, HGRN_ROWS) for i in range(n_tiles - n_tiles % HGRN_GROUP, n_tiles)]
        if rest:
            _hgrn_tiles(rest, HGRN_ROWS, *mixer, pick)

    @pl.when(jnp.logical_and(is_prompt, tile == tiles - 1))
    def _():
        for hd in range(B_HEADS):
            sp_ref[hd] = st_ref[hd].T

    @pl.when(jnp.logical_not(is_prompt))
    def _():
        pick = _hgrn_pick(DEC_SEQ)
        for b in range(DEC_BATCH):
            for hd in range(B_HEADS):
                st_ref[hd] = s0_ref[b, hd].T
            _hgrn_tiles([pl.ds(b * DEC_SEQ, DEC_SEQ)], DEC_SEQ, *mixer, pick)
            for hd in range(B_HEADS):
                ss_ref[b, hd] = st_ref[hd].T


def _hgrn(hq, hk, hb, hv, hg, gain, s0, ffn_f32, cast_targets):
    tiles = LP // HGRN_TILE
    row = lambda s: (s, 0)
    spec = pl.BlockSpec((HGRN_TILE, B_QK_W), row)
    state = (B_HEADS, B_KEY_DIM, B_VAL_DIM)
    cast_in, cast_out, cast_shapes = _cast_plan(cast_targets, lambda s: s)
    out = pl.pallas_call(
        functools.partial(_hgrn_body, n_cast=len(cast_in)),
        out_shape=(jax.ShapeDtypeStruct((N_ROWS, B_V_W), BF16),
                   jax.ShapeDtypeStruct((BATCH,) + state, F32),
                   jax.ShapeDtypeStruct((DEC_BATCH,) + state, F32), *cast_shapes),
        grid=(BATCH * tiles + 1,),
        in_specs=[spec, spec, spec, spec, spec,
                  pl.BlockSpec((B_HEADS, B_VAL_DIM), lambda s: (0, 0)),
                  pl.BlockSpec((DEC_BATCH,) + state, lambda s: (0, 0, 0, 0)), *cast_in],
        out_specs=(spec,
                   pl.BlockSpec((None,) + state, lambda s: (jnp.minimum(s // tiles, BATCH - 1), 0, 0, 0)),
                   pl.BlockSpec((DEC_BATCH,) + state, lambda s: (0, 0, 0, 0)), *cast_out),
        scratch_shapes=[pltpu.VMEM((B_HEADS, B_VAL_DIM, B_KEY_DIM), F32)],
        compiler_params=_params("arbitrary"),
        name="hgrn",
    )(hq, hk, hb, hv, hg, gain, s0, *(ffn_f32 * len(cast_targets)))
    return out[0], out[1], out[2], out[3:]


def _even_out_body(x_ref, g_ref, oa_ref, ob_ref, w_ref, o_ref):
    for lo in range(0, FFN_TILE, FFN_SUB):
        rows = slice(lo, lo + FFN_SUB)
        m = _dot(oa_ref[rows, :], w_ref[:A_Q_W, :]) + _dot(ob_ref[rows, :], w_ref[A_Q_W:, :])
        y = x_ref[rows, :] + _rms(m, g_ref[3:4, :])
        pad = _is_pad_row(pl.program_id(0) * FFN_TILE + lo, FFN_SUB)
        o_ref[rows, :] = jnp.where(pad, 0.0, y)


def _even_out(x, gains, oa, ob, w_out, j):
    row = lambda i: (i, 0)
    return pl.pallas_call(
        _even_out_body,
        out_shape=jax.ShapeDtypeStruct((N_ROWS, D_MODEL), F32),
        grid=(N_ROWS // FFN_TILE,),
        in_specs=[
            pl.BlockSpec((FFN_TILE, D_MODEL), row),
            pl.BlockSpec((6, D_MODEL), lambda i: (0, 0)),
            pl.BlockSpec((FFN_TILE, A_Q_W), row),
            pl.BlockSpec((FFN_TILE, B_V_W), row),
            pl.BlockSpec((None, A_Q_W + B_V_W, D_MODEL), lambda i: (j, 0, 0),
                         pipeline_mode=pl.Buffered(1)),
        ],
        out_specs=pl.BlockSpec((FFN_TILE, D_MODEL), row),
        input_output_aliases={0: 0},
        compiler_params=_params("parallel"),
        name=f"even_out_{j}",
    )(x, gains, oa, ob, w_out)


def _conv3(u, cw_ref, before1=None, before2=None):
    shift1, shift2 = pltpu.roll(u, 1, 0), pltpu.roll(u, 2, 0)
    if before1 is not None:
        pos = lax.broadcasted_iota(jnp.int32, (u.shape[0], 1), 0)
        shift1 = jnp.where(pos == 0, before1, shift1)
        shift2 = jnp.where(pos == 0, before2, jnp.where(pos == 1, before1, shift2))
    return shift2 * cw_ref[0:1, :] + shift1 * cw_ref[1:2, :] + u * cw_ref[2:3, :]


def _odd_body(*refs, n_cast):
    x_ref, halo_ref, g_ref, w_in_ref, cw_ref, w_out_ref, cache_ref = refs[:7]
    o_ref, tail_ref, new_cache_ref = refs[7 + n_cast:10 + n_cast]
    _cast_slabs(pl.program_id(0), refs[7:7 + n_cast], refs[10 + n_cast:])
    is_prompt = pl.program_id(0) < P_ROWS // ODD_TILE

    @pl.when(is_prompt)
    def _():
        x = x_ref[...]
        h = _rms(x, g_ref[2:3, :]).astype(BF16)
        h_ext = jnp.concatenate([_rms(halo_ref[...], g_ref[2:3, :]).astype(BF16), h], axis=0)
        bg = _dot(h, w_in_ref[:, :D_MODEL])
        u_ext = _dot(h_ext, w_in_ref[:, D_MODEL:2 * D_MODEL]) * _dot(h_ext, w_in_ref[:, 2 * D_MODEL:])
        tail_ref[...] = u_ext[ODD_HALO + ODD_TILE - TAIL_ROWS:, :]
        y = _conv3(u_ext, cw_ref)[ODD_HALO:, :]
        m = _dot((bg * y).astype(BF16), w_out_ref[...])
        out = x + _rms(m, g_ref[3:4, :])
        o_ref[...] = jnp.where(_is_pad_row(pl.program_id(0) * ODD_TILE, ODD_TILE), 0.0, out)

    @pl.when(jnp.logical_not(is_prompt))
    def _():
        x = x_ref[:S_ROWS, :]
        h = _rms(x, g_ref[2:3, :]).astype(BF16)
        bg = _dot(h, w_in_ref[:, :D_MODEL])
        u = _dot(h, w_in_ref[:, D_MODEL:2 * D_MODEL]) * _dot(h, w_in_ref[:, 2 * D_MODEL:])
        ys = []
        for b in range(DEC_BATCH):
            ub = u[b * DEC_SEQ:(b + 1) * DEC_SEQ, :]
            ys.append(_conv3(ub, cw_ref, before1=cache_ref[b, 1:2, :], before2=cache_ref[b, 0:1, :]))
            new_cache_ref[b] = ub[DEC_SEQ - (CONV_WIDTH - 1):, :]
        m = _dot((bg * jnp.concatenate(ys, axis=0)).astype(BF16), w_out_ref[...])
        o_ref[:S_ROWS, :] = x + _rms(m, g_ref[3:4, :])
        tail_ref[...] = jnp.zeros(tail_ref.shape, F32)


def _odd(x, gains, w_in, conv_w, w_out, cache, j, ffn_f32, cast_targets):
    tiles = pl.cdiv(N_ROWS, ODD_TILE)
    row = lambda i: (i, 0)
    per = ODD_TILE // ODD_HALO
    whole3 = lambda i: (0, 0, 0)
    halo = lambda i: (jnp.maximum(i * per - 1, 0), 0)
    cast_in, cast_out, cast_shapes = _cast_plan(cast_targets, lambda i: i)
    out = pl.pallas_call(
        functools.partial(_odd_body, n_cast=len(cast_in)),
        out_shape=(jax.ShapeDtypeStruct((N_ROWS, D_MODEL), F32),
                   jax.ShapeDtypeStruct((tiles, TAIL_ROWS, D_MODEL), F32),
                   jax.ShapeDtypeStruct(cache.shape, F32), *cast_shapes),
        grid=(tiles,),
        in_specs=[
            pl.BlockSpec((ODD_TILE, D_MODEL), row),
            pl.BlockSpec((ODD_HALO, D_MODEL), halo),
            pl.BlockSpec((6, D_MODEL), lambda i: (0, 0)),
            pl.BlockSpec((None, D_MODEL, 3 * D_MODEL), lambda i: (j, 0, 0), pipeline_mode=pl.Buffered(1)),
            pl.BlockSpec((None, CONV_WIDTH, D_MODEL), lambda i: (j, 0, 0)),
            pl.BlockSpec((None, D_MODEL, D_MODEL), lambda i: (j, 0, 0), pipeline_mode=pl.Buffered(1)),
            pl.BlockSpec(cache.shape, whole3),
            *cast_in,
        ],
        out_specs=(pl.BlockSpec((ODD_TILE, D_MODEL), row),
                   pl.BlockSpec((None, TAIL_ROWS, D_MODEL), lambda i: (i, 0, 0)),
                   pl.BlockSpec(cache.shape, whole3), *cast_out),
        compiler_params=_params("arbitrary"),
        name=f"odd_{j}",
    )(x, x, gains, w_in, conv_w, w_out, cache, *(ffn_f32 * len(cast_targets)))
    return out[0], out[1], out[2], out[3:]


def _place_frames_body(x_ref, o_ref):
    o_ref[...] = x_ref[...]


def _place_frames(x_prompt):
    steps = SEQ // PLACE_ROWS
    return pl.pallas_call(
        _place_frames_body,
        out_shape=jax.ShapeDtypeStruct((N_ROWS, D_MODEL), F32),
        grid=(BATCH, steps),
        in_specs=[pl.BlockSpec((None, PLACE_ROWS, D_MODEL), lambda b, t: (b, t, 0))],
        out_specs=pl.BlockSpec((pl.Element(PLACE_ROWS), pl.Element(D_MODEL)),
                               lambda b, t: (pl.multiple_of(b * LP + PAD + N_META + t * PLACE_ROWS, CHUNK), 0)),
        compiler_params=_params("parallel", "parallel"),
        name="place_frames",
    )(x_prompt)


def _place_rest_body(x_ref, lead_ref, xs_ref, o_ref):
    del x_ref
    is_lead = pl.program_id(0) < BATCH

    @pl.when(is_lead)
    def _():
        o_ref[...] = lead_ref[...]

    @pl.when(jnp.logical_not(is_lead))
    def _():
        o_ref[...] = xs_ref[...]


def _place_rest(x, lead, x_sample):
    per_stream, first_sample = LP // CHUNK, P_ROWS // CHUNK
    sample_blocks = S_ROWS // CHUNK
    where = lambda t: (jnp.where(t < BATCH, t * per_stream, first_sample + t - BATCH), 0)
    return pl.pallas_call(
        _place_rest_body,
        out_shape=jax.ShapeDtypeStruct((N_ROWS, D_MODEL), F32),
        grid=(BATCH + sample_blocks,),
        in_specs=[pl.BlockSpec(memory_space=pl.ANY),
                  pl.BlockSpec((CHUNK, D_MODEL), lambda t: (0, 0)),
                  pl.BlockSpec((CHUNK, D_MODEL), lambda t: (jnp.maximum(t - BATCH, 0), 0))],
        out_specs=pl.BlockSpec((CHUNK, D_MODEL), where),
        input_output_aliases={0: 0},
        compiler_params=_params("arbitrary"),
        name="place_rest",
    )(x, lead, x_sample)


def _stream_tails(a, row0, streams, length, n):
    ends = [row0 + (s + 1) * length for s in range(streams)]
    return jnp.stack([lax.slice_in_dim(a, e - n, e, axis=0) for e in ends])


def kernel(x_prompt, x_sample, cache_swa_k, cache_swa_v, state_hgrn, cache_conv, meta_tokens, norm_gains,
           w_ffn_gate, w_ffn_up, w_ffn_down, w_in_even, w_out_even, attn_sinks, hgrn_lb_logits,
           hgrn_norm_gain, w_in_odd, conv_w, w_out_odd):
    win = cache_swa_k.shape[2]
    lead = jnp.concatenate([jnp.zeros((PAD, D_MODEL), F32), meta_tokens.astype(F32)], axis=0)
    x = _place_rest(_place_frames(x_prompt), lead, x_sample.reshape(S_ROWS, D_MODEL))

    ffn_f32 = (w_ffn_gate, w_ffn_up, w_ffn_down)
    ffn_w = {(0, 0): (w_ffn_gate[0, 0].astype(BF16), w_ffn_up[0, 0].astype(BF16),
                      w_ffn_down[0, 0].astype(BF16))}

    def keep_converted(targets, converted):
        for n, key in enumerate(targets):
            ffn_w[key] = tuple(converted[3 * n:3 * n + 3])

    w_in_e, w_out_e = w_in_even.astype(BF16), w_out_even.astype(BF16)
    w_in_o, w_out_o = w_in_odd.astype(BF16), w_out_odd.astype(BF16)
    cache_k = cache_swa_k.reshape(N_EVEN, DEC_BATCH, win, A_KV_W)
    cache_v = cache_swa_v.reshape(N_EVEN, DEC_BATCH, win, A_KV_W)

    k_p, v_p, s_p, c_p, k_s, v_s, s_s, c_s = ([] for _ in range(8))
    for layer in range(DEPTH):
        gains = norm_gains[layer]
        j = layer // 2
        x = _ffn(x, norm_gains, ffn_w[(layer, 0)], layer, 0)
        targets = [(layer, 1)] + ([(layer + 1, 0)] if layer + 1 < DEPTH else [])
        if layer % 2 == 0:
            q, k, v, kk, vv, hq, hk, hb, hv, hg = _even_in(x, gains, w_in_e, hgrn_lb_logits, j)
            oa = _attn(attn_sinks[j], q, kk, vv, cache_k[j], cache_v[j])
            ob, st_p, st_s, converted = _hgrn(hq, hk, hb, hv, hg, hgrn_norm_gain[j], state_hgrn[j],
                                              ffn_f32, targets)
            keep_converted(targets, converted)
            x = _even_out(x, gains, oa, ob, w_out_e, j)
            for new, tail_p, tail_s, cache in ((k, k_p, k_s, cache_k[j]), (v, v_p, v_s, cache_v[j])):
                tail_p.append(_stream_tails(new, 0, BATCH, LP, WINDOW).reshape(
                    BATCH, WINDOW, A_KV_HEADS, A_HEAD_DIM))
                both = jnp.concatenate([cache, new[P_ROWS:].reshape(DEC_BATCH, DEC_SEQ, A_KV_W)], axis=1)
                tail_s.append(both[:, both.shape[1] - win:].reshape(DEC_BATCH, win, A_KV_HEADS, A_HEAD_DIM))
            s_p.append(st_p)
            s_s.append(st_s)
        else:
            x, tails, conv_s, converted = _odd(x, gains, w_in_o, conv_w, w_out_o, cache_conv[j], j,
                                               ffn_f32, targets)
            keep_converted(targets, converted)
            per_stream = LP // ODD_TILE
            c_p.append(jnp.stack([tails[(b + 1) * per_stream - 1, TAIL_ROWS - (CONV_WIDTH - 1):]
                                  for b in range(BATCH)]))
            c_s.append(conv_s)
        x = _ffn(x, norm_gains, ffn_w[(layer, 1)], layer, 1)

    y_prompt = _stream_tails(x, 0, BATCH, LP, SEQ)
    y_sample = x[P_ROWS:].reshape(DEC_BATCH, DEC_SEQ, D_MODEL)
    return (y_prompt, y_sample, jnp.stack(k_p), jnp.stack(v_p), jnp.stack(s_p), jnp.stack(c_p),
            jnp.stack(k_s), jnp.stack(v_s), jnp.stack(s_s), jnp.stack(c_s))
```

```python
import functools

import jax
import jax.numpy as jnp
from jax import lax
from jax.experimental import pallas as pl
from jax.experimental.pallas import tpu as pltpu

F32 = jnp.float32
BF16 = jnp.bfloat16

D_MODEL = 1024
BATCH = 4
SEQ = 4096
DEPTH = 4
DEC_BATCH = 8
DEC_SEQ = 32
CHUNK = 64
N_META = 16
WINDOW = 128
WINDOW_CHUNKS = WINDOW // CHUNK
A_HEADS = 8
A_KV_HEADS = 2
A_HEAD_DIM = 64
A_Q_W = A_HEADS * A_HEAD_DIM
A_KV_W = A_KV_HEADS * A_HEAD_DIM
B_HEADS = 4
B_KEY_DIM = 128
B_VAL_DIM = 128
B_QK_W = B_HEADS * B_KEY_DIM
B_V_W = B_HEADS * B_VAL_DIM
B_BLOCK = 16
CONV_WIDTH = 3
D_FF = 2816
FFN_RESIDUAL = 0.5
N_EVEN = (DEPTH + 1) // 2
N_ODD = DEPTH // 2
EVEN_IN_W = A_Q_W + 2 * A_KV_W + 2 * B_QK_W + 2 * B_V_W
EPS = 1e-6
MASK_VALUE = -1e30
LB_FLOOR = 1e-30

PAD = (-N_META) % CHUNK
LP = PAD + N_META + SEQ
N_CHUNKS = LP // CHUNK
P_ROWS = BATCH * LP
S_ROWS = DEC_BATCH * DEC_SEQ
N_ROWS = P_ROWS + S_ROWS

ROW_TILE = 512
FF_CHUNK = 256
HGRN_TILE = 832
HGRN_ROWS = 64
HGRN_SUB = 8
HGRN_GROUP = 3
LOG2_E = 1.4426950408889634
CUMSUM_ROWS = 256
FFN_TILE = 1536
FFN_SUB = 512
ATTN_CHUNKS = 5
PLACE_ROWS = 1024
CAST_SLAB = 256
CAST_STEPS = D_FF // CAST_SLAB
ODD_TILE = 832
ODD_HALO = 16
TAIL_ROWS = 8
VMEM_LIMIT = 56 * 1024 * 1024

assert N_ROWS % ROW_TILE == 0 and D_FF % FF_CHUNK == 0
assert LP % HGRN_TILE == 0 and HGRN_TILE % HGRN_ROWS == 0 and ROW_TILE % HGRN_ROWS == 0
assert HGRN_ROWS % DEC_SEQ == 0 and DEC_SEQ % HGRN_SUB == 0
assert ROW_TILE % CUMSUM_ROWS == 0 and CUMSUM_ROWS % HGRN_ROWS == 0
assert N_ROWS % FFN_TILE == 0 and FFN_TILE % FFN_SUB == 0 and N_CHUNKS % ATTN_CHUNKS == 0
assert D_FF % CAST_SLAB == 0 and CAST_STEPS <= min(P_ROWS // ODD_TILE, P_ROWS // HGRN_TILE)
assert LP % ODD_TILE == 0 and ODD_TILE % ODD_HALO == 0 and P_ROWS % DEC_SEQ == 0
assert P_ROWS % S_ROWS == 0 and S_ROWS <= ODD_TILE and TAIL_ROWS >= CONV_WIDTH - 1


def _params(*sem):
    return pltpu.CompilerParams(dimension_semantics=sem, vmem_limit_bytes=VMEM_LIMIT)


def _rms(x, gain):
    return x * lax.rsqrt(jnp.mean(x * x, axis=-1, keepdims=True) + EPS) * gain


def _silu(x):
    return x * jax.nn.sigmoid(x)


def _dot(a, b):
    return jnp.dot(a, b, preferred_element_type=F32)


def _dot_nt(a, b):
    return lax.dot_general(a, b, (((1,), (1,)), ((), ())), preferred_element_type=F32)


def _dot_tn(a, b):
    return lax.dot_general(a, b, (((0,), (0,)), ((), ())), preferred_element_type=F32)


def _is_pad_row(row0, rows):
    r = row0 + lax.broadcasted_iota(jnp.int32, (rows, 1), 0)
    pad = None
    for b in range(BATCH):
        hit = jnp.logical_and(r >= b * LP, r < b * LP + PAD)
        pad = hit if pad is None else jnp.logical_or(pad, hit)
    return pad


def _ffn_body(sel_ref, x_ref, g_ref, wg_ref, wu_ref, wd_ref, o_ref):
    pre = sel_ref[1] * 4
    post = pre + 1
    pieces = [slice(lo, lo + FFN_SUB) for lo in range(0, FFN_TILE, FFN_SUB)]
    chunks = [slice(c * FF_CHUNK, (c + 1) * FF_CHUNK) for c in range(D_FF // FF_CHUNK)]
    xs = [x_ref[rows, :] for rows in pieces]
    hs = [_rms(x, g_ref[pl.ds(pre, 1), :]).astype(BF16) for x in xs]
    accs = []
    for h in hs:
        acc = jnp.zeros((FFN_SUB, D_MODEL), F32)
        for cols in chunks:
            act = (_silu(_dot(h, wg_ref[:, cols])) * _dot(h, wu_ref[:, cols])).astype(BF16)
            acc = acc + _dot(act, wd_ref[cols, :])
        accs.append(acc)
    for rows, x, acc in zip(pieces, xs, accs):
        o_ref[rows, :] = x + FFN_RESIDUAL * _rms(acc, g_ref[pl.ds(post, 1), :])


def _ffn(x, norm_gains, weights, layer, which):
    row = lambda i, sel: (i, 0)
    whole = lambda i, sel: (0, 0)
    resident = pl.Buffered(1)
    grid_spec = pltpu.PrefetchScalarGridSpec(
        num_scalar_prefetch=1,
        grid=(N_ROWS // FFN_TILE,),
        in_specs=[
            pl.BlockSpec((FFN_TILE, D_MODEL), row),
            pl.BlockSpec((None, 6, D_MODEL), lambda i, sel: (sel[0], 0, 0)),
            pl.BlockSpec((D_MODEL, D_FF), whole, pipeline_mode=resident),
            pl.BlockSpec((D_MODEL, D_FF), whole, pipeline_mode=resident),
            pl.BlockSpec((D_FF, D_MODEL), whole, pipeline_mode=resident),
        ],
        out_specs=pl.BlockSpec((FFN_TILE, D_MODEL), row),
    )
    return pl.pallas_call(
        _ffn_body,
        out_shape=jax.ShapeDtypeStruct((N_ROWS, D_MODEL), F32),
        grid_spec=grid_spec,
        input_output_aliases={1: 0},
        compiler_params=_params("parallel"),
        name="ffn",
    )(jnp.array([layer, which], jnp.int32), x, norm_gains, *weights)


def _cast_plan(targets, step_of):
    slab = lambda *ids: jnp.minimum(step_of(*ids), CAST_STEPS - 1)
    in_specs, out_specs, shapes = [], [], []
    for layer, which in targets:
        for block, shape, place in (((D_MODEL, CAST_SLAB), (D_MODEL, D_FF), lambda s: (0, s)),
                                    ((D_MODEL, CAST_SLAB), (D_MODEL, D_FF), lambda s: (0, s)),
                                    ((CAST_SLAB, D_MODEL), (D_FF, D_MODEL), lambda s: (s, 0))):
            in_specs.append(pl.BlockSpec((None, None) + block,
                                         lambda *ids, l=layer, w=which, p=place: (l, w) + p(slab(*ids))))
            out_specs.append(pl.BlockSpec(block, lambda *ids, p=place: p(slab(*ids))))
            shapes.append(jax.ShapeDtypeStruct(shape, BF16))
    return in_specs, out_specs, shapes


def _cast_slabs(step, srcs, dsts):
    @pl.when(step < CAST_STEPS)
    def _():
        for src, dst in zip(srcs, dsts):
            dst[...] = src[...].astype(BF16)


def _lower_bound(lbl_ref, j):
    rows = [lbl_ref[i:i + 1, :] for i in range(N_EVEN)]
    top = functools.reduce(jnp.maximum, rows)
    e = [jnp.exp(r - top) for r in rows]
    total = functools.reduce(lambda a, b: a + b, e)
    p = [v / total for v in e]
    cum = functools.reduce(lambda a, b: a + b, p[:j + 1])
    return jnp.maximum(cum - p[0], 0.0)


def _span_cumsum(g, row0):
    out = []
    for lo in range(0, g.shape[0], CUMSUM_ROWS):
        part = g[lo:lo + CUMSUM_ROWS, :]
        t = row0 + lo + lax.broadcasted_iota(jnp.int32, (CUMSUM_ROWS, CUMSUM_ROWS), 0)
        u = row0 + lo + lax.broadcasted_iota(jnp.int32, (CUMSUM_ROWS, CUMSUM_ROWS), 1)
        span = jnp.where(t >= P_ROWS, DEC_SEQ, HGRN_ROWS)
        tri = jnp.where(jnp.logical_and((t ^ u) < span, u <= t), 1.0, 0.0).astype(BF16)
        total = None
        for _ in range(3):
            term = part.astype(BF16)
            part = part - term.astype(F32)
            total = _dot(tri, term) if total is None else total + _dot(tri, term)
        out.append(total)
    return jnp.concatenate(out, axis=0)


def _even_in_body(x_ref, g_ref, w_ref, lbl_ref, q_ref, k_ref, v_ref, kk_ref, vv_ref,
                  hq_ref, hk_ref, hb_ref, hv_ref, hg_ref, *, j):
    h = _rms(x_ref[...], g_ref[2:3, :]).astype(BF16)
    widths = (A_Q_W, 2 * A_KV_W, B_QK_W, B_QK_W, B_V_W, B_V_W)
    starts = [sum(widths[:i]) for i in range(len(widths))]
    proj = lambda i: _dot(h, w_ref[:, starts[i]:starts[i] + widths[i]])
    fx = proj(3)
    q_ref[...] = (proj(0) * (A_HEAD_DIM ** -0.5)).astype(BF16)
    kv = proj(1)
    k_ref[...] = kv[:, :A_KV_W]
    v_ref[...] = kv[:, A_KV_W:]
    kk_ref[...] = jnp.concatenate(_half_lane_variants(kv[:, :A_KV_W]), axis=1)
    vv_ref[...] = jnp.concatenate(_half_lane_variants(kv[:, A_KV_W:]), axis=1)
    hq_ref[...] = _silu(proj(2))
    hv_ref[...] = proj(4).astype(BF16)
    hg_ref[...] = _silu(proj(5))
    lb = _lower_bound(lbl_ref, j)
    soft = jnp.log1p(jnp.exp(-jnp.abs(fx)))
    ls_pos = jnp.minimum(fx, 0.0) - soft
    ls_neg = jnp.minimum(-fx, 0.0) - soft
    other = jnp.log(jnp.maximum(lb, LB_FLOOR)) + ls_neg
    log_f = jnp.maximum(ls_pos, other) + jnp.log1p(jnp.exp(-jnp.abs(ls_pos - other)))
    hk_ref[...] = (1.0 - lb) * jax.nn.sigmoid(-fx)
    hb_ref[...] = _span_cumsum(log_f, pl.program_id(0) * ROW_TILE)


def _even_in(x, gains, w_in, lb_logits, j):
    row = lambda i: (i, 0)
    whole = lambda i: (0, 0)
    wide = lambda w, dt: jax.ShapeDtypeStruct((N_ROWS, w), dt)
    spec = lambda w: pl.BlockSpec((ROW_TILE, w), row)
    return pl.pallas_call(
        functools.partial(_even_in_body, j=j),
        out_shape=(wide(A_Q_W, BF16), wide(A_KV_W, F32), wide(A_KV_W, F32),
                   wide(4 * A_KV_W, BF16), wide(4 * A_KV_W, BF16),
                   wide(B_QK_W, F32), wide(B_QK_W, F32), wide(B_QK_W, F32),
                   wide(B_V_W, BF16), wide(B_V_W, F32)),
        grid=(N_ROWS // ROW_TILE,),
        in_specs=[
            pl.BlockSpec((ROW_TILE, D_MODEL), row),
            pl.BlockSpec((6, D_MODEL), whole),
            pl.BlockSpec((None, D_MODEL, EVEN_IN_W), lambda i: (j, 0, 0),
                         pipeline_mode=pl.Buffered(1)),
            pl.BlockSpec((N_EVEN, B_QK_W), whole),
        ],
        out_specs=(spec(A_Q_W), spec(A_KV_W), spec(A_KV_W), spec(4 * A_KV_W), spec(4 * A_KV_W),
                   spec(B_QK_W), spec(B_QK_W), spec(B_QK_W), spec(B_V_W), spec(B_V_W)),
        compiler_params=_params("parallel"),
        name=f"even_in_{j}",
    )(x, gains, w_in, lb_logits)


def _half_lane_variants(a):
    lane = lax.broadcasted_iota(jnp.int32, a.shape, 1)
    low = lane < A_HEAD_DIM
    swapped = pltpu.roll(a, A_HEAD_DIM, 1)
    zero = jnp.zeros_like(a)
    return (jnp.where(low, a, zero).astype(BF16), jnp.where(low, zero, swapped).astype(BF16),
            jnp.where(low, swapped, zero).astype(BF16), jnp.where(low, zero, a).astype(BF16))


def _sink_softmax(s, valid, sink):
    if valid is not None:
        s = jnp.where(valid, s, MASK_VALUE)
    m = jnp.maximum(jnp.max(s, axis=-1, keepdims=True), sink)
    p = jnp.exp(s - m)
    denom = jnp.sum(p, axis=-1, keepdims=True) + jnp.exp(sink - m)
    return (p / denom).astype(BF16)


def _group_sinks(sink_ref, kvh, rows):
    first = lax.broadcasted_iota(jnp.int32, (2 * rows, 1), 0) < rows
    base = kvh * (A_HEADS // A_KV_HEADS)
    return (jnp.where(first, sink_ref[base], sink_ref[base + 2]),
            jnp.where(first, sink_ref[base + 1], sink_ref[base + 3]))


def _attend(items, sinks):
    scores = []
    for q, keys, _, _ in items:
        for kvh in range(A_KV_HEADS):
            c0 = kvh * 256
            q2 = jnp.concatenate([q[:, c0:c0 + 128], q[:, c0 + 128:c0 + 256]], axis=0)
            scores += [_dot_nt(q2, keys[2 * kvh]), _dot_nt(q2, keys[2 * kvh + 1])]
    weights = []
    for n, (_, _, _, valid) in enumerate(items):
        for kvh in range(A_KV_HEADS):
            for half in range(2):
                weights.append(_sink_softmax(scores[4 * n + 2 * kvh + half], valid, sinks[kvh][half]))
    outs = []
    for n, (q, _, vals, _) in enumerate(items):
        rows = q.shape[0]
        groups = []
        for kvh in range(A_KV_HEADS):
            w = weights[4 * n + 2 * kvh:4 * n + 2 * kvh + 2]
            o = _dot(w[0], vals[2 * kvh]) + _dot(w[1], vals[2 * kvh + 1])
            groups.append(jnp.concatenate([o[:rows], o[rows:]], axis=1))
        outs.append(jnp.concatenate(groups, axis=1))
    return outs


def _attn_body(sink_ref, q_ref, kk_ref, vv_ref, ck_ref, cv_ref, o_ref):
    variant = [slice(v * A_KV_W, (v + 1) * A_KV_W) for v in range(4)]

    @pl.when(pl.program_id(0) < BATCH)
    def _():
        _attn_prompt_stream(sink_ref, q_ref, kk_ref, vv_ref, o_ref, variant)

    @pl.when(pl.program_id(0) == BATCH)
    def _():
        sinks = [_group_sinks(sink_ref, kvh, DEC_SEQ) for kvh in range(A_KV_HEADS)]
        items = []
        for b in range(DEC_BATCH):
            rows = slice(b * DEC_SEQ, (b + 1) * DEC_SEQ)
            old_k, old_v = _half_lane_variants(ck_ref[b]), _half_lane_variants(cv_ref[b])
            keys = [jnp.concatenate([old_k[v], kk_ref[rows, variant[v]]], axis=0) for v in range(4)]
            vals = [jnp.concatenate([old_v[v], vv_ref[rows, variant[v]]], axis=0) for v in range(4)]
            items.append((q_ref[rows, :], keys, vals, None))
        for b, out in enumerate(_attend(items, sinks)):
            o_ref[b * DEC_SEQ:(b + 1) * DEC_SEQ, :] = out.astype(BF16)


def _attn_prompt_stream(sink_ref, q_ref, kk_ref, vv_ref, o_ref, variant):
    span = (WINDOW_CHUNKS + 1) * CHUNK
    sinks = [_group_sinks(sink_ref, kvh, CHUNK) for kvh in range(A_KV_HEADS)]

    def step(n, carry):
        starts, items = [], []
        for i in range(ATTN_CHUNKS):
            c = n * ATTN_CHUNKS + i
            r0 = pl.multiple_of(c * CHUNK, CHUNK)
            s0 = pl.multiple_of(jnp.maximum(c - WINDOW_CHUNKS, 0) * CHUNK, CHUNK)
            kpos = s0 + lax.broadcasted_iota(jnp.int32, (1, span), 1)
            valid = jnp.logical_and(kpos >= PAD, kpos < r0 + CHUNK)
            starts.append(r0)
            items.append((q_ref[pl.ds(r0, CHUNK), :],
                          [kk_ref[pl.ds(s0, span), v] for v in variant],
                          [vv_ref[pl.ds(s0, span), v] for v in variant], valid))
        for r0, out in zip(starts, _attend(items, sinks)):
            o_ref[pl.ds(r0, CHUNK), :] = out.astype(BF16)
        return carry

    lax.fori_loop(0, N_CHUNKS // ATTN_CHUNKS, step, 0)


def _attn(sink, q, kk, vv, cache_k, cache_v):
    seq = lambda b: (b, 0)
    old = pl.BlockSpec(cache_k.shape, lambda b: (0, 0, 0))
    return pl.pallas_call(
        _attn_body,
        out_shape=jax.ShapeDtypeStruct((N_ROWS, A_Q_W), BF16),
        grid=(BATCH + 1,),
        in_specs=[
            pl.BlockSpec(memory_space=pltpu.SMEM),
            pl.BlockSpec((LP, A_Q_W), seq),
            pl.BlockSpec((LP, 4 * A_KV_W), seq),
            pl.BlockSpec((LP, 4 * A_KV_W), seq),
            old, old,
        ],
        out_specs=pl.BlockSpec((LP, A_Q_W), seq),
        compiler_params=_params("parallel"),
        name="attn",
    )(sink, q, kk, vv, cache_k, cache_v)


def _group_row(a, group, row):
    spans = a.shape[0] // group
    picked = a.reshape(spans, group, a.shape[1])[:, row:row + 1, :]
    return jnp.broadcast_to(picked, (spans, group, a.shape[1])).reshape(a.shape)


def _hgrn_pairs(q, k, c2):
    parts = []
    for s in range(HGRN_SUB):
        decay = jnp.exp2(jnp.minimum(c2 - _group_row(c2, HGRN_SUB, s), 0.0))
        parts.append((q * _group_row(k, HGRN_SUB, s) * decay).astype(BF16))
    return jnp.concatenate(parts, axis=1)


def _hgrn_pick(rows):
    shape = (HGRN_SUB * B_KEY_DIM, rows)
    block = lax.broadcasted_iota(jnp.int32, shape, 0) // B_KEY_DIM
    col = lax.broadcasted_iota(jnp.int32, shape, 1) & (HGRN_SUB - 1)
    return jnp.where(block == col, 1.0, 0.0).astype(BF16)


def _hgrn_head(q, c, v, state_t, intra, levels, update):
    rows = q.shape[0]
    t_idx = lax.broadcasted_iota(jnp.int32, (rows, rows), 0)
    u_idx = lax.broadcasted_iota(jnp.int32, (rows, rows), 1)
    apart = t_idx ^ u_idx
    in_group = jnp.logical_and(apart < HGRN_SUB, u_idx <= t_idx)
    scores = jnp.where(in_group, intra, 0.0)
    for w, level in levels:
        scores = scores + (level if 2 * w == rows else jnp.where(apart < 2 * w, level, 0.0))
    o = _dot(scores.astype(BF16), v) + _dot_nt((q * jnp.exp2(c)).astype(BF16), state_t.astype(BF16))
    return o, state_t * jnp.exp2(c[rows - 1:rows, :]) + update


def _hgrn_levels(q, k, c):
    rows = q.shape[0]
    pos = lax.broadcasted_iota(jnp.int32, q.shape, 0)
    levels = []
    w = HGRN_SUB
    while w < rows:
        rho = _group_row(c, 2 * w, w - 1)
        upper = (pos & w) != 0
        q_up = jnp.where(upper, q * jnp.exp2(jnp.minimum(c - rho, 0.0)), 0.0).astype(BF16)
        k_low = jnp.where(upper, 0.0, k * jnp.exp2(jnp.minimum(rho - c, 0.0))).astype(BF16)
        levels.append((w, _dot_nt(q_up, k_low)))
        w *= 2
    return levels


def _hgrn_tiles(tiles, n, hq_ref, hk_ref, hc_ref, hv_ref, hg_ref, gain_ref, o_ref, st_ref, pick):
    loaded = [(hq_ref[r, :], hk_ref[r, :], hc_ref[r, :] * LOG2_E, hv_ref[r, :], hg_ref[r, :])
              for r in tiles]
    states = [st_ref[hd] for hd in range(B_HEADS)]
    heads = [slice(hd * B_KEY_DIM, (hd + 1) * B_KEY_DIM) for hd in range(B_HEADS)]
    intras = [_dot(jnp.concatenate([_hgrn_pairs(q[:, h], k[:, h], c[:, h]) for h in heads], axis=0), pick)
              for q, k, c, _, _ in loaded]
    levels = [[_hgrn_levels(q[:, h], k[:, h], c[:, h]) for h in heads] for q, k, c, _, _ in loaded]
    updates = [[_dot_tn(v[:, h], (k[:, h] * jnp.exp2(c[n - 1:n, h] - c[:, h])).astype(BF16))
                for h in heads] for _, k, c, v, _ in loaded]
    results = []
    for t, (q, k, c, v, g) in enumerate(loaded):
        outs = []
        for hd, h in enumerate(heads):
            o, states[hd] = _hgrn_head(q[:, h], c[:, h], v[:, h], states[hd],
                                       intras[t][hd * n:(hd + 1) * n, :], levels[t][hd], updates[t][hd])
            outs.append(_rms(o, gain_ref[hd:hd + 1, :]) * g[:, h])
        results.append(jnp.concatenate(outs, axis=1).astype(BF16))
    for r, res in zip(tiles, results):
        o_ref[r, :] = res
    for hd in range(B_HEADS):
        st_ref[hd] = states[hd]


def _hgrn_body(*refs, n_cast):
    hq_ref, hk_ref, hc_ref, hv_ref, hg_ref, gain_ref, s0_ref = refs[:7]
    o_ref, sp_ref, ss_ref = refs[7 + n_cast:10 + n_cast]
    st_ref = refs[-1]
    step = pl.program_id(0)
    tiles = LP // HGRN_TILE
    tile = lax.rem(step, tiles)
    is_prompt = step < BATCH * tiles
    _cast_slabs(step, refs[7:7 + n_cast], refs[10 + n_cast:10 + 2 * n_cast])
    mixer = (hq_ref, hk_ref, hc_ref, hv_ref, hg_ref, gain_ref, o_ref, st_ref)

    @pl.when(jnp.logical_and(is_prompt, tile == 0))
    def _():
        st_ref[...] = jnp.zeros(st_ref.shape, F32)

    @pl.when(is_prompt)
    def _():
        pick = _hgrn_pick(HGRN_ROWS)
        n_tiles = HGRN_TILE // HGRN_ROWS

        def group(n, carry):
            first = n * HGRN_GROUP
            spans = [pl.ds(pl.multiple_of((first + i) * HGRN_ROWS, HGRN_ROWS), HGRN_ROWS)
                     for i in range(HGRN_GROUP)]
            _hgrn_tiles(spans, HGRN_ROWS, *mixer, pick)
            return carry

        lax.fori_loop(0, n_tiles // HGRN_GROUP, group, 0)
        rest = [pl.ds(i * HGRN_ROWS, HGRN_ROWS) for i in range(n_tiles - n_tiles % HGRN_GROUP, n_tiles)]
        if rest:
            _hgrn_tiles(rest, HGRN_ROWS, *mixer, pick)

    @pl.when(jnp.logical_and(is_prompt, tile == tiles - 1))
    def _():
        for hd in range(B_HEADS):
            sp_ref[hd] = st_ref[hd].T

    @pl.when(jnp.logical_not(is_prompt))
    def _():
        pick = _hgrn_pick(DEC_SEQ)
        for b in range(DEC_BATCH):
            for hd in range(B_HEADS):
                st_ref[hd] = s0_ref[b, hd].T
            _hgrn_tiles([pl.ds(b * DEC_SEQ, DEC_SEQ)], DEC_SEQ, *mixer, pick)
            for hd in range(B_HEADS):
                ss_ref[b, hd] = st_ref[hd].T


def _hgrn(hq, hk, hb, hv, hg, gain, s0, ffn_f32, cast_targets):
    tiles = LP // HGRN_TILE
    row = lambda s: (s, 0)
    spec = pl.BlockSpec((HGRN_TILE, B_QK_W), row)
    state = (B_HEADS, B_KEY_DIM, B_VAL_DIM)
    cast_in, cast_out, cast_shapes = _cast_plan(cast_targets, lambda s: s)
    out = pl.pallas_call(
        functools.partial(_hgrn_body, n_cast=len(cast_in)),
        out_shape=(jax.ShapeDtypeStruct((N_ROWS, B_V_W), BF16),
                   jax.ShapeDtypeStruct((BATCH,) + state, F32),
                   jax.ShapeDtypeStruct((DEC_BATCH,) + state, F32), *cast_shapes),
        grid=(BATCH * tiles + 1,),
        in_specs=[spec, spec, spec, spec, spec,
                  pl.BlockSpec((B_HEADS, B_VAL_DIM), lambda s: (0, 0)),
                  pl.BlockSpec((DEC_BATCH,) + state, lambda s: (0, 0, 0, 0)), *cast_in],
        out_specs=(spec,
                   pl.BlockSpec((None,) + state, lambda s: (jnp.minimum(s // tiles, BATCH - 1), 0, 0, 0)),
                   pl.BlockSpec((DEC_BATCH,) + state, lambda s: (0, 0, 0, 0)), *cast_out),
        scratch_shapes=[pltpu.VMEM((B_HEADS, B_VAL_DIM, B_KEY_DIM), F32)],
        compiler_params=_params("arbitrary"),
        name="hgrn",
    )(hq, hk, hb, hv, hg, gain, s0, *(ffn_f32 * len(cast_targets)))
    return out[0], out[1], out[2], out[3:]


def _even_out_body(x_ref, g_ref, oa_ref, ob_ref, w_ref, o_ref):
    for lo in range(0, FFN_TILE, FFN_SUB):
        rows = slice(lo, lo + FFN_SUB)
        m = _dot(oa_ref[rows, :], w_ref[:A_Q_W, :]) + _dot(ob_ref[rows, :], w_ref[A_Q_W:, :])
        y = x_ref[rows, :] + _rms(m, g_ref[3:4, :])
        pad = _is_pad_row(pl.program_id(0) * FFN_TILE + lo, FFN_SUB)
        o_ref[rows, :] = jnp.where(pad, 0.0, y)


def _even_out(x, gains, oa, ob, w_out, j):
    row = lambda i: (i, 0)
    return pl.pallas_call(
        _even_out_body,
        out_shape=jax.ShapeDtypeStruct((N_ROWS, D_MODEL), F32),
        grid=(N_ROWS // FFN_TILE,),
        in_specs=[
            pl.BlockSpec((FFN_TILE, D_MODEL), row),
            pl.BlockSpec((6, D_MODEL), lambda i: (0, 0)),
            pl.BlockSpec((FFN_TILE, A_Q_W), row),
            pl.BlockSpec((FFN_TILE, B_V_W), row),
            pl.BlockSpec((None, A_Q_W + B_V_W, D_MODEL), lambda i: (j, 0, 0),
                         pipeline_mode=pl.Buffered(1)),
        ],
        out_specs=pl.BlockSpec((FFN_TILE, D_MODEL), row),
        input_output_aliases={0: 0},
        compiler_params=_params("parallel"),
        name=f"even_out_{j}",
    )(x, gains, oa, ob, w_out)


def _conv3(u, cw_ref, before1=None, before2=None):
    shift1, shift2 = pltpu.roll(u, 1, 0), pltpu.roll(u, 2, 0)
    if before1 is not None:
        pos = lax.broadcasted_iota(jnp.int32, (u.shape[0], 1), 0)
        shift1 = jnp.where(pos == 0, before1, shift1)
        shift2 = jnp.where(pos == 0, before2, jnp.where(pos == 1, before1, shift2))
    return shift2 * cw_ref[0:1, :] + shift1 * cw_ref[1:2, :] + u * cw_ref[2:3, :]


def _odd_body(*refs, n_cast):
    x_ref, halo_ref, g_ref, w_in_ref, cw_ref, w_out_ref, cache_ref = refs[:7]
    o_ref, tail_ref, new_cache_ref = refs[7 + n_cast:10 + n_cast]
    _cast_slabs(pl.program_id(0), refs[7:7 + n_cast], refs[10 + n_cast:])
    is_prompt = pl.program_id(0) < P_ROWS // ODD_TILE

    @pl.when(is_prompt)
    def _():
        x = x_ref[...]
        h = _rms(x, g_ref[2:3, :]).astype(BF16)
        h_ext = jnp.concatenate([_rms(halo_ref[...], g_ref[2:3, :]).astype(BF16), h], axis=0)
        bg = _dot(h, w_in_ref[:, :D_MODEL])
        u_ext = _dot(h_ext, w_in_ref[:, D_MODEL:2 * D_MODEL]) * _dot(h_ext, w_in_ref[:, 2 * D_MODEL:])
        tail_ref[...] = u_ext[ODD_HALO + ODD_TILE - TAIL_ROWS:, :]
        y = _conv3(u_ext, cw_ref)[ODD_HALO:, :]
        m = _dot((bg * y).astype(BF16), w_out_ref[...])
        out = x + _rms(m, g_ref[3:4, :])
        o_ref[...] = jnp.where(_is_pad_row(pl.program_id(0) * ODD_TILE, ODD_TILE), 0.0, out)

    @pl.when(jnp.logical_not(is_prompt))
    def _():
        x = x_ref[:S_ROWS, :]
        h = _rms(x, g_ref[2:3, :]).astype(BF16)
        bg = _dot(h, w_in_ref[:, :D_MODEL])
        u = _dot(h, w_in_ref[:, D_MODEL:2 * D_MODEL]) * _dot(h, w_in_ref[:, 2 * D_MODEL:])
        ys = []
        for b in range(DEC_BATCH):
            ub = u[b * DEC_SEQ:(b + 1) * DEC_SEQ, :]
            ys.append(_conv3(ub, cw_ref, before1=cache_ref[b, 1:2, :], before2=cache_ref[b, 0:1, :]))
            new_cache_ref[b] = ub[DEC_SEQ - (CONV_WIDTH - 1):, :]
        m = _dot((bg * jnp.concatenate(ys, axis=0)).astype(BF16), w_out_ref[...])
        o_ref[:S_ROWS, :] = x + _rms(m, g_ref[3:4, :])
        tail_ref[...] = jnp.zeros(tail_ref.shape, F32)


def _odd(x, gains, w_in, conv_w, w_out, cache, j, ffn_f32, cast_targets):
    tiles = pl.cdiv(N_ROWS, ODD_TILE)
    row = lambda i: (i, 0)
    per = ODD_TILE // ODD_HALO
    whole3 = lambda i: (0, 0, 0)
    halo = lambda i: (jnp.maximum(i * per - 1, 0), 0)
    cast_in, cast_out, cast_shapes = _cast_plan(cast_targets, lambda i: i)
    out = pl.pallas_call(
        functools.partial(_odd_body, n_cast=len(cast_in)),
        out_shape=(jax.ShapeDtypeStruct((N_ROWS, D_MODEL), F32),
                   jax.ShapeDtypeStruct((tiles, TAIL_ROWS, D_MODEL), F32),
                   jax.ShapeDtypeStruct(cache.shape, F32), *cast_shapes),
        grid=(tiles,),
        in_specs=[
            pl.BlockSpec((ODD_TILE, D_MODEL), row),
            pl.BlockSpec((ODD_HALO, D_MODEL), halo),
            pl.BlockSpec((6, D_MODEL), lambda i: (0, 0)),
            pl.BlockSpec((None, D_MODEL, 3 * D_MODEL), lambda i: (j, 0, 0), pipeline_mode=pl.Buffered(1)),
            pl.BlockSpec((None, CONV_WIDTH, D_MODEL), lambda i: (j, 0, 0)),
            pl.BlockSpec((None, D_MODEL, D_MODEL), lambda i: (j, 0, 0), pipeline_mode=pl.Buffered(1)),
            pl.BlockSpec(cache.shape, whole3),
            *cast_in,
        ],
        out_specs=(pl.BlockSpec((ODD_TILE, D_MODEL), row),
                   pl.BlockSpec((None, TAIL_ROWS, D_MODEL), lambda i: (i, 0, 0)),
                   pl.BlockSpec(cache.shape, whole3), *cast_out),
        compiler_params=_params("arbitrary"),
        name=f"odd_{j}",
    )(x, x, gains, w_in, conv_w, w_out, cache, *(ffn_f32 * len(cast_targets)))
    return out[0], out[1], out[2], out[3:]


def _place_frames_body(x_ref, o_ref):
    o_ref[...] = x_ref[...]


def _place_frames(x_prompt):
    steps = SEQ // PLACE_ROWS
    return pl.pallas_call(
        _place_frames_body,
        out_shape=jax.ShapeDtypeStruct((N_ROWS, D_MODEL), F32),
        grid=(BATCH, steps),
        in_specs=[pl.BlockSpec((None, PLACE_ROWS, D_MODEL), lambda b, t: (b, t, 0))],
        out_specs=pl.BlockSpec((pl.Element(PLACE_ROWS), pl.Element(D_MODEL)),
                               lambda b, t: (pl.multiple_of(b * LP + PAD + N_META + t * PLACE_ROWS, CHUNK), 0)),
        compiler_params=_params("parallel", "parallel"),
        name="place_frames",
    )(x_prompt)


def _take_frames(x):
    steps = SEQ // PLACE_ROWS
    return pl.pallas_call(
        _place_frames_body,
        out_shape=jax.ShapeDtypeStruct((BATCH, SEQ, D_MODEL), F32),
        grid=(BATCH, steps),
        in_specs=[pl.BlockSpec((pl.Element(PLACE_ROWS), pl.Element(D_MODEL)),
                               lambda b, t: (pl.multiple_of(b * LP + PAD + N_META + t * PLACE_ROWS, CHUNK), 0))],
        out_specs=pl.BlockSpec((None, PLACE_ROWS, D_MODEL), lambda b, t: (b, t, 0)),
        compiler_params=_params("parallel", "parallel"),
        name="take_frames",
    )(x)


def _place_rest_body(x_ref, lead_ref, xs_ref, o_ref):
    del x_ref
    is_lead = pl.program_id(0) < BATCH

    @pl.when(is_lead)
    def _():
        o_ref[...] = lead_ref[...]

    @pl.when(jnp.logical_not(is_lead))
    def _():
        o_ref[...] = xs_ref[...]


def _place_rest(x, lead, x_sample):
    per_stream, first_sample = LP // CHUNK, P_ROWS // CHUNK
    sample_blocks = S_ROWS // CHUNK
    where = lambda t: (jnp.where(t < BATCH, t * per_stream, first_sample + t - BATCH), 0)
    return pl.pallas_call(
        _place_rest_body,
        out_shape=jax.ShapeDtypeStruct((N_ROWS, D_MODEL), F32),
        grid=(BATCH + sample_blocks,),
        in_specs=[pl.BlockSpec(memory_space=pl.ANY),
                  pl.BlockSpec((CHUNK, D_MODEL), lambda t: (0, 0)),
                  pl.BlockSpec((CHUNK, D_MODEL), lambda t: (jnp.maximum(t - BATCH, 0), 0))],
        out_specs=pl.BlockSpec((CHUNK, D_MODEL), where),
        input_output_aliases={0: 0},
        compiler_params=_params("arbitrary"),
        name="place_rest",
    )(x, lead, x_sample)


def _stream_tails(a, row0, streams, length, n):
    ends = [row0 + (s + 1) * length for s in range(streams)]
    return jnp.stack([lax.slice_in_dim(a, e - n, e, axis=0) for e in ends])


def kernel(x_prompt, x_sample, cache_swa_k, cache_swa_v, state_hgrn, cache_conv, meta_tokens, norm_gains,
           w_ffn_gate, w_ffn_up, w_ffn_down, w_in_even, w_out_even, attn_sinks, hgrn_lb_logits,
           hgrn_norm_gain, w_in_odd, conv_w, w_out_odd):
    win = cache_swa_k.shape[2]
    lead = jnp.concatenate([jnp.zeros((PAD, D_MODEL), F32), meta_tokens.astype(F32)], axis=0)
    x = _place_rest(_place_frames(x_prompt), lead, x_sample.reshape(S_ROWS, D_MODEL))

    ffn_f32 = (w_ffn_gate, w_ffn_up, w_ffn_down)
    ffn_w = {(0, 0): (w_ffn_gate[0, 0].astype(BF16), w_ffn_up[0, 0].astype(BF16),
                      w_ffn_down[0, 0].astype(BF16))}

    def keep_converted(targets, converted):
        for n, key in enumerate(targets):
            ffn_w[key] = tuple(converted[3 * n:3 * n + 3])

    w_in_e, w_out_e = w_in_even.astype(BF16), w_out_even.astype(BF16)
    w_in_o, w_out_o = w_in_odd.astype(BF16), w_out_odd.astype(BF16)
    cache_k = cache_swa_k.reshape(N_EVEN, DEC_BATCH, win, A_KV_W)
    cache_v = cache_swa_v.reshape(N_EVEN, DEC_BATCH, win, A_KV_W)

    k_p, v_p, s_p, c_p, k_s, v_s, s_s, c_s = ([] for _ in range(8))
    for layer in range(DEPTH):
        gains = norm_gains[layer]
        j = layer // 2
        x = _ffn(x, norm_gains, ffn_w[(layer, 0)], layer, 0)
        targets = [(layer, 1)] + ([(layer + 1, 0)] if layer + 1 < DEPTH else [])
        if layer % 2 == 0:
            q, k, v, kk, vv, hq, hk, hb, hv, hg = _even_in(x, gains, w_in_e, hgrn_lb_logits, j)
            oa = _attn(attn_sinks[j], q, kk, vv, cache_k[j], cache_v[j])
            ob, st_p, st_s, converted = _hgrn(hq, hk, hb, hv, hg, hgrn_norm_gain[j], state_hgrn[j],
                                              ffn_f32, targets)
            keep_converted(targets, converted)
            x = _even_out(x, gains, oa, ob, w_out_e, j)
            for new, tail_p, tail_s, cache in ((k, k_p, k_s, cache_k[j]), (v, v_p, v_s, cache_v[j])):
                tail_p.append(_stream_tails(new, 0, BATCH, LP, WINDOW).reshape(
                    BATCH, WINDOW, A_KV_HEADS, A_HEAD_DIM))
                both = jnp.concatenate([cache, new[P_ROWS:].reshape(DEC_BATCH, DEC_SEQ, A_KV_W)], axis=1)
                tail_s.append(both[:, both.shape[1] - win:].reshape(DEC_BATCH, win, A_KV_HEADS, A_HEAD_DIM))
            s_p.append(st_p)
            s_s.append(st_s)
        else:
            x, tails, conv_s, converted = _odd(x, gains, w_in_o, conv_w, w_out_o, cache_conv[j], j,
                                               ffn_f32, targets)
            keep_converted(targets, converted)
            per_stream = LP // ODD_TILE
            c_p.append(jnp.stack([tails[(b + 1) * per_stream - 1, TAIL_ROWS - (CONV_WIDTH - 1):]
                                  for b in range(BATCH)]))
            c_s.append(conv_s)
        x = _ffn(x, norm_gains, ffn_w[(layer, 1)], layer, 1)

    y_prompt = _take_frames(x)
    y_sample = x[P_ROWS:].reshape(DEC_BATCH, DEC_SEQ, D_MODEL)
    return (y_prompt, y_sample, jnp.stack(k_p), jnp.stack(v_p), jnp.stack(s_p), jnp.stack(c_p),
            jnp.stack(k_s), jnp.stack(v_s), jnp.stack(s_s), jnp.stack(c_s))
```

```python
import functools

import jax
import jax.numpy as jnp
from jax import lax
from jax.experimental import pallas as pl
from jax.experimental.pallas import tpu as pltpu

F32 = jnp.float32
BF16 = jnp.bfloat16

D_MODEL = 1024
BATCH = 4
SEQ = 4096
DEPTH = 4
DEC_BATCH = 8
DEC_SEQ = 32
CHUNK = 64
N_META = 16
WINDOW = 128
WINDOW_CHUNKS = WINDOW // CHUNK
A_HEADS = 8
A_KV_HEADS = 2
A_HEAD_DIM = 64
A_Q_W = A_HEADS * A_HEAD_DIM
A_KV_W = A_KV_HEADS * A_HEAD_DIM
B_HEADS = 4
B_KEY_DIM = 128
B_VAL_DIM = 128
B_QK_W = B_HEADS * B_KEY_DIM
B_V_W = B_HEADS * B_VAL_DIM
B_BLOCK = 16
CONV_WIDTH = 3
D_FF = 2816
FFN_RESIDUAL = 0.5
N_EVEN = (DEPTH + 1) // 2
N_ODD = DEPTH // 2
EVEN_IN_W = A_Q_W + 2 * A_KV_W + 2 * B_QK_W + 2 * B_V_W
EPS = 1e-6
MASK_VALUE = -1e30
LB_FLOOR = 1e-30

PAD = (-N_META) % CHUNK
LP = PAD + N_META + SEQ
N_CHUNKS = LP // CHUNK
P_ROWS = BATCH * LP
S_ROWS = DEC_BATCH * DEC_SEQ
N_ROWS = P_ROWS + S_ROWS

ROW_TILE = 512
FF_CHUNK = 256
HGRN_TILE = 832
HGRN_ROWS = 64
HGRN_SUB = 8
HGRN_GROUP = 3
LOG2_E = 1.4426950408889634
CUMSUM_ROWS = 256
FFN_TILE = 1536
FFN_SUB = 512
ATTN_CHUNKS = 13
PLACE_ROWS = 1024
CAST_SLAB = 256
CAST_STEPS = D_FF // CAST_SLAB
ODD_TILE = 832
ODD_HALO = 16
TAIL_ROWS = 8
VMEM_LIMIT = 56 * 1024 * 1024

assert N_ROWS % ROW_TILE == 0 and D_FF % FF_CHUNK == 0
assert LP % HGRN_TILE == 0 and HGRN_TILE % HGRN_ROWS == 0 and ROW_TILE % HGRN_ROWS == 0
assert HGRN_ROWS % DEC_SEQ == 0 and DEC_SEQ % HGRN_SUB == 0
assert ROW_TILE % CUMSUM_ROWS == 0 and CUMSUM_ROWS % HGRN_ROWS == 0
assert N_ROWS % FFN_TILE == 0 and FFN_TILE % FFN_SUB == 0 and N_CHUNKS % ATTN_CHUNKS == 0
assert D_FF % CAST_SLAB == 0 and CAST_STEPS <= min(P_ROWS // ODD_TILE, P_ROWS // HGRN_TILE)
assert LP % ODD_TILE == 0 and ODD_TILE % ODD_HALO == 0 and P_ROWS % DEC_SEQ == 0
assert P_ROWS % S_ROWS == 0 and S_ROWS <= ODD_TILE and TAIL_ROWS >= CONV_WIDTH - 1


def _params(*sem):
    return pltpu.CompilerParams(dimension_semantics=sem, vmem_limit_bytes=VMEM_LIMIT)


def _rms(x, gain):
    return x * lax.rsqrt(jnp.mean(x * x, axis=-1, keepdims=True) + EPS) * gain


def _silu(x):
    return x * jax.nn.sigmoid(x)


def _dot(a, b):
    return jnp.dot(a, b, preferred_element_type=F32)


def _dot_nt(a, b):
    return lax.dot_general(a, b, (((1,), (1,)), ((), ())), preferred_element_type=F32)


def _dot_tn(a, b):
    return lax.dot_general(a, b, (((0,), (0,)), ((), ())), preferred_element_type=F32)


def _is_pad_row(row0, rows):
    r = row0 + lax.broadcasted_iota(jnp.int32, (rows, 1), 0)
    pad = None
    for b in range(BATCH):
        hit = jnp.logical_and(r >= b * LP, r < b * LP + PAD)
        pad = hit if pad is None else jnp.logical_or(pad, hit)
    return pad


def _ffn_body(sel_ref, x_ref, g_ref, wg_ref, wu_ref, wd_ref, o_ref):
    pre = sel_ref[1] * 4
    post = pre + 1
    pieces = [slice(lo, lo + FFN_SUB) for lo in range(0, FFN_TILE, FFN_SUB)]
    chunks = [slice(c * FF_CHUNK, (c + 1) * FF_CHUNK) for c in range(D_FF // FF_CHUNK)]
    xs = [x_ref[rows, :] for rows in pieces]
    hs = [_rms(x, g_ref[pl.ds(pre, 1), :]).astype(BF16) for x in xs]
    accs = []
    for h in hs:
        acc = jnp.zeros((FFN_SUB, D_MODEL), F32)
        for cols in chunks:
            act = (_silu(_dot(h, wg_ref[:, cols])) * _dot(h, wu_ref[:, cols])).astype(BF16)
            acc = acc + _dot(act, wd_ref[cols, :])
        accs.append(acc)
    for rows, x, acc in zip(pieces, xs, accs):
        o_ref[rows, :] = x + FFN_RESIDUAL * _rms(acc, g_ref[pl.ds(post, 1), :])


def _ffn(x, norm_gains, weights, layer, which):
    row = lambda i, sel: (i, 0)
    whole = lambda i, sel: (0, 0)
    resident = pl.Buffered(1)
    grid_spec = pltpu.PrefetchScalarGridSpec(
        num_scalar_prefetch=1,
        grid=(N_ROWS // FFN_TILE,),
        in_specs=[
            pl.BlockSpec((FFN_TILE, D_MODEL), row),
            pl.BlockSpec((None, 6, D_MODEL), lambda i, sel: (sel[0], 0, 0)),
            pl.BlockSpec((D_MODEL, D_FF), whole, pipeline_mode=resident),
            pl.BlockSpec((D_MODEL, D_FF), whole, pipeline_mode=resident),
            pl.BlockSpec((D_FF, D_MODEL), whole, pipeline_mode=resident),
        ],
        out_specs=pl.BlockSpec((FFN_TILE, D_MODEL), row),
    )
    return pl.pallas_call(
        _ffn_body,
        out_shape=jax.ShapeDtypeStruct((N_ROWS, D_MODEL), F32),
        grid_spec=grid_spec,
        input_output_aliases={1: 0},
        compiler_params=_params("parallel"),
        name="ffn",
    )(jnp.array([layer, which], jnp.int32), x, norm_gains, *weights)


def _cast_plan(targets, step_of):
    slab = lambda *ids: jnp.minimum(step_of(*ids), CAST_STEPS - 1)
    in_specs, out_specs, shapes = [], [], []
    for layer, which in targets:
        for block, shape, place in (((D_MODEL, CAST_SLAB), (D_MODEL, D_FF), lambda s: (0, s)),
                                    ((D_MODEL, CAST_SLAB), (D_MODEL, D_FF), lambda s: (0, s)),
                                    ((CAST_SLAB, D_MODEL), (D_FF, D_MODEL), lambda s: (s, 0))):
            in_specs.append(pl.BlockSpec((None, None) + block,
                                         lambda *ids, l=layer, w=which, p=place: (l, w) + p(slab(*ids))))
            out_specs.append(pl.BlockSpec(block, lambda *ids, p=place: p(slab(*ids))))
            shapes.append(jax.ShapeDtypeStruct(shape, BF16))
    return in_specs, out_specs, shapes


def _cast_slabs(step, srcs, dsts):
    @pl.when(step < CAST_STEPS)
    def _():
        for src, dst in zip(srcs, dsts):
            dst[...] = src[...].astype(BF16)


def _lower_bound(lbl_ref, j):
    rows = [lbl_ref[i:i + 1, :] for i in range(N_EVEN)]
    top = functools.reduce(jnp.maximum, rows)
    e = [jnp.exp(r - top) for r in rows]
    total = functools.reduce(lambda a, b: a + b, e)
    p = [v / total for v in e]
    cum = functools.reduce(lambda a, b: a + b, p[:j + 1])
    return jnp.maximum(cum - p[0], 0.0)


def _span_cumsum(g, row0):
    out = []
    for lo in range(0, g.shape[0], CUMSUM_ROWS):
        part = g[lo:lo + CUMSUM_ROWS, :]
        t = row0 + lo + lax.broadcasted_iota(jnp.int32, (CUMSUM_ROWS, CUMSUM_ROWS), 0)
        u = row0 + lo + lax.broadcasted_iota(jnp.int32, (CUMSUM_ROWS, CUMSUM_ROWS), 1)
        span = jnp.where(t >= P_ROWS, DEC_SEQ, HGRN_ROWS)
        tri = jnp.where(jnp.logical_and((t ^ u) < span, u <= t), 1.0, 0.0).astype(BF16)
        total = None
        for _ in range(3):
            term = part.astype(BF16)
            part = part - term.astype(F32)
            total = _dot(tri, term) if total is None else total + _dot(tri, term)
        out.append(total)
    return jnp.concatenate(out, axis=0)


def _even_in_body(x_ref, g_ref, w_ref, lbl_ref, q_ref, k_ref, v_ref, kk_ref, vv_ref,
                  hq_ref, hk_ref, hb_ref, hv_ref, hg_ref, *, j):
    h = _rms(x_ref[...], g_ref[2:3, :]).astype(BF16)
    widths = (A_Q_W, 2 * A_KV_W, B_QK_W, B_QK_W, B_V_W, B_V_W)
    starts = [sum(widths[:i]) for i in range(len(widths))]
    proj = lambda i: _dot(h, w_ref[:, starts[i]:starts[i] + widths[i]])
    fx = proj(3)
    q_ref[...] = (proj(0) * (A_HEAD_DIM ** -0.5)).astype(BF16)
    kv = proj(1)
    k_ref[...] = kv[:, :A_KV_W]
    v_ref[...] = kv[:, A_KV_W:]
    kk_ref[...] = jnp.concatenate(_half_lane_variants(kv[:, :A_KV_W]), axis=1)
    vv_ref[...] = jnp.concatenate(_half_lane_variants(kv[:, A_KV_W:]), axis=1)
    hq_ref[...] = _silu(proj(2))
    hv_ref[...] = proj(4).astype(BF16)
    hg_ref[...] = _silu(proj(5))
    lb = _lower_bound(lbl_ref, j)
    soft = jnp.log1p(jnp.exp(-jnp.abs(fx)))
    ls_pos = jnp.minimum(fx, 0.0) - soft
    ls_neg = jnp.minimum(-fx, 0.0) - soft
    other = jnp.log(jnp.maximum(lb, LB_FLOOR)) + ls_neg
    log_f = jnp.maximum(ls_pos, other) + jnp.log1p(jnp.exp(-jnp.abs(ls_pos - other)))
    hk_ref[...] = (1.0 - lb) * jax.nn.sigmoid(-fx)
    hb_ref[...] = _span_cumsum(log_f, pl.program_id(0) * ROW_TILE)


def _even_in(x, gains, w_in, lb_logits, j):
    row = lambda i: (i, 0)
    whole = lambda i: (0, 0)
    wide = lambda w, dt: jax.ShapeDtypeStruct((N_ROWS, w), dt)
    spec = lambda w: pl.BlockSpec((ROW_TILE, w), row)
    return pl.pallas_call(
        functools.partial(_even_in_body, j=j),
        out_shape=(wide(A_Q_W, BF16), wide(A_KV_W, F32), wide(A_KV_W, F32),
                   wide(4 * A_KV_W, BF16), wide(4 * A_KV_W, BF16),
                   wide(B_QK_W, F32), wide(B_QK_W, F32), wide(B_QK_W, F32),
                   wide(B_V_W, BF16), wide(B_V_W, F32)),
        grid=(N_ROWS // ROW_TILE,),
        in_specs=[
            pl.BlockSpec((ROW_TILE, D_MODEL), row),
            pl.BlockSpec((6, D_MODEL), whole),
            pl.BlockSpec((None, D_MODEL, EVEN_IN_W), lambda i: (j, 0, 0),
                         pipeline_mode=pl.Buffered(1)),
            pl.BlockSpec((N_EVEN, B_QK_W), whole),
        ],
        out_specs=(spec(A_Q_W), spec(A_KV_W), spec(A_KV_W), spec(4 * A_KV_W), spec(4 * A_KV_W),
                   spec(B_QK_W), spec(B_QK_W), spec(B_QK_W), spec(B_V_W), spec(B_V_W)),
        compiler_params=_params("parallel"),
        name=f"even_in_{j}",
    )(x, gains, w_in, lb_logits)


def _half_lane_variants(a):
    lane = lax.broadcasted_iota(jnp.int32, a.shape, 1)
    low = lane < A_HEAD_DIM
    swapped = pltpu.roll(a, A_HEAD_DIM, 1)
    zero = jnp.zeros_like(a)
    return (jnp.where(low, a, zero).astype(BF16), jnp.where(low, zero, swapped).astype(BF16),
            jnp.where(low, swapped, zero).astype(BF16), jnp.where(low, zero, a).astype(BF16))


def _sink_softmax(s, valid, sink):
    if valid is not None:
        s = jnp.where(valid, s, MASK_VALUE)
    m = jnp.maximum(jnp.max(s, axis=-1, keepdims=True), sink)
    p = jnp.exp(s - m)
    denom = jnp.sum(p, axis=-1, keepdims=True) + jnp.exp(sink - m)
    return (p / denom).astype(BF16)


def _group_sinks(sink_ref, kvh, rows):
    first = lax.broadcasted_iota(jnp.int32, (2 * rows, 1), 0) < rows
    base = kvh * (A_HEADS // A_KV_HEADS)
    return (jnp.where(first, sink_ref[base], sink_ref[base + 2]),
            jnp.where(first, sink_ref[base + 1], sink_ref[base + 3]))


def _attend(items, sinks):
    scores = []
    for q, keys, _, _ in items:
        for kvh in range(A_KV_HEADS):
            c0 = kvh * 256
            q2 = jnp.concatenate([q[:, c0:c0 + 128], q[:, c0 + 128:c0 + 256]], axis=0)
            scores += [_dot_nt(q2, keys[2 * kvh]), _dot_nt(q2, keys[2 * kvh + 1])]
    weights = []
    for n, (_, _, _, valid) in enumerate(items):
        for kvh in range(A_KV_HEADS):
            for half in range(2):
                weights.append(_sink_softmax(scores[4 * n + 2 * kvh + half], valid, sinks[kvh][half]))
    outs = []
    for n, (q, _, vals, _) in enumerate(items):
        rows = q.shape[0]
        groups = []
        for kvh in range(A_KV_HEADS):
            w = weights[4 * n + 2 * kvh:4 * n + 2 * kvh + 2]
            o = _dot(w[0], vals[2 * kvh]) + _dot(w[1], vals[2 * kvh + 1])
            groups.append(jnp.concatenate([o[:rows], o[rows:]], axis=1))
        outs.append(jnp.concatenate(groups, axis=1))
    return outs


def _attn_body(sink_ref, q_ref, kk_ref, vv_ref, ck_ref, cv_ref, o_ref):
    variant = [slice(v * A_KV_W, (v + 1) * A_KV_W) for v in range(4)]

    @pl.when(pl.program_id(0) < BATCH)
    def _():
        _attn_prompt_stream(sink_ref, q_ref, kk_ref, vv_ref, o_ref, variant)

    @pl.when(pl.program_id(0) == BATCH)
    def _():
        sinks = [_group_sinks(sink_ref, kvh, DEC_SEQ) for kvh in range(A_KV_HEADS)]
        items = []
        for b in range(DEC_BATCH):
            rows = slice(b * DEC_SEQ, (b + 1) * DEC_SEQ)
            old_k, old_v = _half_lane_variants(ck_ref[b]), _half_lane_variants(cv_ref[b])
            keys = [jnp.concatenate([old_k[v], kk_ref[rows, variant[v]]], axis=0) for v in range(4)]
            vals = [jnp.concatenate([old_v[v], vv_ref[rows, variant[v]]], axis=0) for v in range(4)]
            items.append((q_ref[rows, :], keys, vals, None))
        for b, out in enumerate(_attend(items, sinks)):
            o_ref[b * DEC_SEQ:(b + 1) * DEC_SEQ, :] = out.astype(BF16)


def _attn_prompt_stream(sink_ref, q_ref, kk_ref, vv_ref, o_ref, variant):
    span = (WINDOW_CHUNKS + 1) * CHUNK
    sinks = [_group_sinks(sink_ref, kvh, CHUNK) for kvh in range(A_KV_HEADS)]

    def step(n, carry):
        starts, items = [], []
        for i in range(ATTN_CHUNKS):
            c = n * ATTN_CHUNKS + i
            r0 = pl.multiple_of(c * CHUNK, CHUNK)
            s0 = pl.multiple_of(jnp.maximum(c - WINDOW_CHUNKS, 0) * CHUNK, CHUNK)
            kpos = s0 + lax.broadcasted_iota(jnp.int32, (1, span), 1)
            valid = jnp.logical_and(kpos >= PAD, kpos < r0 + CHUNK)
            starts.append(r0)
            items.append((q_ref[pl.ds(r0, CHUNK), :],
                          [kk_ref[pl.ds(s0, span), v] for v in variant],
                          [vv_ref[pl.ds(s0, span), v] for v in variant], valid))
        for r0, out in zip(starts, _attend(items, sinks)):
            o_ref[pl.ds(r0, CHUNK), :] = out.astype(BF16)
        return carry

    lax.fori_loop(0, N_CHUNKS // ATTN_CHUNKS, step, 0)


def _attn(sink, q, kk, vv, cache_k, cache_v):
    seq = lambda b: (b, 0)
    old = pl.BlockSpec(cache_k.shape, lambda b: (0, 0, 0))
    return pl.pallas_call(
        _attn_body,
        out_shape=jax.ShapeDtypeStruct((N_ROWS, A_Q_W), BF16),
        grid=(BATCH + 1,),
        in_specs=[
            pl.BlockSpec(memory_space=pltpu.SMEM),
            pl.BlockSpec((LP, A_Q_W), seq),
            pl.BlockSpec((LP, 4 * A_KV_W), seq),
            pl.BlockSpec((LP, 4 * A_KV_W), seq),
            old, old,
        ],
        out_specs=pl.BlockSpec((LP, A_Q_W), seq),
        compiler_params=_params("parallel"),
        name="attn",
    )(sink, q, kk, vv, cache_k, cache_v)


def _group_row(a, group, row):
    spans = a.shape[0] // group
    picked = a.reshape(spans, group, a.shape[1])[:, row:row + 1, :]
    return jnp.broadcast_to(picked, (spans, group, a.shape[1])).reshape(a.shape)


def _hgrn_pairs(q, k, c2):
    parts = []
    for s in range(HGRN_SUB):
        decay = jnp.exp2(jnp.minimum(c2 - _group_row(c2, HGRN_SUB, s), 0.0))
        parts.append((q * _group_row(k, HGRN_SUB, s) * decay).astype(BF16))
    return jnp.concatenate(parts, axis=1)


def _hgrn_pick(rows):
    shape = (HGRN_SUB * B_KEY_DIM, rows)
    block = lax.broadcasted_iota(jnp.int32, shape, 0) // B_KEY_DIM
    col = lax.broadcasted_iota(jnp.int32, shape, 1) & (HGRN_SUB - 1)
    return jnp.where(block == col, 1.0, 0.0).astype(BF16)


def _hgrn_head(q, c, v, state_t, intra, levels, update):
    rows = q.shape[0]
    t_idx = lax.broadcasted_iota(jnp.int32, (rows, rows), 0)
    u_idx = lax.broadcasted_iota(jnp.int32, (rows, rows), 1)
    apart = t_idx ^ u_idx
    in_group = jnp.logical_and(apart < HGRN_SUB, u_idx <= t_idx)
    scores = jnp.where(in_group, intra, 0.0)
    for w, level in levels:
        scores = scores + (level if 2 * w == rows else jnp.where(apart < 2 * w, level, 0.0))
    o = _dot(scores.astype(BF16), v) + _dot_nt((q * jnp.exp2(c)).astype(BF16), state_t.astype(BF16))
    return o, state_t * jnp.exp2(c[rows - 1:rows, :]) + update


def _hgrn_levels(q, k, c):
    rows = q.shape[0]
    pos = lax.broadcasted_iota(jnp.int32, q.shape, 0)
    levels = []
    w = HGRN_SUB
    while w < rows:
        rho = _group_row(c, 2 * w, w - 1)
        upper = (pos & w) != 0
        q_up = jnp.where(upper, q * jnp.exp2(jnp.minimum(c - rho, 0.0)), 0.0).astype(BF16)
        k_low = jnp.where(upper, 0.0, k * jnp.exp2(jnp.minimum(rho - c, 0.0))).astype(BF16)
        levels.append((w, _dot_nt(q_up, k_low)))
        w *= 2
    return levels


def _hgrn_tiles(tiles, n, hq_ref, hk_ref, hc_ref, hv_ref, hg_ref, gain_ref, o_ref, st_ref, pick):
    loaded = [(hq_ref[r, :], hk_ref[r, :], hc_ref[r, :] * LOG2_E, hv_ref[r, :], hg_ref[r, :])
              for r in tiles]
    states = [st_ref[hd] for hd in range(B_HEADS)]
    heads = [slice(hd * B_KEY_DIM, (hd + 1) * B_KEY_DIM) for hd in range(B_HEADS)]
    intras = [_dot(jnp.concatenate([_hgrn_pairs(q[:, h], k[:, h], c[:, h]) for h in heads], axis=0), pick)
              for q, k, c, _, _ in loaded]
    levels = [[_hgrn_levels(q[:, h], k[:, h], c[:, h]) for h in heads] for q, k, c, _, _ in loaded]
    updates = [[_dot_tn(v[:, h], (k[:, h] * jnp.exp2(c[n - 1:n, h] - c[:, h])).astype(BF16))
                for h in heads] for _, k, c, v, _ in loaded]
    results = []
    for t, (q, k, c, v, g) in enumerate(loaded):
        outs = []
        for hd, h in enumerate(heads):
            o, states[hd] = _hgrn_head(q[:, h], c[:, h], v[:, h], states[hd],
                                       intras[t][hd * n:(hd + 1) * n, :], levels[t][hd], updates[t][hd])
            outs.append(_rms(o, gain_ref[hd:hd + 1, :]) * g[:, h])
        results.append(jnp.concatenate(outs, axis=1).astype(BF16))
    for r, res in zip(tiles, results):
        o_ref[r, :] = res
    for hd in range(B_HEADS):
        st_ref[hd] = states[hd]


def _hgrn_body(*refs, n_cast):
    hq_ref, hk_ref, hc_ref, hv_ref, hg_ref, gain_ref, s0_ref = refs[:7]
    o_ref, sp_ref, ss_ref = refs[7 + n_cast:10 + n_cast]
    st_ref = refs[-1]
    step = pl.program_id(0)
    tiles = LP // HGRN_TILE
    tile = lax.rem(step, tiles)
    is_prompt = step < BATCH * tiles
    _cast_slabs(step, refs[7:7 + n_cast], refs[10 + n_cast:10 + 2 * n_cast])
    mixer = (hq_ref, hk_ref, hc_ref, hv_ref, hg_ref, gain_ref, o_ref, st_ref)

    @pl.when(jnp.logical_and(is_prompt, tile == 0))
    def _():
        st_ref[...] = jnp.zeros(st_ref.shape, F32)

    @pl.when(is_prompt)
    def _():
        pick = _hgrn_pick(HGRN_ROWS)
        n_tiles = HGRN_TILE // HGRN_ROWS

        def group(n, carry):
            first = n * HGRN_GROUP
            spans = [pl.ds(pl.multiple_of((first + i) * HGRN_ROWS, HGRN_ROWS), HGRN_ROWS)
                     for i in range(HGRN_GROUP)]
            _hgrn_tiles(spans, HGRN_ROWS, *mixer, pick)
            return carry

        lax.fori_loop(0, n_tiles // HGRN_GROUP, group, 0)
        rest = [pl.ds(i * HGRN_ROWS, HGRN_ROWS) for i in range(n_tiles - n_tiles % HGRN_GROUP, n_tiles)]
        if rest:
            _hgrn_tiles(rest, HGRN_ROWS, *mixer, pick)

    @pl.when(jnp.logical_and(is_prompt, tile == tiles - 1))
    def _():
        for hd in range(B_HEADS):
            sp_ref[hd] = st_ref[hd].T

    @pl.when(jnp.logical_not(is_prompt))
    def _():
        pick = _hgrn_pick(DEC_SEQ)
        for b in range(DEC_BATCH):
            for hd in range(B_HEADS):
                st_ref[hd] = s0_ref[b, hd].T
            _hgrn_tiles([pl.ds(b * DEC_SEQ, DEC_SEQ)], DEC_SEQ, *mixer, pick)
            for hd in range(B_HEADS):
                ss_ref[b, hd] = st_ref[hd].T


def _hgrn(hq, hk, hb, hv, hg, gain, s0, ffn_f32, cast_targets):
    tiles = LP // HGRN_TILE
    row = lambda s: (s, 0)
    spec = pl.BlockSpec((HGRN_TILE, B_QK_W), row)
    state = (B_HEADS, B_KEY_DIM, B_VAL_DIM)
    cast_in, cast_out, cast_shapes = _cast_plan(cast_targets, lambda s: s)
    out = pl.pallas_call(
        functools.partial(_hgrn_body, n_cast=len(cast_in)),
        out_shape=(jax.ShapeDtypeStruct((N_ROWS, B_V_W), BF16),
                   jax.ShapeDtypeStruct((BATCH,) + state, F32),
                   jax.ShapeDtypeStruct((DEC_BATCH,) + state, F32), *cast_shapes),
        grid=(BATCH * tiles + 1,),
        in_specs=[spec, spec, spec, spec, spec,
                  pl.BlockSpec((B_HEADS, B_VAL_DIM), lambda s: (0, 0)),
                  pl.BlockSpec((DEC_BATCH,) + state, lambda s: (0, 0, 0, 0)), *cast_in],
        out_specs=(spec,
                   pl.BlockSpec((None,) + state, lambda s: (jnp.minimum(s // tiles, BATCH - 1), 0, 0, 0)),
                   pl.BlockSpec((DEC_BATCH,) + state, lambda s: (0, 0, 0, 0)), *cast_out),
        scratch_shapes=[pltpu.VMEM((B_HEADS, B_VAL_DIM, B_KEY_DIM), F32)],
        compiler_params=_params("arbitrary"),
        name="hgrn",
    )(hq, hk, hb, hv, hg, gain, s0, *(ffn_f32 * len(cast_targets)))
    return out[0], out[1], out[2], out[3:]


def _even_out_body(x_ref, g_ref, oa_ref, ob_ref, w_ref, o_ref):
    for lo in range(0, FFN_TILE, FFN_SUB):
        rows = slice(lo, lo + FFN_SUB)
        m = _dot(oa_ref[rows, :], w_ref[:A_Q_W, :]) + _dot(ob_ref[rows, :], w_ref[A_Q_W:, :])
        y = x_ref[rows, :] + _rms(m, g_ref[3:4, :])
        pad = _is_pad_row(pl.program_id(0) * FFN_TILE + lo, FFN_SUB)
        o_ref[rows, :] = jnp.where(pad, 0.0, y)


def _even_out(x, gains, oa, ob, w_out, j):
    row = lambda i: (i, 0)
    return pl.pallas_call(
        _even_out_body,
        out_shape=jax.ShapeDtypeStruct((N_ROWS, D_MODEL), F32),
        grid=(N_ROWS // FFN_TILE,),
        in_specs=[
            pl.BlockSpec((FFN_TILE, D_MODEL), row),
            pl.BlockSpec((6, D_MODEL), lambda i: (0, 0)),
            pl.BlockSpec((FFN_TILE, A_Q_W), row),
            pl.BlockSpec((FFN_TILE, B_V_W), row),
            pl.BlockSpec((None, A_Q_W + B_V_W, D_MODEL), lambda i: (j, 0, 0),
                         pipeline_mode=pl.Buffered(1)),
        ],
        out_specs=pl.BlockSpec((FFN_TILE, D_MODEL), row),
        input_output_aliases={0: 0},
        compiler_params=_params("parallel"),
        name=f"even_out_{j}",
    )(x, gains, oa, ob, w_out)


def _conv3(u, cw_ref, before1=None, before2=None):
    shift1, shift2 = pltpu.roll(u, 1, 0), pltpu.roll(u, 2, 0)
    if before1 is not None:
        pos = lax.broadcasted_iota(jnp.int32, (u.shape[0], 1), 0)
        shift1 = jnp.where(pos == 0, before1, shift1)
        shift2 = jnp.where(pos == 0, before2, jnp.where(pos == 1, before1, shift2))
    return shift2 * cw_ref[0:1, :] + shift1 * cw_ref[1:2, :] + u * cw_ref[2:3, :]


def _odd_body(*refs, n_cast):
    x_ref, halo_ref, g_ref, w_in_ref, cw_ref, w_out_ref, cache_ref = refs[:7]
    o_ref, tail_ref, new_cache_ref = refs[7 + n_cast:10 + n_cast]
    _cast_slabs(pl.program_id(0), refs[7:7 + n_cast], refs[10 + n_cast:])
    is_prompt = pl.program_id(0) < P_ROWS // ODD_TILE

    @pl.when(is_prompt)
    def _():
        x = x_ref[...]
        h = _rms(x, g_ref[2:3, :]).astype(BF16)
        h_ext = jnp.concatenate([_rms(halo_ref[...], g_ref[2:3, :]).astype(BF16), h], axis=0)
        bg = _dot(h, w_in_ref[:, :D_MODEL])
        u_ext = _dot(h_ext, w_in_ref[:, D_MODEL:2 * D_MODEL]) * _dot(h_ext, w_in_ref[:, 2 * D_MODEL:])
        tail_ref[...] = u_ext[ODD_HALO + ODD_TILE - TAIL_ROWS:, :]
        y = _conv3(u_ext, cw_ref)[ODD_HALO:, :]
        m = _dot((bg * y).astype(BF16), w_out_ref[...])
        out = x + _rms(m, g_ref[3:4, :])
        o_ref[...] = jnp.where(_is_pad_row(pl.program_id(0) * ODD_TILE, ODD_TILE), 0.0, out)

    @pl.when(jnp.logical_not(is_prompt))
    def _():
        x = x_ref[:S_ROWS, :]
        h = _rms(x, g_ref[2:3, :]).astype(BF16)
        bg = _dot(h, w_in_ref[:, :D_MODEL])
        u = _dot(h, w_in_ref[:, D_MODEL:2 * D_MODEL]) * _dot(h, w_in_ref[:, 2 * D_MODEL:])
        ys = []
        for b in range(DEC_BATCH):
            ub = u[b * DEC_SEQ:(b + 1) * DEC_SEQ, :]
            ys.append(_conv3(ub, cw_ref, before1=cache_ref[b, 1:2, :], before2=cache_ref[b, 0:1, :]))
            new_cache_ref[b] = ub[DEC_SEQ - (CONV_WIDTH - 1):, :]
        m = _dot((bg * jnp.concatenate(ys, axis=0)).astype(BF16), w_out_ref[...])
        o_ref[:S_ROWS, :] = x + _rms(m, g_ref[3:4, :])
        tail_ref[...] = jnp.zeros(tail_ref.shape, F32)


def _odd(x, gains, w_in, conv_w, w_out, cache, j, ffn_f32, cast_targets):
    tiles = pl.cdiv(N_ROWS, ODD_TILE)
    row = lambda i: (i, 0)
    per = ODD_TILE // ODD_HALO
    whole3 = lambda i: (0, 0, 0)
    halo = lambda i: (jnp.maximum(i * per - 1, 0), 0)
    cast_in, cast_out, cast_shapes = _cast_plan(cast_targets, lambda i: i)
    out = pl.pallas_call(
        functools.partial(_odd_body, n_cast=len(cast_in)),
        out_shape=(jax.ShapeDtypeStruct((N_ROWS, D_MODEL), F32),
                   jax.ShapeDtypeStruct((tiles, TAIL_ROWS, D_MODEL), F32),
                   jax.ShapeDtypeStruct(cache.shape, F32), *cast_shapes),
        grid=(tiles,),
        in_specs=[
            pl.BlockSpec((ODD_TILE, D_MODEL), row),
            pl.BlockSpec((ODD_HALO, D_MODEL), halo),
            pl.BlockSpec((6, D_MODEL), lambda i: (0, 0)),
            pl.BlockSpec((None, D_MODEL, 3 * D_MODEL), lambda i: (j, 0, 0), pipeline_mode=pl.Buffered(1)),
            pl.BlockSpec((None, CONV_WIDTH, D_MODEL), lambda i: (j, 0, 0)),
            pl.BlockSpec((None, D_MODEL, D_MODEL), lambda i: (j, 0, 0), pipeline_mode=pl.Buffered(1)),
            pl.BlockSpec(cache.shape, whole3),
            *cast_in,
        ],
        out_specs=(pl.BlockSpec((ODD_TILE, D_MODEL), row),
                   pl.BlockSpec((None, TAIL_ROWS, D_MODEL), lambda i: (i, 0, 0)),
                   pl.BlockSpec(cache.shape, whole3), *cast_out),
        compiler_params=_params("arbitrary"),
        name=f"odd_{j}",
    )(x, x, gains, w_in, conv_w, w_out, cache, *(ffn_f32 * len(cast_targets)))
    return out[0], out[1], out[2], out[3:]


def _place_frames_body(x_ref, o_ref):
    o_ref[...] = x_ref[...]


def _place_frames(x_prompt):
    steps = SEQ // PLACE_ROWS
    return pl.pallas_call(
        _place_frames_body,
        out_shape=jax.ShapeDtypeStruct((N_ROWS, D_MODEL), F32),
        grid=(BATCH, steps),
        in_specs=[pl.BlockSpec((None, PLACE_ROWS, D_MODEL), lambda b, t: (b, t, 0))],
        out_specs=pl.BlockSpec((pl.Element(PLACE_ROWS), pl.Element(D_MODEL)),
                               lambda b, t: (pl.multiple_of(b * LP + PAD + N_META + t * PLACE_ROWS, CHUNK), 0)),
        compiler_params=_params("parallel", "parallel"),
        name="place_frames",
    )(x_prompt)


def _take_frames(x):
    steps = SEQ // PLACE_ROWS
    return pl.pallas_call(
        _place_frames_body,
        out_shape=jax.ShapeDtypeStruct((BATCH, SEQ, D_MODEL), F32),
        grid=(BATCH, steps),
        in_specs=[pl.BlockSpec((pl.Element(PLACE_ROWS), pl.Element(D_MODEL)),
                               lambda b, t: (pl.multiple_of(b * LP + PAD + N_META + t * PLACE_ROWS, CHUNK), 0))],
        out_specs=pl.BlockSpec((None, PLACE_ROWS, D_MODEL), lambda b, t: (b, t, 0)),
        compiler_params=_params("parallel", "parallel"),
        name="take_frames",
    )(x)


def _place_rest_body(x_ref, lead_ref, xs_ref, o_ref):
    del x_ref
    is_lead = pl.program_id(0) < BATCH

    @pl.when(is_lead)
    def _():
        o_ref[...] = lead_ref[...]

    @pl.when(jnp.logical_not(is_lead))
    def _():
        o_ref[...] = xs_ref[...]


def _place_rest(x, lead, x_sample):
    per_stream, first_sample = LP // CHUNK, P_ROWS // CHUNK
    sample_blocks = S_ROWS // CHUNK
    where = lambda t: (jnp.where(t < BATCH, t * per_stream, first_sample + t - BATCH), 0)
    return pl.pallas_call(
        _place_rest_body,
        out_shape=jax.ShapeDtypeStruct((N_ROWS, D_MODEL), F32),
        grid=(BATCH + sample_blocks,),
        in_specs=[pl.BlockSpec(memory_space=pl.ANY),
                  pl.BlockSpec((CHUNK, D_MODEL), lambda t: (0, 0)),
                  pl.BlockSpec((CHUNK, D_MODEL), lambda t: (jnp.maximum(t - BATCH, 0), 0))],
        out_specs=pl.BlockSpec((CHUNK, D_MODEL), where),
        input_output_aliases={0: 0},
        compiler_params=_params("arbitrary"),
        name="place_rest",
    )(x, lead, x_sample)


def _stream_tails(a, row0, streams, length, n):
    ends = [row0 + (s + 1) * length for s in range(streams)]
    return jnp.stack([lax.slice_in_dim(a, e - n, e, axis=0) for e in ends])


def kernel(x_prompt, x_sample, cache_swa_k, cache_swa_v, state_hgrn, cache_conv, meta_tokens, norm_gains,
           w_ffn_gate, w_ffn_up, w_ffn_down, w_in_even, w_out_even, attn_sinks, hgrn_lb_logits,
           hgrn_norm_gain, w_in_odd, conv_w, w_out_odd):
    win = cache_swa_k.shape[2]
    lead = jnp.concatenate([jnp.zeros((PAD, D_MODEL), F32), meta_tokens.astype(F32)], axis=0)
    x = _place_rest(_place_frames(x_prompt), lead, x_sample.reshape(S_ROWS, D_MODEL))

    ffn_f32 = (w_ffn_gate, w_ffn_up, w_ffn_down)
    ffn_w = {(0, 0): (w_ffn_gate[0, 0].astype(BF16), w_ffn_up[0, 0].astype(BF16),
                      w_ffn_down[0, 0].astype(BF16))}

    def keep_converted(targets, converted):
        for n, key in enumerate(targets):
            ffn_w[key] = tuple(converted[3 * n:3 * n + 3])

    w_in_e, w_out_e = w_in_even.astype(BF16), w_out_even.astype(BF16)
    w_in_o, w_out_o = w_in_odd.astype(BF16), w_out_odd.astype(BF16)
    cache_k = cache_swa_k.reshape(N_EVEN, DEC_BATCH, win, A_KV_W)
    cache_v = cache_swa_v.reshape(N_EVEN, DEC_BATCH, win, A_KV_W)

    k_p, v_p, s_p, c_p, k_s, v_s, s_s, c_s = ([] for _ in range(8))
    for layer in range(DEPTH):
        gains = norm_gains[layer]
        j = layer // 2
        x = _ffn(x, norm_gains, ffn_w[(layer, 0)], layer, 0)
        targets = [(layer, 1)] + ([(layer + 1, 0)] if layer + 1 < DEPTH else [])
        if layer % 2 == 0:
            q, k, v, kk, vv, hq, hk, hb, hv, hg = _even_in(x, gains, w_in_e, hgrn_lb_logits, j)
            oa = _attn(attn_sinks[j], q, kk, vv, cache_k[j], cache_v[j])
            ob, st_p, st_s, converted = _hgrn(hq, hk, hb, hv, hg, hgrn_norm_gain[j], state_hgrn[j],
                                              ffn_f32, targets)
            keep_converted(targets, converted)
            x = _even_out(x, gains, oa, ob, w_out_e, j)
            for new, tail_p, tail_s, cache in ((k, k_p, k_s, cache_k[j]), (v, v_p, v_s, cache_v[j])):
                tail_p.append(_stream_tails(new, 0, BATCH, LP, WINDOW).reshape(
                    BATCH, WINDOW, A_KV_HEADS, A_HEAD_DIM))
                both = jnp.concatenate([cache, new[P_ROWS:].reshape(DEC_BATCH, DEC_SEQ, A_KV_W)], axis=1)
                tail_s.append(both[:, both.shape[1] - win:].reshape(DEC_BATCH, win, A_KV_HEADS, A_HEAD_DIM))
            s_p.append(st_p)
            s_s.append(st_s)
        else:
            x, tails, conv_s, converted = _odd(x, gains, w_in_o, conv_w, w_out_o, cache_conv[j], j,
                                               ffn_f32, targets)
            keep_converted(targets, converted)
            per_stream = LP // ODD_TILE
            c_p.append(jnp.stack([tails[(b + 1) * per_stream - 1, TAIL_ROWS - (CONV_WIDTH - 1):]
                                  for b in range(BATCH)]))
            c_s.append(conv_s)
        x = _ffn(x, norm_gains, ffn_w[(layer, 1)], layer, 1)

    y_prompt = _take_frames(x)
    y_sample = x[P_ROWS:].reshape(DEC_BATCH, DEC_SEQ, D_MODEL)
    return (y_prompt, y_sample, jnp.stack(k_p), jnp.stack(v_p), jnp.stack(s_p), jnp.stack(c_p),
            jnp.stack(k_s), jnp.stack(v_s), jnp.stack(s_s), jnp.stack(c_s))
```
